```python
import math
import jax, jax.numpy as jnp
from jax import lax
import numpy as np

D_MODEL = 1024
BATCH = 8
SEQ = 2048
DEPTH = 2

N_A_LAYERS = DEPTH // 2
N_B_LAYERS = DEPTH - N_A_LAYERS
RET_HEADS = 4
RET_QK_DIM = D_MODEL // RET_HEADS
RET_V_DIM = 2 * RET_QK_DIM
RET_CHUNK = 128
DIFF_HEAD_DIM = 64
DIFF_HEADS = D_MODEL // (2 * DIFF_HEAD_DIM)
Q_BLOCK = 128
D_FF = ((8 * D_MODEL // 3 + 127) // 128) * 128
CONV_WIDTH = 3
ROPE_BASE = 10000.0
EPS = 1e-6

kernel_name = "yoco_retention_diffattn_convffn"


def rms_norm(x, g=None):
    xf = x.astype(jnp.float32)
    y = xf * lax.rsqrt(jnp.mean(xf * xf, axis=-1, keepdims=True) + EPS)
    if g is not None:
        y = y * g.astype(jnp.float32)
    return y.astype(x.dtype)


def rotary(x, pos):
    half = x.shape[-1] // 2
    inv = 1.0 / (ROPE_BASE ** jnp.linspace(0.0, 1.0, half, dtype=jnp.float32))
    ang = pos.astype(jnp.float32)[:, None] * inv[None, :]
    cos = jnp.cos(ang)[None, :, None, :]
    sin = jnp.sin(ang)[None, :, None, :]
    xf = x.astype(jnp.float32)
    x1, x2 = xf[..., :half], xf[..., half:]
    return jnp.concatenate([x1 * cos - x2 * sin, x1 * sin + x2 * cos], axis=-1).astype(x.dtype)


def retention(xn, w_in, w_out):
    B, S, _ = xn.shape
    H, dk, dv, C = RET_HEADS, RET_QK_DIM, RET_V_DIM, RET_CHUNK
    dt = xn.dtype
    proj = xn @ w_in
    q, k, v, g = jnp.split(proj, [H * dk, 2 * H * dk, 2 * H * dk + H * dv], axis=-1)
    pos = jnp.arange(S)
    q = rotary(q.reshape(B, S, H, dk), pos)
    k = rotary(k.reshape(B, S, H, dk), pos) * (dk ** -0.5)
    v = v.reshape(B, S, H, dv)

    log_gamma = jnp.log1p(-jnp.power(2.0, -5.0 - jnp.arange(H, dtype=jnp.float32)))
    idx = jnp.arange(C, dtype=jnp.float32)
    rel = idx[:, None] - idx[None, :]
    decay_mask = jnp.where(rel[None] >= 0,
                           jnp.exp(log_gamma[:, None, None] * jnp.maximum(rel, 0.0)[None]),
                           0.0).astype(dt)
    q_decay = jnp.exp(log_gamma[None, :] * (idx[:, None] + 1.0)).astype(dt)
    k_decay = jnp.exp(log_gamma[None, :] * (C - 1.0 - idx[:, None])).astype(dt)
    chunk_decay = jnp.exp(log_gamma * C).astype(dt)

    n = S // C

    def to_chunks(t):
        return jnp.moveaxis(t.reshape(B, n, C, H, t.shape[-1]), 1, 0)

    def step(state, qkv):
        qc, kc, vc = qkv
        s = jnp.einsum('bihd,bjhd->bhij', qc, kc) * decay_mask[None]
        inner = jnp.einsum('bhij,bjhe->bihe', s, vc)
        cross = jnp.einsum('bihd,bhde->bihe', qc * q_decay[None, :, :, None], state)
        new_state = (state * chunk_decay[None, :, None, None]
                     + jnp.einsum('bjhd,bjhe->bhde', kc * k_decay[None, :, :, None], vc))
        return new_state, inner + cross

    state0 = jnp.zeros((B, H, dk, dv), dt)
    _, out = lax.scan(step, state0, (to_chunks(q), to_chunks(k), to_chunks(v)))
    out = jnp.moveaxis(out, 0, 1).reshape(B, S, H, dv)
    out = rms_norm(out).reshape(B, S, H * dv)
    return (jax.nn.silu(g) * out) @ w_out


def diff_attention(xn, k_sh, v_sh, w_q, lam_params, subln_g, w_out, lambda_init):
    B, S, _ = xn.shape
    H, d = DIFF_HEADS, DIFF_HEAD_DIM
    q = (xn @ w_q).reshape(B, S, H, 2, d) * (d ** -0.5)
    nb = S // Q_BLOCK
    q_blocks = jnp.moveaxis(q.reshape(B, nb, Q_BLOCK, H, 2, d), 1, 0)
    lp = lam_params.astype(jnp.float32)
    lam = jnp.exp(jnp.sum(lp[0] * lp[1])) - jnp.exp(jnp.sum(lp[2] * lp[3])) + lambda_init
    kpos = jnp.arange(S)

    def block(args):
        i, qb = args
        qpos = i * Q_BLOCK + jnp.arange(Q_BLOCK)
        s = jnp.einsum('bqhcd,bkhcd->bhcqk', qb, k_sh).astype(jnp.float32)
        s = jnp.where(kpos[None, :] <= qpos[:, None], s, -jnp.inf)
        p = jax.nn.softmax(s, axis=-1)
        a = p[:, :, 0] - lam * p[:, :, 1]
        return jnp.einsum('bhqk,bkhe->bqhe', a.astype(v_sh.dtype), v_sh)

    o = lax.map(block, (jnp.arange(nb), q_blocks))
    o = jnp.moveaxis(o, 0, 1).reshape(B, S, H, 2 * d)
    o = rms_norm(o, subln_g) * (1.0 - lambda_init)
    return o.reshape(B, S, H * 2 * d) @ w_out


def conv_ffn(xn, w_up, conv_w, conv_b, w_down):
    S = xn.shape[1]
    h = xn @ w_up
    hp = jnp.pad(h, ((0, 0), (CONV_WIDTH - 1, 0), (0, 0)))
    hc = conv_b
    for j in range(CONV_WIDTH):
        hc = hc + hp[:, j:j + S] * conv_w[j]
    g, u = jnp.split(hc, 2, axis=-1)
    return (jax.nn.silu(g) * u) @ w_down


def setup_inputs(seed: int = 0) -> dict:
    key = jax.random.key(seed)
    ks = jax.random.split(key, 24)
    D, F = D_MODEL, D_FF
    ret_in = 2 * RET_HEADS * RET_QK_DIM + 2 * RET_HEADS * RET_V_DIM
    ret_v = RET_HEADS * RET_V_DIM
    diff_w = DIFF_HEADS * 2 * DIFF_HEAD_DIM

    def w(k, shape, fan_in):
        return jax.random.normal(k, shape, jnp.float32) * (fan_in ** -0.5)

    def gain(k, shape):
        return 1.0 + 0.02 * jax.random.normal(k, shape, jnp.float32)

    return {
        "x": jax.random.normal(ks[0], (BATCH, SEQ, D), jnp.float32),
        "a_norm_pre": gain(ks[1], (N_A_LAYERS, D)),
        "a_norm_post": gain(ks[2], (N_A_LAYERS, D)),
        "a_w_in": w(ks[3], (N_A_LAYERS, D, ret_in), D),
        "a_w_out": w(ks[4], (N_A_LAYERS, ret_v, D), ret_v),
        "kv_norm": gain(ks[5], (D,)),
        "w_kv": w(ks[6], (D, 2 * diff_w), D),
        "b_norm_pre": gain(ks[7], (N_B_LAYERS, D)),
        "b_norm_post": gain(ks[8], (N_B_LAYERS, D)),
        "b_w_q": w(ks[9], (N_B_LAYERS, D, diff_w), D),
        "b_lambda": 0.1 * jax.random.normal(ks[10], (N_B_LAYERS, 4, DIFF_HEAD_DIM), jnp.float32),
        "b_subln": gain(ks[11], (N_B_LAYERS, 2 * DIFF_HEAD_DIM)),
        "b_w_out": w(ks[12], (N_B_LAYERS, diff_w, D), diff_w),
        "ffn_norm_pre": gain(ks[13], (DEPTH, D)),
        "ffn_norm_post": gain(ks[14], (DEPTH, D)),
        "ffn_w_up": w(ks[15], (DEPTH, D, 2 * F), D),
        "ffn_conv_w": w(ks[16], (DEPTH, CONV_WIDTH, 2 * F), CONV_WIDTH),
        "ffn_conv_b": 0.02 * jax.random.normal(ks[17], (DEPTH, 2 * F), jnp.float32),
        "ffn_w_down": w(ks[18], (DEPTH, F, D), F),
    }


def reference(x, a_norm_pre, a_norm_post, a_w_in, a_w_out, kv_norm, w_kv,
              b_norm_pre, b_norm_post, b_w_q, b_lambda, b_subln, b_w_out,
              ffn_norm_pre, ffn_norm_post, ffn_w_up, ffn_conv_w, ffn_conv_b, ffn_w_down):
    B, S, _ = x.shape
    H, d = DIFF_HEADS, DIFF_HEAD_DIM
    k_sh = None
    v_sh = None
    for layer in range(DEPTH):
        if layer < N_A_LAYERS:
            i = layer
            h = retention(rms_norm(x, a_norm_pre[i]), a_w_in[i], a_w_out[i])
            x = x + rms_norm(h, a_norm_post[i])
        else:
            i = layer - N_A_LAYERS
            if i == 0:
                kv = rms_norm(x, kv_norm) @ w_kv
                k_flat, v_flat = jnp.split(kv, 2, axis=-1)
                k_sh = k_flat.reshape(B, S, H, 2, d)
                v_sh = v_flat.reshape(B, S, H, 2 * d)
            lambda_init = 0.8 - 0.6 * math.exp(-0.3 * layer)
            h = diff_attention(rms_norm(x, b_norm_pre[i]), k_sh, v_sh, b_w_q[i],
                               b_lambda[i], b_subln[i], b_w_out[i], lambda_init)
            x = x + rms_norm(h, b_norm_post[i])
        h = conv_ffn(rms_norm(x, ffn_norm_pre[layer]), ffn_w_up[layer],
                     ffn_conv_w[layer], ffn_conv_b[layer], ffn_w_down[layer])
        x = x + rms_norm(h, ffn_norm_post[layer])
    return x
```

```python
import functools
import math

import jax
import jax.numpy as jnp
from jax import lax
from jax.experimental import pallas as pl
from jax.experimental.pallas import tpu as pltpu

F32 = jnp.float32
BF16 = jnp.bfloat16

D_MODEL = 1024
RET_HEADS = 4
RET_QK_DIM = D_MODEL // RET_HEADS
RET_V_DIM = 2 * RET_QK_DIM
RET_CHUNK = 128
DIFF_HEAD_DIM = 64
DIFF_HEADS = D_MODEL // (2 * DIFF_HEAD_DIM)
DIFF_V_DIM = 2 * DIFF_HEAD_DIM
D_FF = ((8 * D_MODEL // 3 + 127) // 128) * 128
ROPE_BASE = 10000.0
EPS = 1e-6
MASK_VALUE = -1e30

BF16_ROWS = 16
MIB = 1024 * 1024

PROJ_TM = 1024
PROJ_TN = 1024
POST_TM = 512
KVQ_TM = 512
FFN_TM = 512
FFN_TF = 256
ATT_TQ = 512
ATT_TK = 512


def _params(semantics, vmem_bytes):
    return pltpu.CompilerParams(dimension_semantics=semantics,
                                vmem_limit_bytes=int(vmem_bytes))


def _resident(shape):
    zeros = (0,) * len(shape)
    return pl.BlockSpec(shape, lambda *_: zeros, pipeline_mode=pl.Buffered(1))


def _rms_scale(v):
    return lax.rsqrt(jnp.mean(v * v, axis=-1, keepdims=True) + EPS)


def _proj_in_kernel(x_ref, g_ref, w_ref, cos_ref, sin_ref, o_ref, xn_ref):
    j = pl.program_id(1)

    @pl.when(j == 0)
    def _():
        x = x_ref[...]
        xn_ref[...] = (x * _rms_scale(x) * g_ref[...]).astype(BF16)

    def acc():
        return jnp.dot(xn_ref[...], w_ref[...], preferred_element_type=F32)

    def rotary(scale):
        a = acc()
        cos = cos_ref[...]
        sin = sin_ref[...]
        half = RET_QK_DIM // 2
        for h in range(PROJ_TN // RET_QK_DIM):
            lo = h * RET_QK_DIM
            x1 = a[:, lo:lo + half]
            x2 = a[:, lo + half:lo + 2 * half]
            o_ref[:, lo:lo + half] = ((x1 * cos - x2 * sin) * scale).astype(BF16)
            o_ref[:, lo + half:lo + 2 * half] = ((x1 * sin + x2 * cos) * scale).astype(BF16)

    n_qk = RET_HEADS * RET_QK_DIM // PROJ_TN
    n_v = RET_HEADS * RET_V_DIM // PROJ_TN

    @pl.when(j < n_qk)
    def _():
        rotary(1.0)

    @pl.when((j >= n_qk) & (j < 2 * n_qk))
    def _():
        rotary(RET_QK_DIM ** -0.5)

    @pl.when((j >= 2 * n_qk) & (j < 2 * n_qk + n_v))
    def _():
        o_ref[...] = acc().astype(BF16)

    @pl.when(j >= 2 * n_qk + n_v)
    def _():
        a = acc()
        o_ref[...] = (a / (1.0 + jnp.exp(-a))).astype(BF16)


def _proj_in(x2d, gain, w, cos, sin, seq):
    t, d = x2d.shape
    n = w.shape[1]
    tiles_per_seq = seq // PROJ_TM
    vmem = (2 * PROJ_TM * d * 4 + 2 * d * PROJ_TN * 2 + 2 * PROJ_TM * PROJ_TN * 2
            + PROJ_TM * d * 2 + 4 * PROJ_TM * PROJ_TN * 4 + 4 * MIB)
    return pl.pallas_call(
        _proj_in_kernel,
        grid=(t // PROJ_TM, n // PROJ_TN),
        in_specs=[
            pl.BlockSpec((PROJ_TM, d), lambda i, j: (i, 0)),
            pl.BlockSpec((1, d), lambda i, j: (0, 0)),
            pl.BlockSpec((d, PROJ_TN), lambda i, j: (0, j)),
            pl.BlockSpec((PROJ_TM, RET_QK_DIM // 2), lambda i, j: (i % tiles_per_seq, 0)),
            pl.BlockSpec((PROJ_TM, RET_QK_DIM // 2), lambda i, j: (i % tiles_per_seq, 0)),
        ],
        out_specs=pl.BlockSpec((PROJ_TM, PROJ_TN), lambda i, j: (i, j)),
        out_shape=jax.ShapeDtypeStruct((t, n), BF16),
        scratch_shapes=[pltpu.VMEM((PROJ_TM, d), BF16)],
        compiler_params=_params(("parallel", "arbitrary"), vmem),
        name="ret_proj_in",
    )(x2d, gain, w, cos, sin)


def _retention_kernel(lg_ref, q_ref, k_ref, v_ref, sg_ref, o_ref, state_ref):
    c_len = RET_CHUNK
    lg = lg_ref[pl.program_id(1)]
    row = lax.broadcasted_iota(jnp.int32, (c_len, c_len), 0)
    col = lax.broadcasted_iota(jnp.int32, (c_len, c_len), 1)
    rel = (row - col).astype(F32)
    decay_mask = jnp.where(rel >= 0, jnp.exp(lg * jnp.maximum(rel, 0.0)), 0.0)
    idx = lax.broadcasted_iota(jnp.int32, (c_len, 1), 0).astype(F32)
    q_decay = jnp.exp(lg * (idx + 1.0))
    k_decay = jnp.exp(lg * (c_len - 1.0 - idx))
    chunk_decay = jnp.exp(lg * jnp.full((1, 1), c_len, F32))

    state_ref[...] = jnp.zeros_like(state_ref)

    def chunk(c, carry):
        rows = pl.ds(pl.multiple_of(c * c_len, c_len), c_len)
        q = q_ref[0, rows, :]
        k = k_ref[0, rows, :]
        v = v_ref[0, rows, :]
        s = lax.dot_general(q, k, (((1,), (1,)), ((), ())),
                            preferred_element_type=F32) * decay_mask
        inner = jnp.dot(s.astype(BF16), v, preferred_element_type=F32)
        state = state_ref[...]
        cross = jnp.dot(q, state.astype(BF16), preferred_element_type=F32) * q_decay
        kd = (k.astype(F32) * k_decay).astype(BF16)
        state_ref[...] = state * chunk_decay + lax.dot_general(
            kd, v, (((0,), (0,)), ((), ())), preferred_element_type=F32)
        out = inner + cross
        gate = sg_ref[0, rows, :].astype(F32)
        o_ref[0, rows, :] = (out * _rms_scale(out) * gate).astype(BF16)
        return carry

    lax.fori_loop(0, q_ref.shape[1] // c_len, chunk, 0)


def _retention(proj3d, log_gamma):
    b, s, _ = proj3d.shape
    dk, dv, h = RET_QK_DIM, RET_V_DIM, RET_HEADS
    k_blk0 = h * dk // dk
    v_blk0 = 2 * h * dk // dv
    g_blk0 = (2 * h * dk + h * dv) // dv
    vmem = 2 * s * (2 * dk + 3 * dv) * 2 + dk * dv * 4 + 8 * MIB
    return pl.pallas_call(
        _retention_kernel,
        grid=(b, h),
        in_specs=[
            pl.BlockSpec(memory_space=pltpu.SMEM),
            pl.BlockSpec((1, s, dk), lambda i, j: (i, 0, j)),
            pl.BlockSpec((1, s, dk), lambda i, j: (i, 0, k_blk0 + j)),
            pl.BlockSpec((1, s, dv), lambda i, j: (i, 0, v_blk0 + j)),
            pl.BlockSpec((1, s, dv), lambda i, j: (i, 0, g_blk0 + j)),
        ],
        out_specs=pl.BlockSpec((1, s, dv), lambda i, j: (i, 0, j)),
        out_shape=jax.ShapeDtypeStruct((b, s, h * dv), BF16),
        scratch_shapes=[pltpu.VMEM((dk, dv), F32)],
        compiler_params=_params(("parallel", "parallel"), vmem),
        name="retention_core",
    )(log_gamma, proj3d, proj3d, proj3d, proj3d)


def _post_kernel(y_ref, w_ref, g_ref, x_ref, o_ref):
    h = jnp.dot(y_ref[...], w_ref[...], preferred_element_type=F32)
    o_ref[...] = x_ref[...] + h * _rms_scale(h) * g_ref[...]


def _post(y2d, w, gain, x2d, name):
    t, kdim = y2d.shape
    d = w.shape[1]
    vmem = (2 * POST_TM * kdim * 2 + kdim * d * 2 + 4 * POST_TM * d * 4
            + 2 * POST_TM * d * 4 + 4 * MIB)
    return pl.pallas_call(
        _post_kernel,
        grid=(t // POST_TM,),
        in_specs=[
            pl.BlockSpec((POST_TM, kdim), lambda i: (i, 0)),
            _resident((kdim, d)),
            pl.BlockSpec((1, d), lambda i: (0, 0)),
            pl.BlockSpec((POST_TM, d), lambda i: (i, 0)),
        ],
        out_specs=pl.BlockSpec((POST_TM, d), lambda i: (i, 0)),
        out_shape=jax.ShapeDtypeStruct((t, d), F32),
        compiler_params=_params(("parallel",), vmem),
        name=name,
    )(y2d, w, gain, x2d)


def _ffn_kernel(x_ref, xh_ref, gpre_ref, wup_ref, cw_ref, wdn_ref, gpost_ref, o_ref,
                xn_ref, acc_ref):
    n_f = wdn_ref.shape[0]
    halo = BF16_ROWS
    x = x_ref[0]
    gpre = gpre_ref[...]
    xn_ref[halo:, :] = (x * _rms_scale(x) * gpre).astype(BF16)
    xh = xh_ref[0]
    xhn = jnp.where(pl.program_id(1) > 0, xh * _rms_scale(xh) * gpre, 0.0)
    xn_ref[:halo, :] = xhn.astype(BF16)
    acc_ref[...] = jnp.zeros_like(acc_ref)

    def conv_branch(xn, idx):
        h = jnp.dot(xn, wup_ref[idx], preferred_element_type=F32)
        cw = cw_ref[idx]
        return (cw[3:4] + cw[2:3] * h[halo:]
                + cw[1:2] * pltpu.roll(h, 1, 0)[halo:]
                + cw[0:1] * pltpu.roll(h, 2, 0)[halo:])

    def hidden_tile(f, carry):
        xn = xn_ref[...]
        g = conv_branch(xn, f)
        u = conv_branch(xn, n_f + f)
        act = (g / (1.0 + jnp.exp(-g)) * u).astype(BF16)
        acc_ref[...] += jnp.dot(act, wdn_ref[f], preferred_element_type=F32)
        return carry

    lax.fori_loop(0, n_f, hidden_tile, 0)
    h = acc_ref[...]
    o_ref[0] = x + h * _rms_scale(h) * gpost_ref[...]


def _conv_ffn(x3d, gpre, w_up, conv_w, conv_b, w_down, gpost, name):
    b, s, d = x3d.shape
    f_dim = w_down.shape[0]
    n_f = f_dim // FFN_TF
    tm, halo = FFN_TM, BF16_ROWS
    wup_t = w_up.astype(BF16).reshape(d, 2 * n_f, FFN_TF).transpose(1, 0, 2)
    wdn_t = w_down.astype(BF16).reshape(n_f, FFN_TF, d)
    cw = jnp.concatenate([conv_w, conv_b[None, :], jnp.zeros((4, 2 * f_dim), F32)], axis=0)
    cw_t = cw.reshape(8, 2 * n_f, FFN_TF).transpose(1, 0, 2)
    halo_blocks = tm // halo
    vmem = (2 * f_dim * d * 2 + f_dim * d * 2 + 4 * tm * d * 4 + (tm + halo) * d * 2
            + tm * d * 4 + 12 * (tm + halo) * FFN_TF * 4 + 2 * tm * d * 4 + 4 * MIB)
    return pl.pallas_call(
        _ffn_kernel,
        grid=(b, s // tm),
        in_specs=[
            pl.BlockSpec((1, tm, d), lambda i, j: (i, j, 0)),
            pl.BlockSpec((1, halo, d), lambda i, j: (i, jnp.maximum(j * halo_blocks - 1, 0), 0)),
            pl.BlockSpec((1, d), lambda i, j: (0, 0)),
            _resident((2 * n_f, d, FFN_TF)),
            _resident((2 * n_f, 8, FFN_TF)),
            _resident((n_f, FFN_TF, d)),
            pl.BlockSpec((1, d), lambda i, j: (0, 0)),
        ],
        out_specs=pl.BlockSpec((1, tm, d), lambda i, j: (i, j, 0)),
        out_shape=jax.ShapeDtypeStruct((b, s, d), F32),
        scratch_shapes=[pltpu.VMEM((tm + halo, d), BF16), pltpu.VMEM((tm, d), F32)],
        compiler_params=_params(("parallel", "parallel"), vmem),
        name=name,
    )(x3d, x3d, gpre, wup_t, cw_t, wdn_t, gpost)


def _kvq_kernel(x_ref, gkv_ref, gq_ref, wkv_ref, wq_ref, kv_ref, q_ref):
    x = x_ref[...]
    xr = x * _rms_scale(x)
    kv = jnp.dot((xr * gkv_ref[...]).astype(BF16), wkv_ref[...], preferred_element_type=F32)
    kv_ref[...] = kv.astype(BF16)
    q = jnp.dot((xr * gq_ref[...]).astype(BF16), wq_ref[...], preferred_element_type=F32)
    q_ref[...] = (q * DIFF_HEAD_DIM ** -0.5).astype(BF16)


def _kvq(x2d, g_kv, g_q, w_kv, w_q):
    t, d = x2d.shape
    n_kv, n_q = w_kv.shape[1], w_q.shape[1]
    tm = KVQ_TM
    vmem = (2 * tm * d * 4 + d * (n_kv + n_q) * 2 + 2 * tm * (n_kv + n_q) * 2
            + 6 * tm * d * 4 + 2 * tm * (n_kv + n_q) * 4 + 4 * MIB)
    return pl.pallas_call(
        _kvq_kernel,
        grid=(t // tm,),
        in_specs=[
            pl.BlockSpec((tm, d), lambda i: (i, 0)),
            pl.BlockSpec((1, d), lambda i: (0, 0)),
            pl.BlockSpec((1, d), lambda i: (0, 0)),
            _resident((d, n_kv)),
            _resident((d, n_q)),
        ],
        out_specs=[pl.BlockSpec((tm, n_kv), lambda i: (i, 0)),
                   pl.BlockSpec((tm, n_q), lambda i: (i, 0))],
        out_shape=[jax.ShapeDtypeStruct((t, n_kv), BF16),
                   jax.ShapeDtypeStruct((t, n_q), BF16)],
        compiler_params=_params(("parallel",), vmem),
        name="kv_q_proj",
    )(x2d, g_kv, g_q, w_kv, w_q)


def _diff_attn_kernel(q_ref, k_ref, v_ref, lam_ref, gsub_ref, o_ref, q2_ref, *, lambda_init):
    tq, tk, d = ATT_TQ, ATT_TK, DIFF_HEAD_DIM
    i = pl.program_id(2)

    q = q_ref[0]
    lane = lax.broadcasted_iota(jnp.int32, q.shape, 1)
    zero = jnp.zeros_like(q)
    q2_ref[:tq, :] = jnp.where(lane < d, q, zero)
    q2_ref[tq:, :] = jnp.where(lane >= d, q, zero)

    def kv_tile(j, carry, masked):
        m_old, l_old, acc = carry
        cols = pl.ds(pl.multiple_of(j * tk, tk), tk)
        s = lax.dot_general(q2_ref[...], k_ref[0, cols, :], (((1,), (1,)), ((), ())),
                            preferred_element_type=F32)
        if masked:
            row = lax.broadcasted_iota(jnp.int32, (2 * tq, tk), 0)
            qpos = i * tq + jnp.where(row >= tq, row - tq, row)
            kpos = j * tk + lax.broadcasted_iota(jnp.int32, (2 * tq, tk), 1)
            s = jnp.where(kpos <= qpos, s, MASK_VALUE)
        m_new = jnp.maximum(m_old, jnp.max(s, axis=-1, keepdims=True))
        alpha = jnp.exp(m_old - m_new)
        p = jnp.exp(s - m_new)
        l_new = alpha * l_old + jnp.sum(p, axis=-1, keepdims=True)
        acc = alpha * acc + jnp.dot(p.astype(BF16), v_ref[0, cols, :],
                                    preferred_element_type=F32)
        return m_new, l_new, acc

    n_full = (i * tq) // tk
    n_all = ((i + 1) * tq + tk - 1) // tk
    init = (jnp.full((2 * tq, 1), MASK_VALUE, F32), jnp.zeros((2 * tq, 1), F32),
            jnp.zeros((2 * tq, DIFF_V_DIM), F32))
    carry = lax.fori_loop(0, n_full, functools.partial(kv_tile, masked=False), init)
    _, l, acc = lax.fori_loop(n_full, n_all, functools.partial(kv_tile, masked=True), carry)

    lp = lam_ref[...]
    lam = (jnp.exp(jnp.sum(lp[0:1] * lp[1:2], axis=-1, keepdims=True))
           - jnp.exp(jnp.sum(lp[2:3] * lp[3:4], axis=-1, keepdims=True)) + lambda_init)
    o = acc[:tq] * (1.0 / l[:tq]) - lam * (acc[tq:] * (1.0 / l[tq:]))
    o_ref[0] = (o * _rms_scale(o) * gsub_ref[...] * (1.0 - lambda_init)).astype(BF16)


def _diff_attention(q3d, kv3d, lam_params, g_sub, lambda_init):
    b, s, _ = q3d.shape
    h, dv = DIFF_HEADS, DIFF_V_DIM
    tq, tk = ATT_TQ, ATT_TK
    vmem = (2 * tq * dv * 2 * 2 + 2 * 2 * s * dv * 2 + 2 * tq * dv * 2
            + 6 * 2 * tq * tk * 4 + 4 * 2 * tq * dv * 4 + 8 * MIB)
    return pl.pallas_call(
        functools.partial(_diff_attn_kernel, lambda_init=lambda_init),
        grid=(b, h, s // tq),
        in_specs=[
            pl.BlockSpec((1, tq, dv), lambda bi, hi, qi: (bi, qi, hi)),
            pl.BlockSpec((1, s, dv), lambda bi, hi, qi: (bi, 0, hi)),
            pl.BlockSpec((1, s, dv), lambda bi, hi, qi: (bi, 0, h + hi)),
            pl.BlockSpec(lam_params.shape, lambda bi, hi, qi: (0, 0)),
            pl.BlockSpec((1, dv), lambda bi, hi, qi: (0, 0)),
        ],
        out_specs=pl.BlockSpec((1, tq, dv), lambda bi, hi, qi: (bi, qi, hi)),
        out_shape=jax.ShapeDtypeStruct((b, s, h * dv), BF16),
        scratch_shapes=[pltpu.VMEM((2 * tq, dv), BF16)],
        compiler_params=_params(("parallel", "parallel", "parallel"), vmem),
        name="diff_attention",
    )(q3d, kv3d, kv3d, lam_params, g_sub)


def kernel(x, a_norm_pre, a_norm_post, a_w_in, a_w_out, kv_norm, w_kv, b_norm_pre, b_norm_post,
           b_w_q, b_lambda, b_subln, b_w_out, ffn_norm_pre, ffn_norm_post, ffn_w_up, ffn_conv_w,
           ffn_conv_b, ffn_w_down):
    b, s, d = x.shape
    t = b * s

    half = RET_QK_DIM // 2
    inv = 1.0 / (ROPE_BASE ** jnp.linspace(0.0, 1.0, half, dtype=F32))
    ang = jnp.arange(s).astype(F32)[:, None] * inv[None, :]
    cos, sin = jnp.cos(ang), jnp.sin(ang)
    log_gamma = jnp.log1p(-jnp.power(2.0, -5.0 - jnp.arange(RET_HEADS, dtype=F32)))

    proj = _proj_in(x.reshape(t, d), a_norm_pre[0][None], a_w_in[0].astype(BF16), cos, sin, s)
    y = _retention(proj.reshape(b, s, -1), log_gamma)
    x = _post(y.reshape(t, -1), a_w_out[0].astype(BF16), a_norm_post[0][None],
              x.reshape(t, d), "ret_out_proj")
    x = _conv_ffn(x.reshape(b, s, d), ffn_norm_pre[0][None], ffn_w_up[0], ffn_conv_w[0],
                  ffn_conv_b[0], ffn_w_down[0], ffn_norm_post[0][None], "conv_ffn_0")

    layer = 1
    lambda_init = 0.8 - 0.6 * math.exp(-0.3 * layer)
    kv, q = _kvq(x.reshape(t, d), kv_norm[None], b_norm_pre[0][None],
                 w_kv.astype(BF16), b_w_q[0].astype(BF16))
    o = _diff_attention(q.reshape(b, s, -1), kv.reshape(b, s, -1), b_lambda[0],
                        b_subln[0][None], lambda_init)
    x = _post(o.reshape(t, -1), b_w_out[0].astype(BF16), b_norm_post[0][None],
              x.reshape(t, d), "attn_out_proj")
    x = _conv_ffn(x.reshape(b, s, d), ffn_norm_pre[1][None], ffn_w_up[1], ffn_conv_w[1],
                  ffn_conv_b[1], ffn_w_down[1], ffn_norm_post[1][None], "conv_ffn_1")
    return x
```

```python
import functools
import math

import jax
import jax.numpy as jnp
from jax import lax
from jax.experimental import pallas as pl
from jax.experimental.pallas import tpu as pltpu

F32 = jnp.float32
BF16 = jnp.bfloat16

D_MODEL = 1024
RET_HEADS = 4
RET_QK_DIM = D_MODEL // RET_HEADS
RET_V_DIM = 2 * RET_QK_DIM
RET_CHUNK = 128
DIFF_HEAD_DIM = 64
DIFF_HEADS = D_MODEL // (2 * DIFF_HEAD_DIM)
DIFF_V_DIM = 2 * DIFF_HEAD_DIM
D_FF = ((8 * D_MODEL // 3 + 127) // 128) * 128
ROPE_BASE = 10000.0
EPS = 1e-6
MASK_VALUE = -1e30
LOG2_E = math.log2(math.e)

BF16_ROWS = 16
LANES = 128
MIB = 1024 * 1024

PROJ_TM = 1024
PROJ_TN = 1024
POST_TM = 512
KVQ_TM = 512
FFN_TM = 512
FFN_TF = 256
ATT_TQ = 512
ATT_TK = 256


def _params(semantics, vmem_bytes):
    return pltpu.CompilerParams(dimension_semantics=semantics,
                                vmem_limit_bytes=int(vmem_bytes))


def _resident(shape):
    zeros = (0,) * len(shape)
    return pl.BlockSpec(shape, lambda *_: zeros, pipeline_mode=pl.Buffered(1))


def _rms_scale(v):
    return lax.rsqrt(jnp.mean(v * v, axis=-1, keepdims=True) + EPS)


def _proj_in_kernel(x_ref, g_ref, w_ref, cos_ref, sin_ref, o_ref, xn_ref):
    j = pl.program_id(1)

    @pl.when(j == 0)
    def _():
        x = x_ref[...]
        xn_ref[...] = (x * _rms_scale(x) * g_ref[...]).astype(BF16)

    def acc():
        return jnp.dot(xn_ref[...], w_ref[...], preferred_element_type=F32)

    def rotary(scale):
        a = acc()
        cos = cos_ref[...]
        sin = sin_ref[...]
        half = RET_QK_DIM // 2
        for h in range(PROJ_TN // RET_QK_DIM):
            lo = h * RET_QK_DIM
            x1 = a[:, lo:lo + half]
            x2 = a[:, lo + half:lo + 2 * half]
            o_ref[:, lo:lo + half] = ((x1 * cos - x2 * sin) * scale).astype(BF16)
            o_ref[:, lo + half:lo + 2 * half] = ((x1 * sin + x2 * cos) * scale).astype(BF16)

    n_qk = RET_HEADS * RET_QK_DIM // PROJ_TN
    n_v = RET_HEADS * RET_V_DIM // PROJ_TN

    @pl.when(j < n_qk)
    def _():
        rotary(1.0)

    @pl.when((j >= n_qk) & (j < 2 * n_qk))
    def _():
        rotary(RET_QK_DIM ** -0.5)

    @pl.when((j >= 2 * n_qk) & (j < 2 * n_qk + n_v))
    def _():
        o_ref[...] = acc().astype(BF16)

    @pl.when(j >= 2 * n_qk + n_v)
    def _():
        a = acc()
        o_ref[...] = (a / (1.0 + jnp.exp(-a))).astype(BF16)


def _proj_in(x2d, gain, w, cos, sin, seq):
    t, d = x2d.shape
    n = w.shape[1]
    tiles_per_seq = seq // PROJ_TM
    vmem = (2 * PROJ_TM * d * 4 + 2 * d * PROJ_TN * 2 + 2 * PROJ_TM * PROJ_TN * 2
            + PROJ_TM * d * 2 + 4 * PROJ_TM * PROJ_TN * 4 + 4 * MIB)
    return pl.pallas_call(
        _proj_in_kernel,
        grid=(t // PROJ_TM, n // PROJ_TN),
        in_specs=[
            pl.BlockSpec((PROJ_TM, d), lambda i, j: (i, 0)),
            pl.BlockSpec((1, d), lambda i, j: (0, 0)),
            pl.BlockSpec((d, PROJ_TN), lambda i, j: (0, j)),
            pl.BlockSpec((PROJ_TM, RET_QK_DIM // 2), lambda i, j: (i % tiles_per_seq, 0)),
            pl.BlockSpec((PROJ_TM, RET_QK_DIM // 2), lambda i, j: (i % tiles_per_seq, 0)),
        ],
        out_specs=pl.BlockSpec((PROJ_TM, PROJ_TN), lambda i, j: (i, j)),
        out_shape=jax.ShapeDtypeStruct((t, n), BF16),
        scratch_shapes=[pltpu.VMEM((PROJ_TM, d), BF16)],
        compiler_params=_params(("parallel", "arbitrary"), vmem),
        name="ret_proj_in",
    )(x2d, gain, w, cos, sin)


def _retention_kernel(lg_ref, q_ref, k_ref, v_ref, sg_ref, o_ref, state_ref):
    c_len = RET_CHUNK
    lg = lg_ref[pl.program_id(1)]
    row = lax.broadcasted_iota(jnp.int32, (c_len, c_len), 0)
    col = lax.broadcasted_iota(jnp.int32, (c_len, c_len), 1)
    rel = (row - col).astype(F32)
    decay_mask = jnp.where(rel >= 0, jnp.exp(lg * jnp.maximum(rel, 0.0)), 0.0)
    idx = lax.broadcasted_iota(jnp.int32, (c_len, 1), 0).astype(F32)
    q_decay = jnp.exp(lg * (idx + 1.0))
    k_decay = jnp.exp(lg * (c_len - 1.0 - idx))
    chunk_decay = jnp.exp(lg * jnp.full((1, 1), c_len, F32))

    state_ref[...] = jnp.zeros_like(state_ref)

    def chunk(c, carry):
        rows = pl.ds(pl.multiple_of(c * c_len, c_len), c_len)
        q = q_ref[0, rows, :]
        k = k_ref[0, rows, :]
        v = v_ref[0, rows, :]
        s = lax.dot_general(q, k, (((1,), (1,)), ((), ())),
                            preferred_element_type=F32) * decay_mask
        inner = jnp.dot(s.astype(BF16), v, preferred_element_type=F32)
        state = state_ref[...]
        cross = jnp.dot(q, state.astype(BF16), preferred_element_type=F32) * q_decay
        kd = (k.astype(F32) * k_decay).astype(BF16)
        state_ref[...] = state * chunk_decay + lax.dot_general(
            kd, v, (((0,), (0,)), ((), ())), preferred_element_type=F32)
        out = inner + cross
        gate = sg_ref[0, rows, :].astype(F32)
        o_ref[0, rows, :] = (out * _rms_scale(out) * gate).astype(BF16)
        return carry

    lax.fori_loop(0, q_ref.shape[1] // c_len, chunk, 0)


def _retention(proj3d, log_gamma):
    b, s, _ = proj3d.shape
    dk, dv, h = RET_QK_DIM, RET_V_DIM, RET_HEADS
    k_blk0 = h * dk // dk
    v_blk0 = 2 * h * dk // dv
    g_blk0 = (2 * h * dk + h * dv) // dv
    vmem = 2 * s * (2 * dk + 3 * dv) * 2 + dk * dv * 4 + 8 * MIB
    return pl.pallas_call(
        _retention_kernel,
        grid=(b, h),
        in_specs=[
            pl.BlockSpec(memory_space=pltpu.SMEM),
            pl.BlockSpec((1, s, dk), lambda i, j: (i, 0, j)),
            pl.BlockSpec((1, s, dk), lambda i, j: (i, 0, k_blk0 + j)),
            pl.BlockSpec((1, s, dv), lambda i, j: (i, 0, v_blk0 + j)),
            pl.BlockSpec((1, s, dv), lambda i, j: (i, 0, g_blk0 + j)),
        ],
        out_specs=pl.BlockSpec((1, s, dv), lambda i, j: (i, 0, j)),
        out_shape=jax.ShapeDtypeStruct((b, s, h * dv), BF16),
        scratch_shapes=[pltpu.VMEM((dk, dv), F32)],
        compiler_params=_params(("parallel", "parallel"), vmem),
        name="retention_core",
    )(log_gamma, proj3d, proj3d, proj3d, proj3d)


def _post_kernel(y_ref, w_ref, g_ref, x_ref, o_ref):
    h = jnp.dot(y_ref[...], w_ref[...], preferred_element_type=F32)
    o_ref[...] = x_ref[...] + h * _rms_scale(h) * g_ref[...]


def _post(y2d, w, gain, x2d, name):
    t, kdim = y2d.shape
    d = w.shape[1]
    vmem = (2 * POST_TM * kdim * 2 + kdim * d * 2 + 4 * POST_TM * d * 4
            + 2 * POST_TM * d * 4 + 4 * MIB)
    return pl.pallas_call(
        _post_kernel,
        grid=(t // POST_TM,),
        in_specs=[
            pl.BlockSpec((POST_TM, kdim), lambda i: (i, 0)),
            _resident((kdim, d)),
            pl.BlockSpec((1, d), lambda i: (0, 0)),
            pl.BlockSpec((POST_TM, d), lambda i: (i, 0)),
        ],
        out_specs=pl.BlockSpec((POST_TM, d), lambda i: (i, 0)),
        out_shape=jax.ShapeDtypeStruct((t, d), F32),
        compiler_params=_params(("parallel",), vmem),
        name=name,
    )(y2d, w, gain, x2d)


def _ffn_kernel(x_ref, xh_ref, gpre_ref, wup_ref, cw_ref, wdn_ref, gpost_ref, o_ref,
                xn_ref, acc_ref):
    n_f = wdn_ref.shape[0]
    halo = BF16_ROWS
    x = x_ref[0]
    gpre = gpre_ref[...]
    xn_ref[halo:, :] = (x * _rms_scale(x) * gpre).astype(BF16)
    xh = xh_ref[0]
    xhn = jnp.where(pl.program_id(1) > 0, xh * _rms_scale(xh) * gpre, 0.0)
    xn_ref[:halo, :] = xhn.astype(BF16)
    acc_ref[...] = jnp.zeros_like(acc_ref)

    def conv_branch(xn, idx):
        h = jnp.dot(xn, wup_ref[idx], preferred_element_type=F32)
        cw = cw_ref[idx]
        return (cw[3:4] + cw[2:3] * h[halo:]
                + cw[1:2] * pltpu.roll(h, 1, 0)[halo:]
                + cw[0:1] * pltpu.roll(h, 2, 0)[halo:])

    def hidden_tile(f, carry):
        xn = xn_ref[...]
        g = conv_branch(xn, f)
        u = conv_branch(xn, n_f + f)
        act = (g / (1.0 + jnp.exp(-g)) * u).astype(BF16)
        acc_ref[...] += jnp.dot(act, wdn_ref[f], preferred_element_type=F32)
        return carry

    lax.fori_loop(0, n_f, hidden_tile, 0)
    h = acc_ref[...]
    o_ref[0] = x + h * _rms_scale(h) * gpost_ref[...]


def _conv_ffn(x3d, gpre, w_up, conv_w, conv_b, w_down, gpost, name):
    b, s, d = x3d.shape
    f_dim = w_down.shape[0]
    n_f = f_dim // FFN_TF
    tm, halo = FFN_TM, BF16_ROWS
    wup_t = w_up.astype(BF16).reshape(d, 2 * n_f, FFN_TF).transpose(1, 0, 2)
    wdn_t = w_down.astype(BF16).reshape(n_f, FFN_TF, d)
    cw = jnp.concatenate([conv_w, conv_b[None, :], jnp.zeros((4, 2 * f_dim), F32)], axis=0)
    cw_t = cw.reshape(8, 2 * n_f, FFN_TF).transpose(1, 0, 2)
    halo_blocks = tm // halo
    vmem = (2 * f_dim * d * 2 + f_dim * d * 2 + 4 * tm * d * 4 + (tm + halo) * d * 2
            + tm * d * 4 + 12 * (tm + halo) * FFN_TF * 4 + 2 * tm * d * 4 + 4 * MIB)
    return pl.pallas_call(
        _ffn_kernel,
        grid=(b, s // tm),
        in_specs=[
            pl.BlockSpec((1, tm, d), lambda i, j: (i, j, 0)),
            pl.BlockSpec((1, halo, d), lambda i, j: (i, jnp.maximum(j * halo_blocks - 1, 0), 0)),
            pl.BlockSpec((1, d), lambda i, j: (0, 0)),
            _resident((2 * n_f, d, FFN_TF)),
            _resident((2 * n_f, 8, FFN_TF)),
            _resident((n_f, FFN_TF, d)),
            pl.BlockSpec((1, d), lambda i, j: (0, 0)),
        ],
        out_specs=pl.BlockSpec((1, tm, d), lambda i, j: (i, j, 0)),
        out_shape=jax.ShapeDtypeStruct((b, s, d), F32),
        scratch_shapes=[pltpu.VMEM((tm + halo, d), BF16), pltpu.VMEM((tm, d), F32)],
        compiler_params=_params(("parallel", "parallel"), vmem),
        name=name,
    )(x3d, x3d, gpre, wup_t, cw_t, wdn_t, gpost)


def _kvq_kernel(x_ref, gkv_ref, gq_ref, wkv_ref, wq_ref, kv_ref, q_ref):
    x = x_ref[...]
    xr = x * _rms_scale(x)
    kv = jnp.dot((xr * gkv_ref[...]).astype(BF16), wkv_ref[...], preferred_element_type=F32)
    kv_ref[...] = kv.astype(BF16)
    q = jnp.dot((xr * gq_ref[...]).astype(BF16), wq_ref[...], preferred_element_type=F32)
    q_ref[...] = (q * (DIFF_HEAD_DIM ** -0.5 * LOG2_E)).astype(BF16)


def _kvq(x2d, g_kv, g_q, w_kv, w_q):
    t, d = x2d.shape
    n_kv, n_q = w_kv.shape[1], w_q.shape[1]
    tm = KVQ_TM
    vmem = (2 * tm * d * 4 + d * (n_kv + n_q) * 2 + 2 * tm * (n_kv + n_q) * 2
            + 6 * tm * d * 4 + 2 * tm * (n_kv + n_q) * 4 + 4 * MIB)
    return pl.pallas_call(
        _kvq_kernel,
        grid=(t // tm,),
        in_specs=[
            pl.BlockSpec((tm, d), lambda i: (i, 0)),
            pl.BlockSpec((1, d), lambda i: (0, 0)),
            pl.BlockSpec((1, d), lambda i: (0, 0)),
            _resident((d, n_kv)),
            _resident((d, n_q)),
        ],
        out_specs=[pl.BlockSpec((tm, n_kv), lambda i: (i, 0)),
                   pl.BlockSpec((tm, n_q), lambda i: (i, 0))],
        out_shape=[jax.ShapeDtypeStruct((t, n_kv), BF16),
                   jax.ShapeDtypeStruct((t, n_q), BF16)],
        compiler_params=_params(("parallel",), vmem),
        name="kv_q_proj",
    )(x2d, g_kv, g_q, w_kv, w_q)


def _diff_attn_kernel(q_ref, k_ref, v_ref, lam_ref, gsub_ref, o_ref,
                      q2t_ref, vt_ref, s0_ref, s1_ref, p0_ref, p1_ref, a0_ref, a1_ref,
                      m_ref, acc_ref, *, lambda_init):
    tq, tk, d, dv = ATT_TQ, ATT_TK, DIFF_HEAD_DIM, DIFF_V_DIM
    strip = LANES
    n_kv = vt_ref.shape[0]

    for j in range(n_kv):
        vt_ref[j, :dv, :] = v_ref[0, j * tk:(j + 1) * tk, :].astype(F32).T.astype(BF16)
    vt_ref[:, dv:, :] = jnp.ones((n_kv, BF16_ROWS, tk), BF16)

    def scores(j, s_ref):
        krows = pl.ds(pl.multiple_of(j * tk, tk), tk)
        s_ref[...] = jnp.dot(k_ref[0, krows, :], q2t_ref[...],
                             preferred_element_type=F32)

    def softmax(s_ref, p_ref, a_ref, diag_tile):
        for c in range(2 * tq // strip):
            lanes = slice(c * strip, (c + 1) * strip)
            q_off = (c * strip) % tq
            k_off = 0 if diag_tile is None else diag_tile * tk
            if diag_tile is not None and k_off > q_off + strip - 1:
                p_ref[:, lanes] = jnp.zeros((tk, strip), BF16)
                a_ref[:, lanes] = jnp.ones((1, strip), F32)
                continue
            s = s_ref[:, lanes]
            if diag_tile is not None and k_off + tk - 1 > q_off:
                rel = (lax.broadcasted_iota(jnp.int32, (tk, strip), 1)
                       - lax.broadcasted_iota(jnp.int32, (tk, strip), 0))
                s = jnp.where(rel >= k_off - q_off, s, MASK_VALUE)
            m_old = m_ref[:, lanes]
            m_new = jnp.maximum(m_old, jnp.max(s, axis=0, keepdims=True))
            m_ref[:, lanes] = m_new
            a_ref[:, lanes] = jnp.exp2(m_old - m_new)
            p_ref[:, lanes] = jnp.exp2(s - m_new).astype(BF16)

    def accumulate(j, p_ref, a_ref):
        pv = jnp.dot(vt_ref[j], p_ref[...], preferred_element_type=F32)
        acc_ref[...] = a_ref[...] * acc_ref[...] + pv

    def tile_pair(u, carry, *, diagonal):
        j0 = 2 * u
        softmax(s0_ref, p0_ref, a0_ref, 0 if diagonal else None)
        scores(j0 + 1, s1_ref)
        accumulate(jnp.maximum(j0 - 1, 0), p1_ref, a1_ref)
        softmax(s1_ref, p1_ref, a1_ref, 1 if diagonal else None)
        accumulate(j0, p0_ref, a0_ref)
        if not diagonal:
            scores(j0 + 2, s0_ref)
        return carry

    lp = lam_ref[...]
    lam = (jnp.exp(jnp.sum(lp[0:1] * lp[1:2], axis=-1, keepdims=True))
           - jnp.exp(jnp.sum(lp[2:3] * lp[3:4], axis=-1, keepdims=True)) + lambda_init)

    def q_tile(i, carry):
        rows = pl.ds(pl.multiple_of(i * tq, tq), tq)
        qt = q_ref[0, rows, :].astype(F32).T
        feat = lax.broadcasted_iota(jnp.int32, qt.shape, 0)
        q2t_ref[:, :tq] = jnp.where(feat < d, qt, 0.0).astype(BF16)
        q2t_ref[:, tq:] = jnp.where(feat >= d, qt, 0.0).astype(BF16)
        m_ref[...] = jnp.full_like(m_ref, MASK_VALUE)
        acc_ref[...] = jnp.zeros_like(acc_ref)
        p1_ref[...] = jnp.zeros_like(p1_ref)
        a1_ref[...] = jnp.ones_like(a1_ref)

        scores(0, s0_ref)
        lax.fori_loop(0, i, functools.partial(tile_pair, diagonal=False), 0)
        tile_pair(i, 0, diagonal=True)
        accumulate(2 * i + 1, p1_ref, a1_ref)

        inv_l = 1.0 / acc_ref[dv:dv + 1, :]
        o = (acc_ref[:dv, :tq] * inv_l[:, :tq]
             - lam * (acc_ref[:dv, tq:] * inv_l[:, tq:]))
        r = lax.rsqrt(jnp.mean(o * o, axis=0, keepdims=True) + EPS)
        o = o * r * gsub_ref[...] * (1.0 - lambda_init)
        o_ref[0, rows, :] = o.T.astype(BF16)
        return carry

    lax.fori_loop(0, q_ref.shape[1] // tq, q_tile, 0)


def _diff_attention(q3d, kv3d, lam_params, g_sub_col, lambda_init):
    b, s, _ = q3d.shape
    h, dv = DIFF_HEADS, DIFF_V_DIM
    tq, tk = ATT_TQ, ATT_TK
    assert tq == 2 * tk and s % tq == 0
    dv_aug = dv + BF16_ROWS
    vmem = (2 * 4 * s * dv * 2 + 2 * tq * dv * 2 + s * dv_aug * 2 + 2 * tk * 2 * tq * (4 + 2)
            + dv_aug * 2 * tq * 4 + 4 * tk * 2 * tq * 4 + 8 * MIB)

    def head_block(first):
        return pl.BlockSpec((1, s, dv), lambda bi, hi: (bi, 0, first + hi))

    return pl.pallas_call(
        functools.partial(_diff_attn_kernel, lambda_init=lambda_init),
        grid=(b, h),
        in_specs=[
            head_block(0),
            head_block(0),
            head_block(h),
            pl.BlockSpec(lam_params.shape, lambda bi, hi: (0, 0)),
            pl.BlockSpec((dv, 1), lambda bi, hi: (0, 0)),
        ],
        out_specs=head_block(0),
        out_shape=jax.ShapeDtypeStruct((b, s, h * dv), BF16),
        scratch_shapes=[
            pltpu.VMEM((dv, 2 * tq), BF16),
            pltpu.VMEM((s // tk, dv_aug, tk), BF16),
            pltpu.VMEM((tk, 2 * tq), F32),
            pltpu.VMEM((tk, 2 * tq), F32),
            pltpu.VMEM((tk, 2 * tq), BF16),
            pltpu.VMEM((tk, 2 * tq), BF16),
            pltpu.VMEM((1, 2 * tq), F32),
            pltpu.VMEM((1, 2 * tq), F32),
            pltpu.VMEM((1, 2 * tq), F32),
            pltpu.VMEM((dv_aug, 2 * tq), F32),
        ],
        compiler_params=_params(("parallel", "parallel"), vmem),
        name="diff_attention",
    )(q3d, kv3d, kv3d, lam_params, g_sub_col)


def kernel(x, a_norm_pre, a_norm_post, a_w_in, a_w_out, kv_norm, w_kv, b_norm_pre, b_norm_post,
           b_w_q, b_lambda, b_subln, b_w_out, ffn_norm_pre, ffn_norm_post, ffn_w_up, ffn_conv_w,
           ffn_conv_b, ffn_w_down):
    b, s, d = x.shape
    t = b * s

    half = RET_QK_DIM // 2
    inv = 1.0 / (ROPE_BASE ** jnp.linspace(0.0, 1.0, half, dtype=F32))
    ang = jnp.arange(s).astype(F32)[:, None] * inv[None, :]
    cos, sin = jnp.cos(ang), jnp.sin(ang)
    log_gamma = jnp.log1p(-jnp.power(2.0, -5.0 - jnp.arange(RET_HEADS, dtype=F32)))

    proj = _proj_in(x.reshape(t, d), a_norm_pre[0][None], a_w_in[0].astype(BF16), cos, sin, s)
    y = _retention(proj.reshape(b, s, -1), log_gamma)
    x = _post(y.reshape(t, -1), a_w_out[0].astype(BF16), a_norm_post[0][None],
              x.reshape(t, d), "ret_out_proj")
    x = _conv_ffn(x.reshape(b, s, d), ffn_norm_pre[0][None], ffn_w_up[0], ffn_conv_w[0],
                  ffn_conv_b[0], ffn_w_down[0], ffn_norm_post[0][None], "conv_ffn_0")

    layer = 1
    lambda_init = 0.8 - 0.6 * math.exp(-0.3 * layer)
    kv, q = _kvq(x.reshape(t, d), kv_norm[None], b_norm_pre[0][None],
                 w_kv.astype(BF16), b_w_q[0].astype(BF16))
    o = _diff_attention(q.reshape(b, s, -1), kv.reshape(b, s, -1), b_lambda[0],
                        b_subln[0][:, None], lambda_init)
    x = _post(o.reshape(t, -1), b_w_out[0].astype(BF16), b_norm_post[0][None],
              x.reshape(t, d), "attn_out_proj")
    x = _conv_ffn(x.reshape(b, s, d), ffn_norm_pre[1][None], ffn_w_up[1], ffn_conv_w[1],
                  ffn_conv_b[1], ffn_w_down[1], ffn_norm_post[1][None], "conv_ffn_1")
    return x
```

```python
import functools
import math

import jax
import jax.numpy as jnp
from jax import lax
from jax.experimental import pallas as pl
from jax.experimental.pallas import tpu as pltpu

F32 = jnp.float32
BF16 = jnp.bfloat16

D_MODEL = 1024
RET_HEADS = 4
RET_QK_DIM = D_MODEL // RET_HEADS
RET_V_DIM = 2 * RET_QK_DIM
RET_CHUNK = 128
DIFF_HEAD_DIM = 64
DIFF_HEADS = D_MODEL // (2 * DIFF_HEAD_DIM)
DIFF_V_DIM = 2 * DIFF_HEAD_DIM
D_FF = ((8 * D_MODEL // 3 + 127) // 128) * 128
ROPE_BASE = 10000.0
EPS = 1e-6
MASK_VALUE = -1e30
LOG2_E = math.log2(math.e)

BF16_ROWS = 16
LANES = 128
MXU_COLS = 256
MIB = 1024 * 1024

PROJ_TM = 1024
PROJ_TN = 1024
POST_TM = 512
KVQ_TM = 512
FFN_TM = 512
FFN_TF = 256
FFN_ROWS = 64
ATT_TQ = 512
ATT_TK = 256


def _params(semantics, vmem_bytes):
    return pltpu.CompilerParams(dimension_semantics=semantics,
                                vmem_limit_bytes=int(vmem_bytes))


def _resident(shape):
    zeros = (0,) * len(shape)
    return pl.BlockSpec(shape, lambda *_: zeros, pipeline_mode=pl.Buffered(1))


def _rms_scale(v):
    return lax.rsqrt(jnp.mean(v * v, axis=-1, keepdims=True) + EPS)


def _proj_in_kernel(x_ref, g_ref, w_ref, cos_ref, sin_ref, o_ref, xn_ref):
    j = pl.program_id(1)

    @pl.when(j == 0)
    def _():
        x = x_ref[...]
        xn_ref[...] = (x * _rms_scale(x) * g_ref[...]).astype(BF16)

    def acc():
        return jnp.dot(xn_ref[...], w_ref[...], preferred_element_type=F32)

    def rotary(scale):
        a = acc()
        cos = cos_ref[...]
        sin = sin_ref[...]
        half = RET_QK_DIM // 2
        for h in range(PROJ_TN // RET_QK_DIM):
            lo = h * RET_QK_DIM
            x1 = a[:, lo:lo + half]
            x2 = a[:, lo + half:lo + 2 * half]
            o_ref[:, lo:lo + half] = ((x1 * cos - x2 * sin) * scale).astype(BF16)
            o_ref[:, lo + half:lo + 2 * half] = ((x1 * sin + x2 * cos) * scale).astype(BF16)

    n_qk = RET_HEADS * RET_QK_DIM // PROJ_TN
    n_v = RET_HEADS * RET_V_DIM // PROJ_TN

    @pl.when(j < n_qk)
    def _():
        rotary(1.0)

    @pl.when((j >= n_qk) & (j < 2 * n_qk))
    def _():
        rotary(RET_QK_DIM ** -0.5)

    @pl.when((j >= 2 * n_qk) & (j < 2 * n_qk + n_v))
    def _():
        o_ref[...] = acc().astype(BF16)

    @pl.when(j >= 2 * n_qk + n_v)
    def _():
        a = acc()
        o_ref[...] = (a / (1.0 + jnp.exp(-a))).astype(BF16)


def _proj_in(x2d, gain, w, cos, sin, seq):
    t, d = x2d.shape
    n = w.shape[1]
    tiles_per_seq = seq // PROJ_TM
    vmem = (2 * PROJ_TM * d * 4 + 2 * d * PROJ_TN * 2 + 2 * PROJ_TM * PROJ_TN * 2
            + PROJ_TM * d * 2 + 4 * PROJ_TM * PROJ_TN * 4 + 4 * MIB)
    return pl.pallas_call(
        _proj_in_kernel,
        grid=(t // PROJ_TM, n // PROJ_TN),
        in_specs=[
            pl.BlockSpec((PROJ_TM, d), lambda i, j: (i, 0)),
            pl.BlockSpec((1, d), lambda i, j: (0, 0)),
            pl.BlockSpec((d, PROJ_TN), lambda i, j: (0, j)),
            pl.BlockSpec((PROJ_TM, RET_QK_DIM // 2), lambda i, j: (i % tiles_per_seq, 0)),
            pl.BlockSpec((PROJ_TM, RET_QK_DIM // 2), lambda i, j: (i % tiles_per_seq, 0)),
        ],
        out_specs=pl.BlockSpec((PROJ_TM, PROJ_TN), lambda i, j: (i, j)),
        out_shape=jax.ShapeDtypeStruct((t, n), BF16),
        scratch_shapes=[pltpu.VMEM((PROJ_TM, d), BF16)],
        compiler_params=_params(("parallel", "arbitrary"), vmem),
        name="ret_proj_in",
    )(x2d, gain, w, cos, sin)


def _retention_kernel(lg_ref, q_ref, k_ref, v_ref, sg_ref, o_ref, state_ref):
    c_len = RET_CHUNK
    lg = lg_ref[pl.program_id(1)]
    row = lax.broadcasted_iota(jnp.int32, (c_len, c_len), 0)
    col = lax.broadcasted_iota(jnp.int32, (c_len, c_len), 1)
    rel = (row - col).astype(F32)
    decay_mask = jnp.where(rel >= 0, jnp.exp(lg * jnp.maximum(rel, 0.0)), 0.0)
    idx = lax.broadcasted_iota(jnp.int32, (c_len, 1), 0).astype(F32)
    q_decay = jnp.exp(lg * (idx + 1.0))
    k_decay = jnp.exp(lg * (c_len - 1.0 - idx))
    chunk_decay = jnp.exp(lg * jnp.full((1, 1), c_len, F32))

    state_ref[...] = jnp.zeros_like(state_ref)

    def chunk(c, carry):
        rows = pl.ds(pl.multiple_of(c * c_len, c_len), c_len)
        q = q_ref[0, rows, :]
        k = k_ref[0, rows, :]
        v = v_ref[0, rows, :]
        s = lax.dot_general(q, k, (((1,), (1,)), ((), ())),
                            preferred_element_type=F32) * decay_mask
        inner = jnp.dot(s.astype(BF16), v, preferred_element_type=F32)
        state = state_ref[...]
        cross = jnp.dot(q, state.astype(BF16), preferred_element_type=F32) * q_decay
        kd = (k.astype(F32) * k_decay).astype(BF16)
        state_ref[...] = state * chunk_decay + lax.dot_general(
            kd, v, (((0,), (0,)), ((), ())), preferred_element_type=F32)
        out = inner + cross
        gate = sg_ref[0, rows, :].astype(F32)
        o_ref[0, rows, :] = (out * _rms_scale(out) * gate).astype(BF16)
        return carry

    lax.fori_loop(0, q_ref.shape[1] // c_len, chunk, 0)


def _retention(proj3d, log_gamma):
    b, s, _ = proj3d.shape
    dk, dv, h = RET_QK_DIM, RET_V_DIM, RET_HEADS
    k_blk0 = h * dk // dk
    v_blk0 = 2 * h * dk // dv
    g_blk0 = (2 * h * dk + h * dv) // dv
    vmem = 2 * s * (2 * dk + 3 * dv) * 2 + dk * dv * 4 + 8 * MIB
    return pl.pallas_call(
        _retention_kernel,
        grid=(b, h),
        in_specs=[
            pl.BlockSpec(memory_space=pltpu.SMEM),
            pl.BlockSpec((1, s, dk), lambda i, j: (i, 0, j)),
            pl.BlockSpec((1, s, dk), lambda i, j: (i, 0, k_blk0 + j)),
            pl.BlockSpec((1, s, dv), lambda i, j: (i, 0, v_blk0 + j)),
            pl.BlockSpec((1, s, dv), lambda i, j: (i, 0, g_blk0 + j)),
        ],
        out_specs=pl.BlockSpec((1, s, dv), lambda i, j: (i, 0, j)),
        out_shape=jax.ShapeDtypeStruct((b, s, h * dv), BF16),
        scratch_shapes=[pltpu.VMEM((dk, dv), F32)],
        compiler_params=_params(("parallel", "parallel"), vmem),
        name="retention_core",
    )(log_gamma, proj3d, proj3d, proj3d, proj3d)


def _post_kernel(y_ref, w_ref, g_ref, x_ref, o_ref):
    h = jnp.dot(y_ref[...], w_ref[...], preferred_element_type=F32)
    o_ref[...] = x_ref[...] + h * _rms_scale(h) * g_ref[...]


def _post(y2d, w, gain, x2d, name):
    t, kdim = y2d.shape
    d = w.shape[1]
    vmem = (2 * POST_TM * kdim * 2 + kdim * d * 2 + 4 * POST_TM * d * 4
            + 2 * POST_TM * d * 4 + 4 * MIB)
    return pl.pallas_call(
        _post_kernel,
        grid=(t // POST_TM,),
        in_specs=[
            pl.BlockSpec((POST_TM, kdim), lambda i: (i, 0)),
            _resident((kdim, d)),
            pl.BlockSpec((1, d), lambda i: (0, 0)),
            pl.BlockSpec((POST_TM, d), lambda i: (i, 0)),
        ],
        out_specs=pl.BlockSpec((POST_TM, d), lambda i: (i, 0)),
        out_shape=jax.ShapeDtypeStruct((t, d), F32),
        compiler_params=_params(("parallel",), vmem),
        name=name,
    )(y2d, w, gain, x2d)


def _ffn_kernel(x_ref, xh_ref, gpre_ref, wup_ref, cw_ref, wdn_ref, gpost_ref, o_ref,
                xn_ref, h0_ref, h1_ref, act_ref):
    n_f = wup_ref.shape[0] // 2
    tm, tf = act_ref.shape[0], wup_ref.shape[2]
    halo = BF16_ROWS
    x = x_ref[0]
    gpre = gpre_ref[...]
    xn_ref[halo:, :] = (x * _rms_scale(x) * gpre).astype(BF16)
    xh = xh_ref[0]
    xhn = jnp.where(pl.program_id(1) > 0, xh * _rms_scale(xh) * gpre, 0.0)
    xn_ref[:halo, :] = xhn.astype(BF16)

    def up_project(f, h_ref):
        xn = xn_ref[...]
        h_ref[0] = jnp.dot(xn, wup_ref[f], preferred_element_type=F32)
        h_ref[1] = jnp.dot(xn, wup_ref[n_f + f], preferred_element_type=F32)

    def conv(h_ref, half, idx, row0):
        cw = cw_ref[idx]
        rows = lambda back: pl.ds(halo + row0 - back, FFN_ROWS)
        return (cw[3:4] + cw[2:3] * h_ref[half, rows(0), :]
                + cw[1:2] * h_ref[half, rows(1), :] + cw[0:1] * h_ref[half, rows(2), :])

    def gate(f, h_ref):
        for row0 in range(0, tm, FFN_ROWS):
            g = conv(h_ref, 0, f, row0)
            u = conv(h_ref, 1, n_f + f, row0)
            act_ref[row0:row0 + FFN_ROWS, f * tf:(f + 1) * tf] = (
                g / (1.0 + jnp.exp(-g)) * u).astype(BF16)

    h_refs = (h0_ref, h1_ref)
    up_project(0, h_refs[0])
    for f in range(n_f):
        if f + 1 < n_f:
            up_project(f + 1, h_refs[(f + 1) % 2])
        gate(f, h_refs[f % 2])
    h = jnp.dot(act_ref[...], wdn_ref[...], preferred_element_type=F32)
    o_ref[0] = x + h * _rms_scale(h) * gpost_ref[...]


def _conv_ffn(x3d, gpre, w_up, conv_w, conv_b, w_down, gpost, name):
    b, s, d = x3d.shape
    f_dim = w_down.shape[0]
    n_f = f_dim // FFN_TF
    tm, halo = FFN_TM, BF16_ROWS
    wup_t = w_up.astype(BF16).reshape(d, 2 * n_f, FFN_TF).transpose(1, 0, 2)
    wdn = w_down.astype(BF16)
    cw = jnp.concatenate([conv_w, conv_b[None, :], jnp.zeros((4, 2 * f_dim), F32)], axis=0)
    cw_t = cw.reshape(8, 2 * n_f, FFN_TF).transpose(1, 0, 2)
    halo_blocks = tm // halo
    vmem = (2 * f_dim * d * 2 + f_dim * d * 2 + 4 * tm * d * 4 + (tm + halo) * d * 2
            + tm * f_dim * 2 + 4 * (tm + halo) * FFN_TF * 4 + 4 * tm * d * 4 + 4 * MIB)
    return pl.pallas_call(
        _ffn_kernel,
        grid=(b, s // tm),
        in_specs=[
            pl.BlockSpec((1, tm, d), lambda i, j: (i, j, 0)),
            pl.BlockSpec((1, halo, d), lambda i, j: (i, jnp.maximum(j * halo_blocks - 1, 0), 0)),
            pl.BlockSpec((1, d), lambda i, j: (0, 0)),
            _resident((2 * n_f, d, FFN_TF)),
            _resident((2 * n_f, 8, FFN_TF)),
            _resident((f_dim, d)),
            pl.BlockSpec((1, d), lambda i, j: (0, 0)),
        ],
        out_specs=pl.BlockSpec((1, tm, d), lambda i, j: (i, j, 0)),
        out_shape=jax.ShapeDtypeStruct((b, s, d), F32),
        scratch_shapes=[
            pltpu.VMEM((tm + halo, d), BF16),
            pltpu.VMEM((2, tm + halo, FFN_TF), F32),
            pltpu.VMEM((2, tm + halo, FFN_TF), F32),
            pltpu.VMEM((tm, f_dim), BF16),
        ],
        compiler_params=_params(("parallel", "parallel"), vmem),
        name=name,
    )(x3d, x3d, gpre, wup_t, cw_t, wdn, gpost)


def _kvq_kernel(x_ref, gkv_ref, gq_ref, wkv_ref, wq_ref, kv_ref, q_ref):
    x = x_ref[...]
    xr = x * _rms_scale(x)
    kv = jnp.dot((xr * gkv_ref[...]).astype(BF16), wkv_ref[...], preferred_element_type=F32)
    kv_ref[...] = kv.astype(BF16)
    q = jnp.dot((xr * gq_ref[...]).astype(BF16), wq_ref[...], preferred_element_type=F32)
    q_ref[...] = (q * (DIFF_HEAD_DIM ** -0.5 * LOG2_E)).astype(BF16)


def _kvq(x2d, g_kv, g_q, w_kv, w_q):
    t, d = x2d.shape
    n_kv, n_q = w_kv.shape[1], w_q.shape[1]
    tm = KVQ_TM
    vmem = (2 * tm * d * 4 + d * (n_kv + n_q) * 2 + 2 * tm * (n_kv + n_q) * 2
            + 6 * tm * d * 4 + 2 * tm * (n_kv + n_q) * 4 + 4 * MIB)
    return pl.pallas_call(
        _kvq_kernel,
        grid=(t // tm,),
        in_specs=[
            pl.BlockSpec((tm, d), lambda i: (i, 0)),
            pl.BlockSpec((1, d), lambda i: (0, 0)),
            pl.BlockSpec((1, d), lambda i: (0, 0)),
            _resident((d, n_kv)),
            _resident((d, n_q)),
        ],
        out_specs=[pl.BlockSpec((tm, n_kv), lambda i: (i, 0)),
                   pl.BlockSpec((tm, n_q), lambda i: (i, 0))],
        out_shape=[jax.ShapeDtypeStruct((t, n_kv), BF16),
                   jax.ShapeDtypeStruct((t, n_q), BF16)],
        compiler_params=_params(("parallel",), vmem),
        name="kv_q_proj",
    )(x2d, g_kv, g_q, w_kv, w_q)


def _diff_attn_kernel(q_ref, k_ref, v_ref, lam_ref, gsub_ref, o_ref,
                      q2t_ref, vt_ref, s0_ref, s1_ref, p0_ref, p1_ref, a0_ref, a1_ref,
                      m_ref, acc_ref, *, lambda_init):
    tq, tk, d, dv = ATT_TQ, ATT_TK, DIFF_HEAD_DIM, DIFF_V_DIM
    strip = LANES
    n_kv = vt_ref.shape[0]

    for j in range(n_kv):
        vt_ref[j, :dv, :] = v_ref[0, j * tk:(j + 1) * tk, :].astype(F32).T.astype(BF16)
    vt_ref[:, dv:, :] = jnp.ones((n_kv, BF16_ROWS, tk), BF16)

    n_pieces = 2 * tq // MXU_COLS
    strips_per_piece = MXU_COLS // strip

    def piece(r):
        return slice(r * MXU_COLS, (r + 1) * MXU_COLS)

    def scores(j, s_ref, r):
        krows = pl.ds(pl.multiple_of(j * tk, tk), tk)
        s_ref[:, piece(r)] = jnp.dot(k_ref[0, krows, :], q2t_ref[:, piece(r)],
                                     preferred_element_type=F32)

    def softmax(s_ref, p_ref, a_ref, diag_tile, r):
        for c in range(r * strips_per_piece, (r + 1) * strips_per_piece):
            lanes = slice(c * strip, (c + 1) * strip)
            q_off = (c * strip) % tq
            k_off = 0 if diag_tile is None else diag_tile * tk
            if diag_tile is not None and k_off > q_off + strip - 1:
                p_ref[:, lanes] = jnp.zeros((tk, strip), BF16)
                a_ref[:, lanes] = jnp.ones((1, strip), F32)
                continue
            s = s_ref[:, lanes]
            if diag_tile is not None and k_off + tk - 1 > q_off:
                rel = (lax.broadcasted_iota(jnp.int32, (tk, strip), 1)
                       - lax.broadcasted_iota(jnp.int32, (tk, strip), 0))
                s = jnp.where(rel >= k_off - q_off, s, MASK_VALUE)
            m_old = m_ref[:, lanes]
            m_new = jnp.maximum(m_old, jnp.max(s, axis=0, keepdims=True))
            m_ref[:, lanes] = m_new
            a_ref[:, lanes] = jnp.exp2(m_old - m_new)
            p_ref[:, lanes] = jnp.exp2(s - m_new).astype(BF16)

    def accumulate(j, p_ref, a_ref, r):
        pv = jnp.dot(vt_ref[j], p_ref[:, piece(r)],
                     preferred_element_type=F32)
        acc_ref[:, piece(r)] = a_ref[:, piece(r)] * acc_ref[:, piece(r)] + pv

    def tile_pair(u, carry, *, diagonal):
        j0 = 2 * u
        j_prev = jnp.maximum(j0 - 1, 0)
        for r in range(n_pieces):
            scores(j0 + 1, s1_ref, r)
            softmax(s0_ref, p0_ref, a0_ref, 0 if diagonal else None, r)
            accumulate(j_prev, p1_ref, a1_ref, r)
        for r in range(n_pieces):
            if not diagonal:
                scores(j0 + 2, s0_ref, r)
            softmax(s1_ref, p1_ref, a1_ref, 1 if diagonal else None, r)
            accumulate(j0, p0_ref, a0_ref, r)
        return carry

    lp = lam_ref[...]
    lam = (jnp.exp(jnp.sum(lp[0:1] * lp[1:2], axis=-1, keepdims=True))
           - jnp.exp(jnp.sum(lp[2:3] * lp[3:4], axis=-1, keepdims=True)) + lambda_init)

    def q_tile(i, carry):
        rows = pl.ds(pl.multiple_of(i * tq, tq), tq)
        qt = q_ref[0, rows, :].astype(F32).T
        feat = lax.broadcasted_iota(jnp.int32, qt.shape, 0)
        q2t_ref[:, :tq] = jnp.where(feat < d, qt, 0.0).astype(BF16)
        q2t_ref[:, tq:] = jnp.where(feat >= d, qt, 0.0).astype(BF16)
        m_ref[...] = jnp.full_like(m_ref, MASK_VALUE)
        acc_ref[...] = jnp.zeros_like(acc_ref)
        p1_ref[...] = jnp.zeros_like(p1_ref)
        a1_ref[...] = jnp.ones_like(a1_ref)

        for r in range(n_pieces):
            scores(0, s0_ref, r)
        lax.fori_loop(0, i, functools.partial(tile_pair, diagonal=False), 0)
        tile_pair(i, 0, diagonal=True)
        for r in range(n_pieces):
            accumulate(2 * i + 1, p1_ref, a1_ref, r)

        inv_l = 1.0 / acc_ref[dv:dv + 1, :]
        o = (acc_ref[:dv, :tq] * inv_l[:, :tq]
             - lam * (acc_ref[:dv, tq:] * inv_l[:, tq:]))
        r = lax.rsqrt(jnp.mean(o * o, axis=0, keepdims=True) + EPS)
        o = o * r * gsub_ref[...] * (1.0 - lambda_init)
        o_ref[0, rows, :] = o.T.astype(BF16)
        return carry

    lax.fori_loop(0, q_ref.shape[1] // tq, q_tile, 0)


def _diff_attention(q3d, kv3d, lam_params, g_sub_col, lambda_init):
    b, s, _ = q3d.shape
    h, dv = DIFF_HEADS, DIFF_V_DIM
    tq, tk = ATT_TQ, ATT_TK
    assert tq == 2 * tk and s % tq == 0
    dv_aug = dv + BF16_ROWS
    vmem = (2 * 4 * s * dv * 2 + 2 * tq * dv * 2 + s * dv_aug * 2 + 2 * tk * 2 * tq * (4 + 2)
            + dv_aug * 2 * tq * 4 + 4 * tk * 2 * tq * 4 + 8 * MIB)

    def head_block(first):
        return pl.BlockSpec((1, s, dv), lambda bi, hi: (bi, 0, first + hi))

    return pl.pallas_call(
        functools.partial(_diff_attn_kernel, lambda_init=lambda_init),
        grid=(b, h),
        in_specs=[
            head_block(0),
            head_block(0),
            head_block(h),
            pl.BlockSpec(lam_params.shape, lambda bi, hi: (0, 0)),
            pl.BlockSpec((dv, 1), lambda bi, hi: (0, 0)),
        ],
        out_specs=head_block(0),
        out_shape=jax.ShapeDtypeStruct((b, s, h * dv), BF16),
        scratch_shapes=[
            pltpu.VMEM((dv, 2 * tq), BF16),
            pltpu.VMEM((s // tk, dv_aug, tk), BF16),
            pltpu.VMEM((tk, 2 * tq), F32),
            pltpu.VMEM((tk, 2 * tq), F32),
            pltpu.VMEM((tk, 2 * tq), BF16),
            pltpu.VMEM((tk, 2 * tq), BF16),
            pltpu.VMEM((1, 2 * tq), F32),
            pltpu.VMEM((1, 2 * tq), F32),
            pltpu.VMEM((1, 2 * tq), F32),
            pltpu.VMEM((dv_aug, 2 * tq), F32),
        ],
        compiler_params=_params(("parallel", "parallel"), vmem),
        name="diff_attention",
    )(q3d, kv3d, kv3d, lam_params, g_sub_col)


def kernel(x, a_norm_pre, a_norm_post, a_w_in, a_w_out, kv_norm, w_kv, b_norm_pre, b_norm_post,
           b_w_q, b_lambda, b_subln, b_w_out, ffn_norm_pre, ffn_norm_post, ffn_w_up, ffn_conv_w,
           ffn_conv_b, ffn_w_down):
    b, s, d = x.shape
    t = b * s

    half = RET_QK_DIM // 2
    inv = 1.0 / (ROPE_BASE ** jnp.linspace(0.0, 1.0, half, dtype=F32))
    ang = jnp.arange(s).astype(F32)[:, None] * inv[None, :]
    cos, sin = jnp.cos(ang), jnp.sin(ang)
    log_gamma = jnp.log1p(-jnp.power(2.0, -5.0 - jnp.arange(RET_HEADS, dtype=F32)))

    proj = _proj_in(x.reshape(t, d), a_norm_pre[0][None], a_w_in[0].astype(BF16), cos, sin, s)
    y = _retention(proj.reshape(b, s, -1), log_gamma)
    x = _post(y.reshape(t, -1), a_w_out[0].astype(BF16), a_norm_post[0][None],
              x.reshape(t, d), "ret_out_proj")
    x = _conv_ffn(x.reshape(b, s, d), ffn_norm_pre[0][None], ffn_w_up[0], ffn_conv_w[0],
                  ffn_conv_b[0], ffn_w_down[0], ffn_norm_post[0][None], "conv_ffn_0")

    layer = 1
    lambda_init = 0.8 - 0.6 * math.exp(-0.3 * layer)
    kv, q = _kvq(x.reshape(t, d), kv_norm[None], b_norm_pre[0][None],
                 w_kv.astype(BF16), b_w_q[0].astype(BF16))
    o = _diff_attention(q.reshape(b, s, -1), kv.reshape(b, s, -1), b_lambda[0],
                        b_subln[0][:, None], lambda_init)
    x = _post(o.reshape(t, -1), b_w_out[0].astype(BF16), b_norm_post[0][None],
              x.reshape(t, d), "attn_out_proj")
    x = _conv_ffn(x.reshape(b, s, d), ffn_norm_pre[1][None], ffn_w_up[1], ffn_conv_w[1],
                  ffn_conv_b[1], ffn_w_down[1], ffn_norm_post[1][None], "conv_ffn_1")
    return x
```

```python
import functools
import math

import jax
import jax.numpy as jnp
from jax import lax
from jax.experimental import pallas as pl
from jax.experimental.pallas import tpu as pltpu

F32 = jnp.float32
BF16 = jnp.bfloat16

D_MODEL = 1024
RET_HEADS = 4
RET_QK_DIM = D_MODEL // RET_HEADS
RET_V_DIM = 2 * RET_QK_DIM
RET_CHUNK = 256
DIFF_HEAD_DIM = 64
DIFF_HEADS = D_MODEL // (2 * DIFF_HEAD_DIM)
DIFF_V_DIM = 2 * DIFF_HEAD_DIM
D_FF = ((8 * D_MODEL // 3 + 127) // 128) * 128
ROPE_BASE = 10000.0
EPS = 1e-6
MASK_VALUE = -1e30
LOG2_E = math.log2(math.e)

BF16_ROWS = 16
LANES = 128
MXU_COLS = 256
MIB = 1024 * 1024

PROJ_TM = 1024
PROJ_TN = 1024
POST_TM = 512
KVQ_TM = 512
FFN_TM = 512
FFN_TF = 256
FFN_ROWS = 64
ATT_TQ = 512
ATT_TK = 256


def _params(semantics, vmem_bytes):
    return pltpu.CompilerParams(dimension_semantics=semantics,
                                vmem_limit_bytes=int(vmem_bytes))


def _resident(shape):
    zeros = (0,) * len(shape)
    return pl.BlockSpec(shape, lambda *_: zeros, pipeline_mode=pl.Buffered(1))


def _rms_scale(v):
    return lax.rsqrt(jnp.mean(v * v, axis=-1, keepdims=True) + EPS)


def _proj_in_kernel(x_ref, g_ref, w_ref, cos_ref, sin_ref, o_ref, xn_ref):
    j = pl.program_id(1)

    @pl.when(j == 0)
    def _():
        x = x_ref[...]
        xn_ref[...] = (x * _rms_scale(x) * g_ref[...]).astype(BF16)

    def acc():
        return jnp.dot(xn_ref[...], w_ref[...], preferred_element_type=F32)

    def rotary(scale):
        a = acc()
        cos = cos_ref[...]
        sin = sin_ref[...]
        half = RET_QK_DIM // 2
        for h in range(PROJ_TN // RET_QK_DIM):
            lo = h * RET_QK_DIM
            x1 = a[:, lo:lo + half]
            x2 = a[:, lo + half:lo + 2 * half]
            o_ref[:, lo:lo + half] = ((x1 * cos - x2 * sin) * scale).astype(BF16)
            o_ref[:, lo + half:lo + 2 * half] = ((x1 * sin + x2 * cos) * scale).astype(BF16)

    n_qk = RET_HEADS * RET_QK_DIM // PROJ_TN
    n_v = RET_HEADS * RET_V_DIM // PROJ_TN

    @pl.when(j < n_qk)
    def _():
        rotary(1.0)

    @pl.when((j >= n_qk) & (j < 2 * n_qk))
    def _():
        rotary(RET_QK_DIM ** -0.5)

    @pl.when((j >= 2 * n_qk) & (j < 2 * n_qk + n_v))
    def _():
        o_ref[...] = acc().astype(BF16)

    @pl.when(j >= 2 * n_qk + n_v)
    def _():
        a = acc()
        o_ref[...] = (a / (1.0 + jnp.exp(-a))).astype(BF16)


def _proj_in(x2d, gain, w, cos, sin, seq):
    t, d = x2d.shape
    n = w.shape[1]
    tiles_per_seq = seq // PROJ_TM
    vmem = (2 * PROJ_TM * d * 4 + 2 * d * PROJ_TN * 2 + 2 * PROJ_TM * PROJ_TN * 2
            + PROJ_TM * d * 2 + 4 * PROJ_TM * PROJ_TN * 4 + 4 * MIB)
    return pl.pallas_call(
        _proj_in_kernel,
        grid=(t // PROJ_TM, n // PROJ_TN),
        in_specs=[
            pl.BlockSpec((PROJ_TM, d), lambda i, j: (i, 0)),
            pl.BlockSpec((1, d), lambda i, j: (0, 0)),
            pl.BlockSpec((d, PROJ_TN), lambda i, j: (0, j)),
            pl.BlockSpec((PROJ_TM, RET_QK_DIM // 2), lambda i, j: (i % tiles_per_seq, 0)),
            pl.BlockSpec((PROJ_TM, RET_QK_DIM // 2), lambda i, j: (i % tiles_per_seq, 0)),
        ],
        out_specs=pl.BlockSpec((PROJ_TM, PROJ_TN), lambda i, j: (i, j)),
        out_shape=jax.ShapeDtypeStruct((t, n), BF16),
        scratch_shapes=[pltpu.VMEM((PROJ_TM, d), BF16)],
        compiler_params=_params(("parallel", "arbitrary"), vmem),
        name="ret_proj_in",
    )(x2d, gain, w, cos, sin)


def _retention_kernel(lg_ref, q_ref, k_ref, v_ref, sg_ref, o_ref, state_ref):
    c_len = RET_CHUNK
    lg = lg_ref[pl.program_id(1)]
    row = lax.broadcasted_iota(jnp.int32, (c_len, c_len), 0)
    col = lax.broadcasted_iota(jnp.int32, (c_len, c_len), 1)
    rel = (row - col).astype(F32)
    decay_mask = jnp.where(rel >= 0, jnp.exp(lg * jnp.maximum(rel, 0.0)), 0.0)
    idx = lax.broadcasted_iota(jnp.int32, (c_len, 1), 0).astype(F32)
    q_decay = jnp.exp(lg * (idx + 1.0))
    k_decay = jnp.exp(lg * (c_len - 1.0 - idx))
    chunk_decay = jnp.exp(lg * jnp.full((1, 1), c_len, F32))

    state_ref[...] = jnp.zeros_like(state_ref)

    def chunk(c, carry):
        rows = pl.ds(pl.multiple_of(c * c_len, c_len), c_len)
        q = q_ref[0, rows, :]
        k = k_ref[0, rows, :]
        v = v_ref[0, rows, :]
        s = lax.dot_general(q, k, (((1,), (1,)), ((), ())), preferred_element_type=F32)
        state = state_ref[...]
        cross = jnp.dot(q, state.astype(BF16), preferred_element_type=F32)
        kd = (k.astype(F32) * k_decay).astype(BF16)
        update = lax.dot_general(kd, v, (((0,), (0,)), ((), ())), preferred_element_type=F32)
        inner = jnp.dot((s * decay_mask).astype(BF16), v, preferred_element_type=F32)
        state_ref[...] = state * chunk_decay + update
        out = inner + cross * q_decay
        gate = sg_ref[0, rows, :].astype(F32)
        o_ref[0, rows, :] = (out * _rms_scale(out) * gate).astype(BF16)
        return carry

    lax.fori_loop(0, q_ref.shape[1] // c_len, chunk, 0)


def _retention(proj3d, log_gamma):
    b, s, _ = proj3d.shape
    dk, dv, h = RET_QK_DIM, RET_V_DIM, RET_HEADS
    k_blk0 = h * dk // dk
    v_blk0 = 2 * h * dk // dv
    g_blk0 = (2 * h * dk + h * dv) // dv
    vmem = 2 * s * (2 * dk + 3 * dv) * 2 + dk * dv * 4 + 8 * MIB
    return pl.pallas_call(
        _retention_kernel,
        grid=(b, h),
        in_specs=[
            pl.BlockSpec(memory_space=pltpu.SMEM),
            pl.BlockSpec((1, s, dk), lambda i, j: (i, 0, j)),
            pl.BlockSpec((1, s, dk), lambda i, j: (i, 0, k_blk0 + j)),
            pl.BlockSpec((1, s, dv), lambda i, j: (i, 0, v_blk0 + j)),
            pl.BlockSpec((1, s, dv), lambda i, j: (i, 0, g_blk0 + j)),
        ],
        out_specs=pl.BlockSpec((1, s, dv), lambda i, j: (i, 0, j)),
        out_shape=jax.ShapeDtypeStruct((b, s, h * dv), BF16),
        scratch_shapes=[pltpu.VMEM((dk, dv), F32)],
        compiler_params=_params(("parallel", "parallel"), vmem),
        name="retention_core",
    )(log_gamma, proj3d, proj3d, proj3d, proj3d)


def _post_kernel(y_ref, w_ref, g_ref, x_ref, o_ref):
    h = jnp.dot(y_ref[...], w_ref[...], preferred_element_type=F32)
    o_ref[...] = x_ref[...] + h * _rms_scale(h) * g_ref[...]


def _post(y2d, w, gain, x2d, name):
    t, kdim = y2d.shape
    d = w.shape[1]
    vmem = (2 * POST_TM * kdim * 2 + kdim * d * 2 + 4 * POST_TM * d * 4
            + 2 * POST_TM * d * 4 + 4 * MIB)
    return pl.pallas_call(
        _post_kernel,
        grid=(t // POST_TM,),
        in_specs=[
            pl.BlockSpec((POST_TM, kdim), lambda i: (i, 0)),
            _resident((kdim, d)),
            pl.BlockSpec((1, d), lambda i: (0, 0)),
            pl.BlockSpec((POST_TM, d), lambda i: (i, 0)),
        ],
        out_specs=pl.BlockSpec((POST_TM, d), lambda i: (i, 0)),
        out_shape=jax.ShapeDtypeStruct((t, d), F32),
        compiler_params=_params(("parallel",), vmem),
        name=name,
    )(y2d, w, gain, x2d)


def _ffn_kernel(x_ref, xh_ref, gpre_ref, wup_ref, cw_ref, wdn_ref, gpost_ref, o_ref,
                xn_ref, h0_ref, h1_ref, act_ref):
    n_f = wup_ref.shape[0] // 2
    tm, tf = act_ref.shape[0], wup_ref.shape[2]
    halo = BF16_ROWS
    x = x_ref[0]
    gpre = gpre_ref[...]
    xn_ref[halo:, :] = (x * _rms_scale(x) * gpre).astype(BF16)
    xh = xh_ref[0]
    xhn = jnp.where(pl.program_id(1) > 0, xh * _rms_scale(xh) * gpre, 0.0)
    xn_ref[:halo, :] = xhn.astype(BF16)

    def up_project(f, h_ref):
        xn = xn_ref[...]
        h_ref[0] = jnp.dot(xn, wup_ref[f], preferred_element_type=F32)
        h_ref[1] = jnp.dot(xn, wup_ref[n_f + f], preferred_element_type=F32)

    def conv(h_ref, half, idx, row0):
        cw = cw_ref[idx]
        rows = lambda back: pl.ds(halo + row0 - back, FFN_ROWS)
        return (cw[3:4] + cw[2:3] * h_ref[half, rows(0), :]
                + cw[1:2] * h_ref[half, rows(1), :] + cw[0:1] * h_ref[half, rows(2), :])

    def gate(f, h_ref):
        for row0 in range(0, tm, FFN_ROWS):
            g = conv(h_ref, 0, f, row0)
            u = conv(h_ref, 1, n_f + f, row0)
            act_ref[row0:row0 + FFN_ROWS, f * tf:(f + 1) * tf] = (
                g / (1.0 + jnp.exp(-g)) * u).astype(BF16)

    h_refs = (h0_ref, h1_ref)
    up_project(0, h_refs[0])
    for f in range(n_f):
        if f + 1 < n_f:
            up_project(f + 1, h_refs[(f + 1) % 2])
        gate(f, h_refs[f % 2])
    h = jnp.dot(act_ref[...], wdn_ref[...], preferred_element_type=F32)
    o_ref[0] = x + h * _rms_scale(h) * gpost_ref[...]


def _conv_ffn(x3d, gpre, w_up, conv_w, conv_b, w_down, gpost, name):
    b, s, d = x3d.shape
    f_dim = w_down.shape[0]
    n_f = f_dim // FFN_TF
    tm, halo = FFN_TM, BF16_ROWS
    wup_t = w_up.astype(BF16).reshape(d, 2 * n_f, FFN_TF).transpose(1, 0, 2)
    wdn = w_down.astype(BF16)
    cw = jnp.concatenate([conv_w, conv_b[None, :], jnp.zeros((4, 2 * f_dim), F32)], axis=0)
    cw_t = cw.reshape(8, 2 * n_f, FFN_TF).transpose(1, 0, 2)
    halo_blocks = tm // halo
    vmem = (2 * f_dim * d * 2 + f_dim * d * 2 + 4 * tm * d * 4 + (tm + halo) * d * 2
            + tm * f_dim * 2 + 4 * (tm + halo) * FFN_TF * 4 + 4 * tm * d * 4 + 4 * MIB)
    return pl.pallas_call(
        _ffn_kernel,
        grid=(b, s // tm),
        in_specs=[
            pl.BlockSpec((1, tm, d), lambda i, j: (i, j, 0)),
            pl.BlockSpec((1, halo, d), lambda i, j: (i, jnp.maximum(j * halo_blocks - 1, 0), 0)),
            pl.BlockSpec((1, d), lambda i, j: (0, 0)),
            _resident((2 * n_f, d, FFN_TF)),
            _resident((2 * n_f, 8, FFN_TF)),
            _resident((f_dim, d)),
            pl.BlockSpec((1, d), lambda i, j: (0, 0)),
        ],
        out_specs=pl.BlockSpec((1, tm, d), lambda i, j: (i, j, 0)),
        out_shape=jax.ShapeDtypeStruct((b, s, d), F32),
        scratch_shapes=[
            pltpu.VMEM((tm + halo, d), BF16),
            pltpu.VMEM((2, tm + halo, FFN_TF), F32),
            pltpu.VMEM((2, tm + halo, FFN_TF), F32),
            pltpu.VMEM((tm, f_dim), BF16),
        ],
        compiler_params=_params(("parallel", "parallel"), vmem),
        name=name,
    )(x3d, x3d, gpre, wup_t, cw_t, wdn, gpost)


def _kvq_kernel(x_ref, gkv_ref, gq_ref, wkv_ref, wq_ref, kv_ref, q_ref):
    x = x_ref[...]
    xr = x * _rms_scale(x)
    kv = jnp.dot((xr * gkv_ref[...]).astype(BF16), wkv_ref[...], preferred_element_type=F32)
    kv_ref[...] = kv.astype(BF16)
    q = jnp.dot((xr * gq_ref[...]).astype(BF16), wq_ref[...], preferred_element_type=F32)
    q_ref[...] = (q * (DIFF_HEAD_DIM ** -0.5 * LOG2_E)).astype(BF16)


def _kvq(x2d, g_kv, g_q, w_kv, w_q):
    t, d = x2d.shape
    n_kv, n_q = w_kv.shape[1], w_q.shape[1]
    tm = KVQ_TM
    vmem = (2 * tm * d * 4 + d * (n_kv + n_q) * 2 + 2 * tm * (n_kv + n_q) * 2
            + 6 * tm * d * 4 + 2 * tm * (n_kv + n_q) * 4 + 4 * MIB)
    return pl.pallas_call(
        _kvq_kernel,
        grid=(t // tm,),
        in_specs=[
            pl.BlockSpec((tm, d), lambda i: (i, 0)),
            pl.BlockSpec((1, d), lambda i: (0, 0)),
            pl.BlockSpec((1, d), lambda i: (0, 0)),
            _resident((d, n_kv)),
            _resident((d, n_q)),
        ],
        out_specs=[pl.BlockSpec((tm, n_kv), lambda i: (i, 0)),
                   pl.BlockSpec((tm, n_q), lambda i: (i, 0))],
        out_shape=[jax.ShapeDtypeStruct((t, n_kv), BF16),
                   jax.ShapeDtypeStruct((t, n_q), BF16)],
        compiler_params=_params(("parallel",), vmem),
        name="kv_q_proj",
    )(x2d, g_kv, g_q, w_kv, w_q)


def _diff_attn_kernel(q_ref, k_ref, v_ref, lam_ref, gsub_ref, o_ref,
                      q2t_ref, vt_ref, s0_ref, s1_ref, p0_ref, p1_ref, a0_ref, a1_ref,
                      m_ref, acc_ref, *, lambda_init):
    tq, tk, d, dv = ATT_TQ, ATT_TK, DIFF_HEAD_DIM, DIFF_V_DIM
    strip = LANES
    n_kv = vt_ref.shape[0]

    for j in range(n_kv):
        vt_ref[j, :dv, :] = v_ref[0, j * tk:(j + 1) * tk, :].astype(F32).T.astype(BF16)
    vt_ref[:, dv:, :] = jnp.ones((n_kv, BF16_ROWS, tk), BF16)

    n_pieces = 2 * tq // MXU_COLS
    strips_per_piece = MXU_COLS // strip

    def piece(r):
        return slice(r * MXU_COLS, (r + 1) * MXU_COLS)

    def scores(j, s_ref, r):
        krows = pl.ds(pl.multiple_of(j * tk, tk), tk)
        res = jnp.dot(k_ref[0, krows, :], q2t_ref[:, piece(r)],
                      preferred_element_type=F32)
        for c in range(strips_per_piece):
            s_ref[r * strips_per_piece + c] = res[:, c * strip:(c + 1) * strip]

    def softmax(s_ref, p_ref, a_ref, diag_tile, r):
        for c in range(r * strips_per_piece, (r + 1) * strips_per_piece):
            lanes = slice(c * strip, (c + 1) * strip)
            q_off = (c * strip) % tq
            k_off = 0 if diag_tile is None else diag_tile * tk
            if diag_tile is not None and k_off > q_off + strip - 1:
                p_ref[c] = jnp.zeros((tk, strip), BF16)
                a_ref[:, lanes] = jnp.ones((1, strip), F32)
                continue
            s = s_ref[c]
            if diag_tile is not None and k_off + tk - 1 > q_off:
                rel = (lax.broadcasted_iota(jnp.int32, (tk, strip), 1)
                       - lax.broadcasted_iota(jnp.int32, (tk, strip), 0))
                s = jnp.where(rel >= k_off - q_off, s, MASK_VALUE)
            m_old = m_ref[:, lanes]
            m_new = jnp.maximum(m_old, jnp.max(s, axis=0, keepdims=True))
            m_ref[:, lanes] = m_new
            a_ref[:, lanes] = jnp.exp2(m_old - m_new)
            p_ref[c] = jnp.exp2(s - m_new).astype(BF16)

    def accumulate(j, p_ref, a_ref, r):
        p = jnp.concatenate([p_ref[r * strips_per_piece + c] for c in range(strips_per_piece)],
                            axis=1)
        pv = jnp.dot(vt_ref[j], p, preferred_element_type=F32)
        acc_ref[:, piece(r)] = a_ref[:, piece(r)] * acc_ref[:, piece(r)] + pv

    def tile_pair(u, carry, *, diagonal):
        j0 = 2 * u
        j_prev = jnp.maximum(j0 - 1, 0)
        for r in range(n_pieces):
            scores(j0 + 1, s1_ref, r)
            softmax(s0_ref, p0_ref, a0_ref, 0 if diagonal else None, r)
            accumulate(j_prev, p1_ref, a1_ref, r)
        for r in range(n_pieces):
            if not diagonal:
                scores(j0 + 2, s0_ref, r)
            softmax(s1_ref, p1_ref, a1_ref, 1 if diagonal else None, r)
            accumulate(j0, p0_ref, a0_ref, r)
        return carry

    lp = lam_ref[...]
    lam = (jnp.exp(jnp.sum(lp[0:1] * lp[1:2], axis=-1, keepdims=True))
           - jnp.exp(jnp.sum(lp[2:3] * lp[3:4], axis=-1, keepdims=True)) + lambda_init)

    def q_tile(i, carry):
        rows = pl.ds(pl.multiple_of(i * tq, tq), tq)
        qt = q_ref[0, rows, :].astype(F32).T
        feat = lax.broadcasted_iota(jnp.int32, qt.shape, 0)
        q2t_ref[:, :tq] = jnp.where(feat < d, qt, 0.0).astype(BF16)
        q2t_ref[:, tq:] = jnp.where(feat >= d, qt, 0.0).astype(BF16)
        m_ref[...] = jnp.full_like(m_ref, MASK_VALUE)
        acc_ref[...] = jnp.zeros_like(acc_ref)
        p1_ref[...] = jnp.zeros_like(p1_ref)
        a1_ref[...] = jnp.ones_like(a1_ref)

        for r in range(n_pieces):
            scores(0, s0_ref, r)
        lax.fori_loop(0, i, functools.partial(tile_pair, diagonal=False), 0)
        tile_pair(i, 0, diagonal=True)
        for r in range(n_pieces):
            accumulate(2 * i + 1, p1_ref, a1_ref, r)

        inv_l = 1.0 / acc_ref[dv:dv + 1, :]
        o = (acc_ref[:dv, :tq] * inv_l[:, :tq]
             - lam * (acc_ref[:dv, tq:] * inv_l[:, tq:]))
        r = lax.rsqrt(jnp.mean(o * o, axis=0, keepdims=True) + EPS)
        o = o * r * gsub_ref[...] * (1.0 - lambda_init)
        o_ref[0, rows, :] = o.T.astype(BF16)
        return carry

    lax.fori_loop(0, q_ref.shape[1] // tq, q_tile, 0)


def _diff_attention(q3d, kv3d, lam_params, g_sub_col, lambda_init):
    b, s, _ = q3d.shape
    h, dv = DIFF_HEADS, DIFF_V_DIM
    tq, tk = ATT_TQ, ATT_TK
    assert tq == 2 * tk and s % tq == 0
    dv_aug = dv + BF16_ROWS
    strips = (2 * tq // LANES, tk, LANES)
    vmem = (2 * 4 * s * dv * 2 + 2 * tq * dv * 2 + s * dv_aug * 2 + 2 * tk * 2 * tq * (4 + 2)
            + dv_aug * 2 * tq * 4 + 4 * tk * 2 * tq * 4 + 8 * MIB)

    def head_block(first):
        return pl.BlockSpec((1, s, dv), lambda bi, hi: (bi, 0, first + hi))

    return pl.pallas_call(
        functools.partial(_diff_attn_kernel, lambda_init=lambda_init),
        grid=(b, h),
        in_specs=[
            head_block(0),
            head_block(0),
            head_block(h),
            pl.BlockSpec(lam_params.shape, lambda bi, hi: (0, 0)),
            pl.BlockSpec((dv, 1), lambda bi, hi: (0, 0)),
        ],
        out_specs=head_block(0),
        out_shape=jax.ShapeDtypeStruct((b, s, h * dv), BF16),
        scratch_shapes=[
            pltpu.VMEM((dv, 2 * tq), BF16),
            pltpu.VMEM((s // tk, dv_aug, tk), BF16),
            pltpu.VMEM(strips, F32),
            pltpu.VMEM(strips, F32),
            pltpu.VMEM(strips, BF16),
            pltpu.VMEM(strips, BF16),
            pltpu.VMEM((1, 2 * tq), F32),
            pltpu.VMEM((1, 2 * tq), F32),
            pltpu.VMEM((1, 2 * tq), F32),
            pltpu.VMEM((dv_aug, 2 * tq), F32),
        ],
        compiler_params=_params(("parallel", "parallel"), vmem),
        name="diff_attention",
    )(q3d, kv3d, kv3d, lam_params, g_sub_col)


def kernel(x, a_norm_pre, a_norm_post, a_w_in, a_w_out, kv_norm, w_kv, b_norm_pre, b_norm_post,
           b_w_q, b_lambda, b_subln, b_w_out, ffn_norm_pre, ffn_norm_post, ffn_w_up, ffn_conv_w,
           ffn_conv_b, ffn_w_down):
    b, s, d = x.shape
    t = b * s

    half = RET_QK_DIM // 2
    inv = 1.0 / (ROPE_BASE ** jnp.linspace(0.0, 1.0, half, dtype=F32))
    ang = jnp.arange(s).astype(F32)[:, None] * inv[None, :]
    cos, sin = jnp.cos(ang), jnp.sin(ang)
    log_gamma = jnp.log1p(-jnp.power(2.0, -5.0 - jnp.arange(RET_HEADS, dtype=F32)))

    proj = _proj_in(x.reshape(t, d), a_norm_pre[0][None], a_w_in[0].astype(BF16), cos, sin, s)
    y = _retention(proj.reshape(b, s, -1), log_gamma)
    x = _post(y.reshape(t, -1), a_w_out[0].astype(BF16), a_norm_post[0][None],
              x.reshape(t, d), "ret_out_proj")
    x = _conv_ffn(x.reshape(b, s, d), ffn_norm_pre[0][None], ffn_w_up[0], ffn_conv_w[0],
                  ffn_conv_b[0], ffn_w_down[0], ffn_norm_post[0][None], "conv_ffn_0")

    layer = 1
    lambda_init = 0.8 - 0.6 * math.exp(-0.3 * layer)
    kv, q = _kvq(x.reshape(t, d), kv_norm[None], b_norm_pre[0][None],
                 w_kv.astype(BF16), b_w_q[0].astype(BF16))
    o = _diff_attention(q.reshape(b, s, -1), kv.reshape(b, s, -1), b_lambda[0],
                        b_subln[0][:, None], lambda_init)
    x = _post(o.reshape(t, -1), b_w_out[0].astype(BF16), b_norm_post[0][None],
              x.reshape(t, d), "attn_out_proj")
    x = _conv_ffn(x.reshape(b, s, d), ffn_norm_pre[1][None], ffn_w_up[1], ffn_conv_w[1],
                  ffn_conv_b[1], ffn_w_down[1], ffn_norm_post[1][None], "conv_ffn_1")
    return x
```

```python
import functools
import math

import jax
import jax.numpy as jnp
from jax import lax
from jax.experimental import pallas as pl
from jax.experimental.pallas import tpu as pltpu

F32 = jnp.float32
BF16 = jnp.bfloat16

D_MODEL = 1024
RET_HEADS = 4
RET_QK_DIM = D_MODEL // RET_HEADS
RET_V_DIM = 2 * RET_QK_DIM
RET_CHUNK = 256
DIFF_HEAD_DIM = 64
DIFF_HEADS = D_MODEL // (2 * DIFF_HEAD_DIM)
DIFF_V_DIM = 2 * DIFF_HEAD_DIM
D_FF = ((8 * D_MODEL // 3 + 127) // 128) * 128
ROPE_BASE = 10000.0
EPS = 1e-6
MASK_VALUE = -1e30
LOG2_E = math.log2(math.e)

BF16_ROWS = 16
LANES = 128
MXU_COLS = 256
MIB = 1024 * 1024

PROJ_TM = 1024
PROJ_TN = 1024
POST_TM = 512
KVQ_TM = 512
FFN_TM = 512
FFN_TF = 256
FFN_ROWS = 64
ATT_TQ = 512
ATT_TK = 256


def _params(semantics, vmem_bytes):
    return pltpu.CompilerParams(dimension_semantics=semantics,
                                vmem_limit_bytes=int(vmem_bytes))


def _resident(shape):
    zeros = (0,) * len(shape)
    return pl.BlockSpec(shape, lambda *_: zeros, pipeline_mode=pl.Buffered(1))


def _rms_scale(v):
    return lax.rsqrt(jnp.mean(v * v, axis=-1, keepdims=True) + EPS)


def _proj_in_kernel(x_ref, g_ref, w_ref, cos_ref, sin_ref, o_ref, xn_ref):
    j = pl.program_id(1)

    @pl.when(j == 0)
    def _():
        x = x_ref[...]
        xn_ref[...] = (x * _rms_scale(x) * g_ref[...]).astype(BF16)

    def acc():
        return jnp.dot(xn_ref[...], w_ref[...], preferred_element_type=F32)

    def rotary(scale):
        a = acc()
        cos = cos_ref[...]
        sin = sin_ref[...]
        half = RET_QK_DIM // 2
        for h in range(PROJ_TN // RET_QK_DIM):
            lo = h * RET_QK_DIM
            x1 = a[:, lo:lo + half]
            x2 = a[:, lo + half:lo + 2 * half]
            o_ref[:, lo:lo + half] = ((x1 * cos - x2 * sin) * scale).astype(BF16)
            o_ref[:, lo + half:lo + 2 * half] = ((x1 * sin + x2 * cos) * scale).astype(BF16)

    n_qk = RET_HEADS * RET_QK_DIM // PROJ_TN
    n_v = RET_HEADS * RET_V_DIM // PROJ_TN

    @pl.when(j < n_qk)
    def _():
        rotary(1.0)

    @pl.when((j >= n_qk) & (j < 2 * n_qk))
    def _():
        rotary(RET_QK_DIM ** -0.5)

    @pl.when((j >= 2 * n_qk) & (j < 2 * n_qk + n_v))
    def _():
        o_ref[...] = acc().astype(BF16)

    @pl.when(j >= 2 * n_qk + n_v)
    def _():
        a = acc()
        o_ref[...] = (a / (1.0 + jnp.exp(-a))).astype(BF16)


def _proj_in(x2d, gain, w, cos, sin, seq):
    t, d = x2d.shape
    n = w.shape[1]
    tiles_per_seq = seq // PROJ_TM
    vmem = (2 * PROJ_TM * d * 4 + 2 * d * PROJ_TN * 2 + 2 * PROJ_TM * PROJ_TN * 2
            + PROJ_TM * d * 2 + 4 * PROJ_TM * PROJ_TN * 4 + 4 * MIB)
    return pl.pallas_call(
        _proj_in_kernel,
        grid=(t // PROJ_TM, n // PROJ_TN),
        in_specs=[
            pl.BlockSpec((PROJ_TM, d), lambda i, j: (i, 0)),
            pl.BlockSpec((1, d), lambda i, j: (0, 0)),
            pl.BlockSpec((d, PROJ_TN), lambda i, j: (0, j)),
            pl.BlockSpec((PROJ_TM, RET_QK_DIM // 2), lambda i, j: (i % tiles_per_seq, 0)),
            pl.BlockSpec((PROJ_TM, RET_QK_DIM // 2), lambda i, j: (i % tiles_per_seq, 0)),
        ],
        out_specs=pl.BlockSpec((PROJ_TM, PROJ_TN), lambda i, j: (i, j)),
        out_shape=jax.ShapeDtypeStruct((t, n), BF16),
        scratch_shapes=[pltpu.VMEM((PROJ_TM, d), BF16)],
        compiler_params=_params(("parallel", "arbitrary"), vmem),
        name="ret_proj_in",
    )(x2d, gain, w, cos, sin)


def _retention_kernel(lg_ref, q_ref, k_ref, v_ref, sg_ref, o_ref, state_ref):
    c_len = RET_CHUNK
    lg = lg_ref[pl.program_id(1)]
    row = lax.broadcasted_iota(jnp.int32, (c_len, c_len), 0)
    col = lax.broadcasted_iota(jnp.int32, (c_len, c_len), 1)
    rel = (row - col).astype(F32)
    decay_mask = jnp.where(rel >= 0, jnp.exp(lg * jnp.maximum(rel, 0.0)), 0.0)
    idx = lax.broadcasted_iota(jnp.int32, (c_len, 1), 0).astype(F32)
    q_decay = jnp.exp(lg * (idx + 1.0))
    k_decay = jnp.exp(lg * (c_len - 1.0 - idx))
    chunk_decay = jnp.exp(lg * jnp.full((1, 1), c_len, F32))

    state_ref[...] = jnp.zeros_like(state_ref)

    def chunk(c, carry):
        rows = pl.ds(pl.multiple_of(c * c_len, c_len), c_len)
        q = q_ref[0, rows, :]
        k = k_ref[0, rows, :]
        v = v_ref[0, rows, :]
        s = lax.dot_general(q, k, (((1,), (1,)), ((), ())), preferred_element_type=F32)
        state = state_ref[...]
        cross = jnp.dot(q, state.astype(BF16), preferred_element_type=F32)
        kd = (k.astype(F32) * k_decay).astype(BF16)
        update = lax.dot_general(kd, v, (((0,), (0,)), ((), ())), preferred_element_type=F32)
        inner = jnp.dot((s * decay_mask).astype(BF16), v, preferred_element_type=F32)
        state_ref[...] = state * chunk_decay + update
        out = inner + cross * q_decay
        gate = sg_ref[0, rows, :].astype(F32)
        o_ref[0, rows, :] = (out * _rms_scale(out) * gate).astype(BF16)
        return carry

    lax.fori_loop(0, q_ref.shape[1] // c_len, chunk, 0)


def _retention(proj3d, log_gamma):
    b, s, _ = proj3d.shape
    dk, dv, h = RET_QK_DIM, RET_V_DIM, RET_HEADS
    k_blk0 = h * dk // dk
    v_blk0 = 2 * h * dk // dv
    g_blk0 = (2 * h * dk + h * dv) // dv
    vmem = 2 * s * (2 * dk + 3 * dv) * 2 + dk * dv * 4 + 8 * MIB
    return pl.pallas_call(
        _retention_kernel,
        grid=(b, h),
        in_specs=[
            pl.BlockSpec(memory_space=pltpu.SMEM),
            pl.BlockSpec((1, s, dk), lambda i, j: (i, 0, j)),
            pl.BlockSpec((1, s, dk), lambda i, j: (i, 0, k_blk0 + j)),
            pl.BlockSpec((1, s, dv), lambda i, j: (i, 0, v_blk0 + j)),
            pl.BlockSpec((1, s, dv), lambda i, j: (i, 0, g_blk0 + j)),
        ],
        out_specs=pl.BlockSpec((1, s, dv), lambda i, j: (i, 0, j)),
        out_shape=jax.ShapeDtypeStruct((b, s, h * dv), BF16),
        scratch_shapes=[pltpu.VMEM((dk, dv), F32)],
        compiler_params=_params(("parallel", "parallel"), vmem),
        name="retention_core",
    )(log_gamma, proj3d, proj3d, proj3d, proj3d)


def _post_kernel(y_ref, w_ref, g_ref, x_ref, o_ref):
    h = jnp.dot(y_ref[...], w_ref[...], preferred_element_type=F32)
    o_ref[...] = x_ref[...] + h * _rms_scale(h) * g_ref[...]


def _post(y2d, w, gain, x2d, name):
    t, kdim = y2d.shape
    d = w.shape[1]
    vmem = (2 * POST_TM * kdim * 2 + kdim * d * 2 + 4 * POST_TM * d * 4
            + 2 * POST_TM * d * 4 + 4 * MIB)
    return pl.pallas_call(
        _post_kernel,
        grid=(t // POST_TM,),
        in_specs=[
            pl.BlockSpec((POST_TM, kdim), lambda i: (i, 0)),
            _resident((kdim, d)),
            pl.BlockSpec((1, d), lambda i: (0, 0)),
            pl.BlockSpec((POST_TM, d), lambda i: (i, 0)),
        ],
        out_specs=pl.BlockSpec((POST_TM, d), lambda i: (i, 0)),
        out_shape=jax.ShapeDtypeStruct((t, d), F32),
        compiler_params=_params(("parallel",), vmem),
        name=name,
    )(y2d, w, gain, x2d)


def _ffn_kernel(x_ref, xh_ref, gpre_ref, wup_ref, cw_ref, wdn_ref, gpost_ref, o_ref,
                xn_ref, h0_ref, h1_ref, act_ref):
    n_f = wup_ref.shape[0] // 2
    tm, tf = act_ref.shape[0], wup_ref.shape[2]
    halo = BF16_ROWS
    x = x_ref[0]
    gpre = gpre_ref[...]
    xn_ref[halo:, :] = (x * _rms_scale(x) * gpre).astype(BF16)
    xh = xh_ref[0]
    xhn = jnp.where(pl.program_id(1) > 0, xh * _rms_scale(xh) * gpre, 0.0)
    xn_ref[:halo, :] = xhn.astype(BF16)

    def up_project(f, h_ref):
        xn = xn_ref[...]
        h_ref[0] = jnp.dot(xn, wup_ref[f], preferred_element_type=F32)
        h_ref[1] = jnp.dot(xn, wup_ref[n_f + f], preferred_element_type=F32)

    def conv(h_ref, half, idx, row0):
        cw = cw_ref[idx]
        rows = lambda back: pl.ds(halo + row0 - back, FFN_ROWS)
        return (cw[3:4] + cw[2:3] * h_ref[half, rows(0), :]
                + cw[1:2] * h_ref[half, rows(1), :] + cw[0:1] * h_ref[half, rows(2), :])

    def gate(f, h_ref):
        for row0 in range(0, tm, FFN_ROWS):
            g = conv(h_ref, 0, f, row0)
            u = conv(h_ref, 1, n_f + f, row0)
            act_ref[row0:row0 + FFN_ROWS, f * tf:(f + 1) * tf] = (
                g / (1.0 + jnp.exp(-g)) * u).astype(BF16)

    h_refs = (h0_ref, h1_ref)
    up_project(0, h_refs[0])
    for f in range(n_f):
        if f + 1 < n_f:
            up_project(f + 1, h_refs[(f + 1) % 2])
        gate(f, h_refs[f % 2])
    h = jnp.dot(act_ref[...], wdn_ref[...], preferred_element_type=F32)
    o_ref[0] = x + h * _rms_scale(h) * gpost_ref[...]


def _conv_ffn(x3d, gpre, w_up, conv_w, conv_b, w_down, gpost, name):
    b, s, d = x3d.shape
    f_dim = w_down.shape[0]
    n_f = f_dim // FFN_TF
    tm, halo = FFN_TM, BF16_ROWS
    wup_t = w_up.astype(BF16).reshape(d, 2 * n_f, FFN_TF).transpose(1, 0, 2)
    wdn = w_down.astype(BF16)
    cw = jnp.concatenate([conv_w, conv_b[None, :], jnp.zeros((4, 2 * f_dim), F32)], axis=0)
    cw_t = cw.reshape(8, 2 * n_f, FFN_TF).transpose(1, 0, 2)
    halo_blocks = tm // halo
    vmem = (2 * f_dim * d * 2 + f_dim * d * 2 + 4 * tm * d * 4 + (tm + halo) * d * 2
            + tm * f_dim * 2 + 4 * (tm + halo) * FFN_TF * 4 + 4 * tm * d * 4 + 4 * MIB)
    return pl.pallas_call(
        _ffn_kernel,
        grid=(b, s // tm),
        in_specs=[
            pl.BlockSpec((1, tm, d), lambda i, j: (i, j, 0)),
            pl.BlockSpec((1, halo, d), lambda i, j: (i, jnp.maximum(j * halo_blocks - 1, 0), 0)),
            pl.BlockSpec((1, d), lambda i, j: (0, 0)),
            _resident((2 * n_f, d, FFN_TF)),
            _resident((2 * n_f, 8, FFN_TF)),
            _resident((f_dim, d)),
            pl.BlockSpec((1, d), lambda i, j: (0, 0)),
        ],
        out_specs=pl.BlockSpec((1, tm, d), lambda i, j: (i, j, 0)),
        out_shape=jax.ShapeDtypeStruct((b, s, d), F32),
        scratch_shapes=[
            pltpu.VMEM((tm + halo, d), BF16),
            pltpu.VMEM((2, tm + halo, FFN_TF), F32),
            pltpu.VMEM((2, tm + halo, FFN_TF), F32),
            pltpu.VMEM((tm, f_dim), BF16),
        ],
        compiler_params=_params(("parallel", "parallel"), vmem),
        name=name,
    )(x3d, x3d, gpre, wup_t, cw_t, wdn, gpost)


def _kvq_kernel(x_ref, gkv_ref, gq_ref, wkv_ref, wq_ref, kv_ref, q_ref):
    x = x_ref[...]
    xr = x * _rms_scale(x)
    kv = jnp.dot((xr * gkv_ref[...]).astype(BF16), wkv_ref[...], preferred_element_type=F32)
    kv_ref[...] = kv.astype(BF16)
    q = jnp.dot((xr * gq_ref[...]).astype(BF16), wq_ref[...], preferred_element_type=F32)
    q_ref[...] = (q * (DIFF_HEAD_DIM ** -0.5 * LOG2_E)).astype(BF16)


def _kvq(x2d, g_kv, g_q, w_kv, w_q):
    t, d = x2d.shape
    n_kv, n_q = w_kv.shape[1], w_q.shape[1]
    tm = KVQ_TM
    vmem = (2 * tm * d * 4 + d * (n_kv + n_q) * 2 + 2 * tm * (n_kv + n_q) * 2
            + 6 * tm * d * 4 + 2 * tm * (n_kv + n_q) * 4 + 4 * MIB)
    return pl.pallas_call(
        _kvq_kernel,
        grid=(t // tm,),
        in_specs=[
            pl.BlockSpec((tm, d), lambda i: (i, 0)),
            pl.BlockSpec((1, d), lambda i: (0, 0)),
            pl.BlockSpec((1, d), lambda i: (0, 0)),
            _resident((d, n_kv)),
            _resident((d, n_q)),
        ],
        out_specs=[pl.BlockSpec((tm, n_kv), lambda i: (i, 0)),
                   pl.BlockSpec((tm, n_q), lambda i: (i, 0))],
        out_shape=[jax.ShapeDtypeStruct((t, n_kv), BF16),
                   jax.ShapeDtypeStruct((t, n_q), BF16)],
        compiler_params=_params(("parallel",), vmem),
        name="kv_q_proj",
    )(x2d, g_kv, g_q, w_kv, w_q)


def _diff_attn_kernel(q_ref, k_ref, v_ref, lam_ref, gsub_ref, o_ref,
                      q2t_ref, vt_ref, s_ref, p_ref, a_ref, m_ref, acc_ref, *, lambda_init):
    tq, tk, d, dv = ATT_TQ, ATT_TK, DIFF_HEAD_DIM, DIFF_V_DIM
    strip = LANES
    n_q = q2t_ref.shape[0]
    n_kv = vt_ref.shape[0]
    n_pieces = 2 * tq // MXU_COLS
    strips_per_piece = MXU_COLS // strip

    for j in range(n_kv):
        vt_ref[j, :dv, :] = v_ref[0, j * tk:(j + 1) * tk, :].astype(F32).T.astype(BF16)
    vt_ref[:, dv:, :] = jnp.ones((n_kv, BF16_ROWS, tk), BF16)

    for i in range(n_q):
        qt = q_ref[0, i * tq:(i + 1) * tq, :].astype(F32).T
        feat = lax.broadcasted_iota(jnp.int32, qt.shape, 0)
        q2t_ref[i, :, :tq] = jnp.where(feat < d, qt, 0.0).astype(BF16)
        q2t_ref[i, :, tq:] = jnp.where(feat >= d, qt, 0.0).astype(BF16)

    lp = lam_ref[...]
    lam = (jnp.exp(jnp.sum(lp[0:1] * lp[1:2], axis=-1, keepdims=True))
           - jnp.exp(jnp.sum(lp[2:3] * lp[3:4], axis=-1, keepdims=True)) + lambda_init)

    tiles = [(i, j) for i in range(n_q) for j in range((i + 1) * tq // tk)]

    def piece(r):
        return slice(r * MXU_COLS, (r + 1) * MXU_COLS)

    def visibility(i, j, c):
        q_first = i * tq + (c * strip) % tq
        k_first = j * tk
        if k_first > q_first + strip - 1:
            return "none"
        return "some" if k_first + tk - 1 > q_first else "all"

    def piece_visible(i, j, r):
        seen = [visibility(i, j, c) != "none"
                for c in range(r * strips_per_piece, (r + 1) * strips_per_piece)]
        assert all(seen) or not any(seen)
        return seen[0]

    def scores(t, r):
        i, j = tiles[t]
        if not piece_visible(i, j, r):
            return
        res = jnp.dot(k_ref[0, j * tk:(j + 1) * tk, :], q2t_ref[i, :, piece(r)],
                      preferred_element_type=F32)
        for c in range(strips_per_piece):
            s_ref[t % 2, r * strips_per_piece + c] = res[:, c * strip:(c + 1) * strip]

    def softmax(t, r):
        i, j = tiles[t]
        if not piece_visible(i, j, r):
            return
        for c in range(r * strips_per_piece, (r + 1) * strips_per_piece):
            lanes = slice(c * strip, (c + 1) * strip)
            s = s_ref[t % 2, c]
            if visibility(i, j, c) == "some":
                rel = (lax.broadcasted_iota(jnp.int32, (tk, strip), 1)
                       - lax.broadcasted_iota(jnp.int32, (tk, strip), 0))
                s = jnp.where(rel >= j * tk - i * tq - (c * strip) % tq, s, MASK_VALUE)
            m_new = jnp.max(s, axis=0, keepdims=True)
            if j > 0:
                m_old = m_ref[i, :, lanes]
                m_new = jnp.maximum(m_old, m_new)
                a_ref[t % 2, :, lanes] = jnp.exp2(m_old - m_new)
            m_ref[i, :, lanes] = m_new
            p_ref[t % 2, c] = jnp.exp2(s - m_new).astype(BF16)

    def accumulate(t, r):
        i, j = tiles[t]
        if not piece_visible(i, j, r):
            return
        p = jnp.concatenate([p_ref[t % 2, r * strips_per_piece + c]
                             for c in range(strips_per_piece)], axis=1)
        pv = jnp.dot(vt_ref[j], p, preferred_element_type=F32)
        if j > 0:
            pv = a_ref[t % 2, :, piece(r)] * acc_ref[i, :, piece(r)] + pv
        acc_ref[i, :, piece(r)] = pv

    def finalize(i):
        inv_l = 1.0 / acc_ref[i, dv:dv + 1, :]
        o = (acc_ref[i, :dv, :tq] * inv_l[:, :tq]
             - lam * (acc_ref[i, :dv, tq:] * inv_l[:, tq:]))
        r = lax.rsqrt(jnp.mean(o * o, axis=0, keepdims=True) + EPS)
        o = o * r * gsub_ref[...] * (1.0 - lambda_init)
        o_ref[0, i * tq:(i + 1) * tq, :] = o.T.astype(BF16)

    for r in range(n_pieces):
        scores(0, r)
    for t in range(len(tiles)):
        for r in range(n_pieces):
            if t + 1 < len(tiles):
                scores(t + 1, r)
            softmax(t, r)
            if t > 0:
                accumulate(t - 1, r)
        if t > 0 and tiles[t][0] != tiles[t - 1][0]:
            finalize(tiles[t - 1][0])
    for r in range(n_pieces):
        accumulate(len(tiles) - 1, r)
    finalize(n_q - 1)


def _diff_attention(q3d, kv3d, lam_params, g_sub_col, lambda_init):
    b, s, _ = q3d.shape
    h, dv = DIFF_HEADS, DIFF_V_DIM
    tq, tk = ATT_TQ, ATT_TK
    assert tq % tk == 0 and s % tq == 0
    dv_aug = dv + BF16_ROWS
    n_q = s // tq
    strips = (2, 2 * tq // LANES, tk, LANES)
    vmem = (2 * 4 * s * dv * 2 + 2 * s * dv * 2 + s * dv_aug * 2 + 2 * tk * 2 * tq * (4 + 2)
            + n_q * dv_aug * 2 * tq * 4 + 4 * tk * 2 * tq * 4 + 8 * MIB)

    def head_block(first):
        return pl.BlockSpec((1, s, dv), lambda bi, hi: (bi, 0, first + hi))

    return pl.pallas_call(
        functools.partial(_diff_attn_kernel, lambda_init=lambda_init),
        grid=(b, h),
        in_specs=[
            head_block(0),
            head_block(0),
            head_block(h),
            pl.BlockSpec(lam_params.shape, lambda bi, hi: (0, 0)),
            pl.BlockSpec((dv, 1), lambda bi, hi: (0, 0)),
        ],
        out_specs=head_block(0),
        out_shape=jax.ShapeDtypeStruct((b, s, h * dv), BF16),
        scratch_shapes=[
            pltpu.VMEM((n_q, dv, 2 * tq), BF16),
            pltpu.VMEM((s // tk, dv_aug, tk), BF16),
            pltpu.VMEM(strips, F32),
            pltpu.VMEM(strips, BF16),
            pltpu.VMEM((2, 1, 2 * tq), F32),
            pltpu.VMEM((n_q, 1, 2 * tq), F32),
            pltpu.VMEM((n_q, dv_aug, 2 * tq), F32),
        ],
        compiler_params=_params(("parallel", "parallel"), vmem),
        name="diff_attention",
    )(q3d, kv3d, kv3d, lam_params, g_sub_col)


def kernel(x, a_norm_pre, a_norm_post, a_w_in, a_w_out, kv_norm, w_kv, b_norm_pre, b_norm_post,
           b_w_q, b_lambda, b_subln, b_w_out, ffn_norm_pre, ffn_norm_post, ffn_w_up, ffn_conv_w,
           ffn_conv_b, ffn_w_down):
    b, s, d = x.shape
    t = b * s

    half = RET_QK_DIM // 2
    inv = 1.0 / (ROPE_BASE ** jnp.linspace(0.0, 1.0, half, dtype=F32))
    ang = jnp.arange(s).astype(F32)[:, None] * inv[None, :]
    cos, sin = jnp.cos(ang), jnp.sin(ang)
    log_gamma = jnp.log1p(-jnp.power(2.0, -5.0 - jnp.arange(RET_HEADS, dtype=F32)))

    proj = _proj_in(x.reshape(t, d), a_norm_pre[0][None], a_w_in[0].astype(BF16), cos, sin, s)
    y = _retention(proj.reshape(b, s, -1), log_gamma)
    x = _post(y.reshape(t, -1), a_w_out[0].astype(BF16), a_norm_post[0][None],
              x.reshape(t, d), "ret_out_proj")
    x = _conv_ffn(x.reshape(b, s, d), ffn_norm_pre[0][None], ffn_w_up[0], ffn_conv_w[0],
                  ffn_conv_b[0], ffn_w_down[0], ffn_norm_post[0][None], "conv_ffn_0")

    layer = 1
    lambda_init = 0.8 - 0.6 * math.exp(-0.3 * layer)
    kv, q = _kvq(x.reshape(t, d), kv_norm[None], b_norm_pre[0][None],
                 w_kv.astype(BF16), b_w_q[0].astype(BF16))
    o = _diff_attention(q.reshape(b, s, -1), kv.reshape(b, s, -1), b_lambda[0],
                        b_subln[0][:, None], lambda_init)
    x = _post(o.reshape(t, -1), b_w_out[0].astype(BF16), b_norm_post[0][None],
              x.reshape(t, d), "attn_out_proj")
    x = _conv_ffn(x.reshape(b, s, d), ffn_norm_pre[1][None], ffn_w_up[1], ffn_conv_w[1],
                  ffn_conv_b[1], ffn_w_down[1], ffn_norm_post[1][None], "conv_ffn_1")
    return x
```

```python
import functools
import math

import jax
import jax.numpy as jnp
from jax import lax
from jax.experimental import pallas as pl
from jax.experimental.pallas import tpu as pltpu

F32 = jnp.float32
BF16 = jnp.bfloat16

D_MODEL = 1024
RET_HEADS = 4
RET_QK_DIM = D_MODEL // RET_HEADS
RET_V_DIM = 2 * RET_QK_DIM
RET_CHUNK = 256
DIFF_HEAD_DIM = 64
DIFF_HEADS = D_MODEL // (2 * DIFF_HEAD_DIM)
DIFF_V_DIM = 2 * DIFF_HEAD_DIM
D_FF = ((8 * D_MODEL // 3 + 127) // 128) * 128
ROPE_BASE = 10000.0
EPS = 1e-6
MASK_VALUE = -1e30
LOG2_E = math.log2(math.e)

BF16_ROWS = 16
LANES = 128
MXU_COLS = 256
MIB = 1024 * 1024

PROJ_TM = 1024
PROJ_TN = 1024
POST_TM = 512
KVQ_TM = 512
FFN_TM = 512
FFN_TF = 256
FFN_ROWS = 64
ATT_TQ = 512
ATT_TK = 256


def _params(semantics, vmem_bytes):
    return pltpu.CompilerParams(dimension_semantics=semantics,
                                vmem_limit_bytes=int(vmem_bytes))


def _resident(shape):
    zeros = (0,) * len(shape)
    return pl.BlockSpec(shape, lambda *_: zeros, pipeline_mode=pl.Buffered(1))


def _rms_scale(v):
    return lax.rsqrt(jnp.mean(v * v, axis=-1, keepdims=True) + EPS)


def _proj_in_kernel(x_ref, g_ref, w_ref, cos_ref, sin_ref, o_ref, xn_ref):
    j = pl.program_id(1)

    @pl.when(j == 0)
    def _():
        x = x_ref[...]
        xn_ref[...] = (x * _rms_scale(x) * g_ref[...]).astype(BF16)

    def acc():
        return jnp.dot(xn_ref[...], w_ref[...], preferred_element_type=F32)

    def rotary(scale):
        a = acc()
        cos = cos_ref[...]
        sin = sin_ref[...]
        half = RET_QK_DIM // 2
        for h in range(PROJ_TN // RET_QK_DIM):
            lo = h * RET_QK_DIM
            x1 = a[:, lo:lo + half]
            x2 = a[:, lo + half:lo + 2 * half]
            o_ref[:, lo:lo + half] = ((x1 * cos - x2 * sin) * scale).astype(BF16)
            o_ref[:, lo + half:lo + 2 * half] = ((x1 * sin + x2 * cos) * scale).astype(BF16)

    n_qk = RET_HEADS * RET_QK_DIM // PROJ_TN
    n_v = RET_HEADS * RET_V_DIM // PROJ_TN

    @pl.when(j < n_qk)
    def _():
        rotary(1.0)

    @pl.when((j >= n_qk) & (j < 2 * n_qk))
    def _():
        rotary(RET_QK_DIM ** -0.5)

    @pl.when((j >= 2 * n_qk) & (j < 2 * n_qk + n_v))
    def _():
        o_ref[...] = acc().astype(BF16)

    @pl.when(j >= 2 * n_qk + n_v)
    def _():
        a = acc()
        o_ref[...] = (a / (1.0 + jnp.exp(-a))).astype(BF16)


def _proj_in(x2d, gain, w, cos, sin, seq):
    t, d = x2d.shape
    n = w.shape[1]
    tiles_per_seq = seq // PROJ_TM
    vmem = (2 * PROJ_TM * d * 4 + 2 * d * PROJ_TN * 2 + 2 * PROJ_TM * PROJ_TN * 2
            + PROJ_TM * d * 2 + 4 * PROJ_TM * PROJ_TN * 4 + 4 * MIB)
    return pl.pallas_call(
        _proj_in_kernel,
        grid=(t // PROJ_TM, n // PROJ_TN),
        in_specs=[
            pl.BlockSpec((PROJ_TM, d), lambda i, j: (i, 0)),
            pl.BlockSpec((1, d), lambda i, j: (0, 0)),
            pl.BlockSpec((d, PROJ_TN), lambda i, j: (0, j)),
            pl.BlockSpec((PROJ_TM, RET_QK_DIM // 2), lambda i, j: (i % tiles_per_seq, 0)),
            pl.BlockSpec((PROJ_TM, RET_QK_DIM // 2), lambda i, j: (i % tiles_per_seq, 0)),
        ],
        out_specs=pl.BlockSpec((PROJ_TM, PROJ_TN), lambda i, j: (i, j)),
        out_shape=jax.ShapeDtypeStruct((t, n), BF16),
        scratch_shapes=[pltpu.VMEM((PROJ_TM, d), BF16)],
        compiler_params=_params(("parallel", "arbitrary"), vmem),
        name="ret_proj_in",
    )(x2d, gain, w, cos, sin)


def _retention_kernel(lg_ref, q_ref, k_ref, v_ref, sg_ref, o_ref, state_ref):
    c_len = RET_CHUNK
    lg = lg_ref[pl.program_id(1)]
    row = lax.broadcasted_iota(jnp.int32, (c_len, c_len), 0)
    col = lax.broadcasted_iota(jnp.int32, (c_len, c_len), 1)
    rel = (row - col).astype(F32)
    decay_mask = jnp.where(rel >= 0, jnp.exp(lg * jnp.maximum(rel, 0.0)), 0.0)
    idx = lax.broadcasted_iota(jnp.int32, (c_len, 1), 0).astype(F32)
    q_decay = jnp.exp(lg * (idx + 1.0))
    k_decay = jnp.exp(lg * (c_len - 1.0 - idx))
    chunk_decay = jnp.exp(lg * jnp.full((1, 1), c_len, F32))

    n_chunks = q_ref.shape[1] // c_len

    def rows(c):
        return slice(c * c_len, (c + 1) * c_len)

    def scores(c):
        return lax.dot_general(q_ref[0, rows(c), :], k_ref[0, rows(c), :],
                               (((1,), (1,)), ((), ())), preferred_element_type=F32)

    s_next = scores(0)
    for c in range(n_chunks):
        q = q_ref[0, rows(c), :]
        k = k_ref[0, rows(c), :]
        v = v_ref[0, rows(c), :]
        s_masked = (s_next * decay_mask).astype(BF16)
        out = None
        if c > 0:
            out = jnp.dot(q, state_ref[...].astype(BF16),
                          preferred_element_type=F32) * q_decay
        if c + 1 < n_chunks:
            s_next = scores(c + 1)
            kd = (k.astype(F32) * k_decay).astype(BF16)
            update = lax.dot_general(kd, v, (((0,), (0,)), ((), ())),
                                     preferred_element_type=F32)
        inner = jnp.dot(s_masked, v, preferred_element_type=F32)
        if c + 1 < n_chunks:
            state_ref[...] = update if c == 0 else state_ref[...] * chunk_decay + update
        out = inner if out is None else inner + out
        gate = sg_ref[0, rows(c), :].astype(F32)
        o_ref[0, rows(c), :] = (out * _rms_scale(out) * gate).astype(BF16)


def _retention(proj3d, log_gamma):
    b, s, _ = proj3d.shape
    dk, dv, h = RET_QK_DIM, RET_V_DIM, RET_HEADS
    k_blk0 = h * dk // dk
    v_blk0 = 2 * h * dk // dv
    g_blk0 = (2 * h * dk + h * dv) // dv
    vmem = 2 * s * (2 * dk + 3 * dv) * 2 + dk * dv * 4 + 8 * MIB
    return pl.pallas_call(
        _retention_kernel,
        grid=(b, h),
        in_specs=[
            pl.BlockSpec(memory_space=pltpu.SMEM),
            pl.BlockSpec((1, s, dk), lambda i, j: (i, 0, j)),
            pl.BlockSpec((1, s, dk), lambda i, j: (i, 0, k_blk0 + j)),
            pl.BlockSpec((1, s, dv), lambda i, j: (i, 0, v_blk0 + j)),
            pl.BlockSpec((1, s, dv), lambda i, j: (i, 0, g_blk0 + j)),
        ],
        out_specs=pl.BlockSpec((1, s, dv), lambda i, j: (i, 0, j)),
        out_shape=jax.ShapeDtypeStruct((b, s, h * dv), BF16),
        scratch_shapes=[pltpu.VMEM((dk, dv), F32)],
        compiler_params=_params(("parallel", "parallel"), vmem),
        name="retention_core",
    )(log_gamma, proj3d, proj3d, proj3d, proj3d)


def _post_kernel(y_ref, w_ref, g_ref, x_ref, o_ref):
    h = jnp.dot(y_ref[...], w_ref[...], preferred_element_type=F32)
    o_ref[...] = x_ref[...] + h * _rms_scale(h) * g_ref[...]


def _post(y2d, w, gain, x2d, name):
    t, kdim = y2d.shape
    d = w.shape[1]
    vmem = (2 * POST_TM * kdim * 2 + kdim * d * 2 + 4 * POST_TM * d * 4
            + 2 * POST_TM * d * 4 + 4 * MIB)
    return pl.pallas_call(
        _post_kernel,
        grid=(t // POST_TM,),
        in_specs=[
            pl.BlockSpec((POST_TM, kdim), lambda i: (i, 0)),
            _resident((kdim, d)),
            pl.BlockSpec((1, d), lambda i: (0, 0)),
            pl.BlockSpec((POST_TM, d), lambda i: (i, 0)),
        ],
        out_specs=pl.BlockSpec((POST_TM, d), lambda i: (i, 0)),
        out_shape=jax.ShapeDtypeStruct((t, d), F32),
        compiler_params=_params(("parallel",), vmem),
        name=name,
    )(y2d, w, gain, x2d)


def _ffn_kernel(x_ref, xh_ref, gpre_ref, wup_ref, cw_ref, wdn_ref, gpost_ref, o_ref,
                xn_ref, h0_ref, h1_ref, act_ref):
    tm, f_dim = act_ref.shape
    tf = h0_ref.shape[2]
    n_f = f_dim // tf
    halo = BF16_ROWS
    x = x_ref[0]
    gpre = gpre_ref[...]
    xn_ref[halo:, :] = (x * _rms_scale(x) * gpre).astype(BF16)
    xh = xh_ref[0]
    xhn = jnp.where(pl.program_id(1) > 0, xh * _rms_scale(xh) * gpre, 0.0)
    xn_ref[:halo, :] = xhn.astype(BF16)

    def up_project(f, h_ref):
        xn = xn_ref[...]
        for half in range(2):
            col0 = half * f_dim + f * tf
            h_ref[half] = jnp.dot(xn, wup_ref[:, col0:col0 + tf], preferred_element_type=F32)

    def conv(h_ref, half, f, row0):
        col0 = half * f_dim + f * tf
        cw = cw_ref[:, col0:col0 + tf]
        rows = lambda back: pl.ds(halo + row0 - back, FFN_ROWS)
        return (cw[3:4] + cw[2:3] * h_ref[half, rows(0), :]
                + cw[1:2] * h_ref[half, rows(1), :] + cw[0:1] * h_ref[half, rows(2), :])

    def gate(f, h_ref):
        for row0 in range(0, tm, FFN_ROWS):
            g = conv(h_ref, 0, f, row0)
            u = conv(h_ref, 1, f, row0)
            act_ref[row0:row0 + FFN_ROWS, f * tf:(f + 1) * tf] = (
                g / (1.0 + jnp.exp(-g)) * u).astype(BF16)

    h_refs = (h0_ref, h1_ref)
    up_project(0, h_refs[0])
    for f in range(n_f):
        if f + 1 < n_f:
            up_project(f + 1, h_refs[(f + 1) % 2])
        gate(f, h_refs[f % 2])
    h = jnp.dot(act_ref[...], wdn_ref[...], preferred_element_type=F32)
    o_ref[0] = x + h * _rms_scale(h) * gpost_ref[...]


def _conv_ffn(x3d, gpre, w_up, conv_w, conv_b, w_down, gpost, name):
    b, s, d = x3d.shape
    f_dim = w_down.shape[0]
    assert f_dim % FFN_TF == 0
    tm, halo = FFN_TM, BF16_ROWS
    wup = w_up.astype(BF16)
    wdn = w_down.astype(BF16)
    cw = jnp.concatenate([conv_w, conv_b[None, :], jnp.zeros((4, 2 * f_dim), F32)], axis=0)
    halo_blocks = tm // halo
    vmem = (2 * f_dim * d * 2 + f_dim * d * 2 + 4 * tm * d * 4 + (tm + halo) * d * 2
            + tm * f_dim * 2 + 4 * (tm + halo) * FFN_TF * 4 + 4 * tm * d * 4 + 4 * MIB)
    return pl.pallas_call(
        _ffn_kernel,
        grid=(b, s // tm),
        in_specs=[
            pl.BlockSpec((1, tm, d), lambda i, j: (i, j, 0)),
            pl.BlockSpec((1, halo, d), lambda i, j: (i, jnp.maximum(j * halo_blocks - 1, 0), 0)),
            pl.BlockSpec((1, d), lambda i, j: (0, 0)),
            _resident((d, 2 * f_dim)),
            _resident((8, 2 * f_dim)),
            _resident((f_dim, d)),
            pl.BlockSpec((1, d), lambda i, j: (0, 0)),
        ],
        out_specs=pl.BlockSpec((1, tm, d), lambda i, j: (i, j, 0)),
        out_shape=jax.ShapeDtypeStruct((b, s, d), F32),
        scratch_shapes=[
            pltpu.VMEM((tm + halo, d), BF16),
            pltpu.VMEM((2, tm + halo, FFN_TF), F32),
            pltpu.VMEM((2, tm + halo, FFN_TF), F32),
            pltpu.VMEM((tm, f_dim), BF16),
        ],
        compiler_params=_params(("parallel", "parallel"), vmem),
        name=name,
    )(x3d, x3d, gpre, wup, cw, wdn, gpost)


def _kvq_kernel(x_ref, gkv_ref, gq_ref, wkv_ref, wq_ref, kv_ref, q_ref):
    x = x_ref[...]
    xr = x * _rms_scale(x)
    kv = jnp.dot((xr * gkv_ref[...]).astype(BF16), wkv_ref[...], preferred_element_type=F32)
    kv_ref[...] = kv.astype(BF16)
    q = jnp.dot((xr * gq_ref[...]).astype(BF16), wq_ref[...], preferred_element_type=F32)
    q_ref[...] = (q * (DIFF_HEAD_DIM ** -0.5 * LOG2_E)).astype(BF16)


def _kvq(x2d, g_kv, g_q, w_kv, w_q):
    t, d = x2d.shape
    n_kv, n_q = w_kv.shape[1], w_q.shape[1]
    tm = KVQ_TM
    vmem = (2 * tm * d * 4 + d * (n_kv + n_q) * 2 + 2 * tm * (n_kv + n_q) * 2
            + 6 * tm * d * 4 + 2 * tm * (n_kv + n_q) * 4 + 4 * MIB)
    return pl.pallas_call(
        _kvq_kernel,
        grid=(t // tm,),
        in_specs=[
            pl.BlockSpec((tm, d), lambda i: (i, 0)),
            pl.BlockSpec((1, d), lambda i: (0, 0)),
            pl.BlockSpec((1, d), lambda i: (0, 0)),
            _resident((d, n_kv)),
            _resident((d, n_q)),
        ],
        out_specs=[pl.BlockSpec((tm, n_kv), lambda i: (i, 0)),
                   pl.BlockSpec((tm, n_q), lambda i: (i, 0))],
        out_shape=[jax.ShapeDtypeStruct((t, n_kv), BF16),
                   jax.ShapeDtypeStruct((t, n_q), BF16)],
        compiler_params=_params(("parallel",), vmem),
        name="kv_q_proj",
    )(x2d, g_kv, g_q, w_kv, w_q)


def _diff_attn_kernel(q_ref, k_ref, v_ref, lam_ref, gsub_ref, o_ref,
                      q2t_ref, vt_ref, s_ref, p_ref, a_ref, m_ref, acc_ref, *, lambda_init):
    tq, tk, d, dv = ATT_TQ, ATT_TK, DIFF_HEAD_DIM, DIFF_V_DIM
    strip = LANES
    n_q = q2t_ref.shape[0]
    n_kv = vt_ref.shape[0]
    n_pieces = 2 * tq // MXU_COLS
    strips_per_piece = MXU_COLS // strip

    for j in range(n_kv):
        vt_ref[j, :dv, :] = v_ref[0, j * tk:(j + 1) * tk, :].astype(F32).T.astype(BF16)
    vt_ref[:, dv:, :] = jnp.ones((n_kv, BF16_ROWS, tk), BF16)

    for i in range(n_q):
        qt = q_ref[0, i * tq:(i + 1) * tq, :].astype(F32).T
        feat = lax.broadcasted_iota(jnp.int32, qt.shape, 0)
        q2t_ref[i, :, :tq] = jnp.where(feat < d, qt, 0.0).astype(BF16)
        q2t_ref[i, :, tq:] = jnp.where(feat >= d, qt, 0.0).astype(BF16)

    lp = lam_ref[...]
    lam = (jnp.exp(jnp.sum(lp[0:1] * lp[1:2], axis=-1, keepdims=True))
           - jnp.exp(jnp.sum(lp[2:3] * lp[3:4], axis=-1, keepdims=True)) + lambda_init)

    tiles = [(i, j) for i in range(n_q) for j in range((i + 1) * tq // tk)]

    def piece(r):
        return slice(r * MXU_COLS, (r + 1) * MXU_COLS)

    def visibility(i, j, c):
        q_first = i * tq + (c * strip) % tq
        k_first = j * tk
        if k_first > q_first + strip - 1:
            return "none"
        return "some" if k_first + tk - 1 > q_first else "all"

    def piece_visible(i, j, r):
        seen = [visibility(i, j, c) != "none"
                for c in range(r * strips_per_piece, (r + 1) * strips_per_piece)]
        assert all(seen) or not any(seen)
        return seen[0]

    def scores(t, r):
        i, j = tiles[t]
        if not piece_visible(i, j, r):
            return
        res = jnp.dot(k_ref[0, j * tk:(j + 1) * tk, :], q2t_ref[i, :, piece(r)],
                      preferred_element_type=F32)
        for c in range(strips_per_piece):
            s_ref[t % 2, r * strips_per_piece + c] = res[:, c * strip:(c + 1) * strip]

    def softmax(t, r):
        i, j = tiles[t]
        if not piece_visible(i, j, r):
            return
        for c in range(r * strips_per_piece, (r + 1) * strips_per_piece):
            lanes = slice(c * strip, (c + 1) * strip)
            s = s_ref[t % 2, c]
            if visibility(i, j, c) == "some":
                rel = (lax.broadcasted_iota(jnp.int32, (tk, strip), 1)
                       - lax.broadcasted_iota(jnp.int32, (tk, strip), 0))
                s = jnp.where(rel >= j * tk - i * tq - (c * strip) % tq, s, MASK_VALUE)
            m_new = jnp.max(s, axis=0, keepdims=True)
            if j > 0:
                m_old = m_ref[i, :, lanes]
                m_new = jnp.maximum(m_old, m_new)
                a_ref[t % 2, :, lanes] = jnp.exp2(m_old - m_new)
            m_ref[i, :, lanes] = m_new
            p_ref[t % 2, c] = jnp.exp2(s - m_new).astype(BF16)

    def accumulate(t, r):
        i, j = tiles[t]
        if not piece_visible(i, j, r):
            return
        p = jnp.concatenate([p_ref[t % 2, r * strips_per_piece + c]
                             for c in range(strips_per_piece)], axis=1)
        pv = jnp.dot(vt_ref[j], p, preferred_element_type=F32)
        if j > 0:
            pv = a_ref[t % 2, :, piece(r)] * acc_ref[i, :, piece(r)] + pv
        acc_ref[i, :, piece(r)] = pv

    def finalize(i):
        inv_l = 1.0 / acc_ref[i, dv:dv + 1, :]
        o = (acc_ref[i, :dv, :tq] * inv_l[:, :tq]
             - lam * (acc_ref[i, :dv, tq:] * inv_l[:, tq:]))
        r = lax.rsqrt(jnp.mean(o * o, axis=0, keepdims=True) + EPS)
        o = o * r * gsub_ref[...] * (1.0 - lambda_init)
        o_ref[0, i * tq:(i + 1) * tq, :] = o.T.astype(BF16)

    for r in range(n_pieces):
        scores(0, r)
    for t in range(len(tiles)):
        for r in range(n_pieces):
            if t + 1 < len(tiles):
                scores(t + 1, r)
            softmax(t, r)
            if t > 0:
                accumulate(t - 1, r)
        if t > 0 and tiles[t][0] != tiles[t - 1][0]:
            finalize(tiles[t - 1][0])
    for r in range(n_pieces):
        accumulate(len(tiles) - 1, r)
    finalize(n_q - 1)


def _diff_attention(q3d, kv3d, lam_params, g_sub_col, lambda_init):
    b, s, _ = q3d.shape
    h, dv = DIFF_HEADS, DIFF_V_DIM
    tq, tk = ATT_TQ, ATT_TK
    assert tq % tk == 0 and s % tq == 0
    dv_aug = dv + BF16_ROWS
    n_q = s // tq
    strips = (2, 2 * tq // LANES, tk, LANES)
    vmem = (2 * 4 * s * dv * 2 + 2 * s * dv * 2 + s * dv_aug * 2 + 2 * tk * 2 * tq * (4 + 2)
            + n_q * dv_aug * 2 * tq * 4 + 4 * tk * 2 * tq * 4 + 8 * MIB)

    def head_block(first):
        return pl.BlockSpec((1, s, dv), lambda bi, hi: (bi, 0, first + hi))

    return pl.pallas_call(
        functools.partial(_diff_attn_kernel, lambda_init=lambda_init),
        grid=(b, h),
        in_specs=[
            head_block(0),
            head_block(0),
            head_block(h),
            pl.BlockSpec(lam_params.shape, lambda bi, hi: (0, 0)),
            pl.BlockSpec((dv, 1), lambda bi, hi: (0, 0)),
        ],
        out_specs=head_block(0),
        out_shape=jax.ShapeDtypeStruct((b, s, h * dv), BF16),
        scratch_shapes=[
            pltpu.VMEM((n_q, dv, 2 * tq), BF16),
            pltpu.VMEM((s // tk, dv_aug, tk), BF16),
            pltpu.VMEM(strips, F32),
            pltpu.VMEM(strips, BF16),
            pltpu.VMEM((2, 1, 2 * tq), F32),
            pltpu.VMEM((n_q, 1, 2 * tq), F32),
            pltpu.VMEM((n_q, dv_aug, 2 * tq), F32),
        ],
        compiler_params=_params(("parallel", "parallel"), vmem),
        name="diff_attention",
    )(q3d, kv3d, kv3d, lam_params, g_sub_col)


def kernel(x, a_norm_pre, a_norm_post, a_w_in, a_w_out, kv_norm, w_kv, b_norm_pre, b_norm_post,
           b_w_q, b_lambda, b_subln, b_w_out, ffn_norm_pre, ffn_norm_post, ffn_w_up, ffn_conv_w,
           ffn_conv_b, ffn_w_down):
    b, s, d = x.shape
    t = b * s

    half = RET_QK_DIM // 2
    inv = 1.0 / (ROPE_BASE ** jnp.linspace(0.0, 1.0, half, dtype=F32))
    ang = jnp.arange(s).astype(F32)[:, None] * inv[None, :]
    cos, sin = jnp.cos(ang), jnp.sin(ang)
    log_gamma = jnp.log1p(-jnp.power(2.0, -5.0 - jnp.arange(RET_HEADS, dtype=F32)))

    proj = _proj_in(x.reshape(t, d), a_norm_pre[0][None], a_w_in[0].astype(BF16), cos, sin, s)
    y = _retention(proj.reshape(b, s, -1), log_gamma)
    x = _post(y.reshape(t, -1), a_w_out[0].astype(BF16), a_norm_post[0][None],
              x.reshape(t, d), "ret_out_proj")
    x = _conv_ffn(x.reshape(b, s, d), ffn_norm_pre[0][None], ffn_w_up[0], ffn_conv_w[0],
                  ffn_conv_b[0], ffn_w_down[0], ffn_norm_post[0][None], "conv_ffn_0")

    layer = 1
    lambda_init = 0.8 - 0.6 * math.exp(-0.3 * layer)
    kv, q = _kvq(x.reshape(t, d), kv_norm[None], b_norm_pre[0][None],
                 w_kv.astype(BF16), b_w_q[0].astype(BF16))
    o = _diff_attention(q.reshape(b, s, -1), kv.reshape(b, s, -1), b_lambda[0],
                        b_subln[0][:, None], lambda_init)
    x = _post(o.reshape(t, -1), b_w_out[0].astype(BF16), b_norm_post[0][None],
              x.reshape(t, d), "attn_out_proj")
    x = _conv_ffn(x.reshape(b, s, d), ffn_norm_pre[1][None], ffn_w_up[1], ffn_conv_w[1],
                  ffn_conv_b[1], ffn_w_down[1], ffn_norm_post[1][None], "conv_ffn_1")
    return x
```

```python
import functools
import math

import jax
import jax.numpy as jnp
from jax import lax
from jax.experimental import pallas as pl
from jax.experimental.pallas import tpu as pltpu

F32 = jnp.float32
BF16 = jnp.bfloat16

D_MODEL = 1024
RET_HEADS = 4
RET_QK_DIM = D_MODEL // RET_HEADS
RET_V_DIM = 2 * RET_QK_DIM
RET_CHUNK = 256
DIFF_HEAD_DIM = 64
DIFF_HEADS = D_MODEL // (2 * DIFF_HEAD_DIM)
DIFF_V_DIM = 2 * DIFF_HEAD_DIM
D_FF = ((8 * D_MODEL // 3 + 127) // 128) * 128
ROPE_BASE = 10000.0
EPS = 1e-6
MASK_VALUE = -1e30
LOG2_E = math.log2(math.e)

BF16_ROWS = 16
LANES = 128
MXU_COLS = 256
MIB = 1024 * 1024

PROJ_TM = 512
PROJ_TN = 1024
POST_TM = 512
KVQ_TM = 512
FFN_TM = 512
FFN_TF = 256
FFN_ROWS = 64
ATT_TQ = 512
ATT_TK = 256


def _params(semantics, vmem_bytes):
    return pltpu.CompilerParams(dimension_semantics=semantics,
                                vmem_limit_bytes=int(vmem_bytes))


def _resident(shape):
    zeros = (0,) * len(shape)
    return pl.BlockSpec(shape, lambda *_: zeros, pipeline_mode=pl.Buffered(1))


def _rms_scale(v):
    return lax.rsqrt(jnp.mean(v * v, axis=-1, keepdims=True) + EPS)


def _proj_in_kernel(x_ref, g_ref, w_ref, cos_ref, sin_ref, o_ref, xn_ref):
    x = x_ref[...]
    xn_ref[...] = (x * _rms_scale(x) * g_ref[...]).astype(BF16)
    cos = cos_ref[...]
    sin = sin_ref[...]
    half = RET_QK_DIM // 2
    q_cols = RET_HEADS * RET_QK_DIM
    v_cols = RET_HEADS * RET_V_DIM

    for col0 in range(0, w_ref.shape[1], PROJ_TN):
        a = jnp.dot(xn_ref[...], w_ref[:, col0:col0 + PROJ_TN], preferred_element_type=F32)
        if col0 < 2 * q_cols:
            scale = 1.0 if col0 < q_cols else RET_QK_DIM ** -0.5
            for lo in range(0, PROJ_TN, RET_QK_DIM):
                x1 = a[:, lo:lo + half]
                x2 = a[:, lo + half:lo + 2 * half]
                o_ref[:, col0 + lo:col0 + lo + half] = (
                    (x1 * cos - x2 * sin) * scale).astype(BF16)
                o_ref[:, col0 + lo + half:col0 + lo + 2 * half] = (
                    (x1 * sin + x2 * cos) * scale).astype(BF16)
        elif col0 < 2 * q_cols + v_cols:
            o_ref[:, col0:col0 + PROJ_TN] = a.astype(BF16)
        else:
            o_ref[:, col0:col0 + PROJ_TN] = (a / (1.0 + jnp.exp(-a))).astype(BF16)


def _proj_in(x2d, gain, w, cos, sin, seq):
    t, d = x2d.shape
    n = w.shape[1]
    tm = PROJ_TM
    assert PROJ_TN % RET_QK_DIM == 0 and (RET_HEADS * RET_QK_DIM) % PROJ_TN == 0
    tiles_per_seq = seq // tm
    vmem = (2 * tm * d * 4 + d * n * 2 + 2 * tm * n * 2 + tm * d * 2
            + 6 * tm * PROJ_TN * 4 + 4 * MIB)
    return pl.pallas_call(
        _proj_in_kernel,
        grid=(t // tm,),
        in_specs=[
            pl.BlockSpec((tm, d), lambda i: (i, 0)),
            pl.BlockSpec((1, d), lambda i: (0, 0)),
            _resident((d, n)),
            pl.BlockSpec((tm, RET_QK_DIM // 2), lambda i: (i % tiles_per_seq, 0)),
            pl.BlockSpec((tm, RET_QK_DIM // 2), lambda i: (i % tiles_per_seq, 0)),
        ],
        out_specs=pl.BlockSpec((tm, n), lambda i: (i, 0)),
        out_shape=jax.ShapeDtypeStruct((t, n), BF16),
        scratch_shapes=[pltpu.VMEM((tm, d), BF16)],
        compiler_params=_params(("parallel",), vmem),
        name="ret_proj_in",
    )(x2d, gain, w, cos, sin)


def _retention_kernel(lg_ref, q_ref, k_ref, v_ref, sg_ref, o_ref, state_ref):
    c_len = RET_CHUNK
    lg = lg_ref[pl.program_id(1)]
    row = lax.broadcasted_iota(jnp.int32, (c_len, c_len), 0)
    col = lax.broadcasted_iota(jnp.int32, (c_len, c_len), 1)
    rel = (row - col).astype(F32)
    decay_mask = jnp.where(rel >= 0, jnp.exp(lg * jnp.maximum(rel, 0.0)), 0.0)
    idx = lax.broadcasted_iota(jnp.int32, (c_len, 1), 0).astype(F32)
    q_decay = jnp.exp(lg * (idx + 1.0))
    k_decay = jnp.exp(lg * (c_len - 1.0 - idx))
    chunk_decay = jnp.exp(lg * jnp.full((1, 1), c_len, F32))

    n_chunks = q_ref.shape[1] // c_len

    def rows(c):
        return slice(c * c_len, (c + 1) * c_len)

    def scores(c):
        return lax.dot_general(q_ref[0, rows(c), :], k_ref[0, rows(c), :],
                               (((1,), (1,)), ((), ())), preferred_element_type=F32)

    s_next = scores(0)
    for c in range(n_chunks):
        q = q_ref[0, rows(c), :]
        k = k_ref[0, rows(c), :]
        v = v_ref[0, rows(c), :]
        s_masked = (s_next * decay_mask).astype(BF16)
        out = None
        if c > 0:
            out = jnp.dot(q, state_ref[...].astype(BF16),
                          preferred_element_type=F32) * q_decay
        if c + 1 < n_chunks:
            s_next = scores(c + 1)
            kd = (k.astype(F32) * k_decay).astype(BF16)
            update = lax.dot_general(kd, v, (((0,), (0,)), ((), ())),
                                     preferred_element_type=F32)
        inner = jnp.dot(s_masked, v, preferred_element_type=F32)
        if c + 1 < n_chunks:
            state_ref[...] = update if c == 0 else state_ref[...] * chunk_decay + update
        out = inner if out is None else inner + out
        gate = sg_ref[0, rows(c), :].astype(F32)
        o_ref[0, rows(c), :] = (out * _rms_scale(out) * gate).astype(BF16)


def _retention(proj3d, log_gamma):
    b, s, _ = proj3d.shape
    dk, dv, h = RET_QK_DIM, RET_V_DIM, RET_HEADS
    k_blk0 = h * dk // dk
    v_blk0 = 2 * h * dk // dv
    g_blk0 = (2 * h * dk + h * dv) // dv
    vmem = 2 * s * (2 * dk + 3 * dv) * 2 + dk * dv * 4 + 8 * MIB
    return pl.pallas_call(
        _retention_kernel,
        grid=(b, h),
        in_specs=[
            pl.BlockSpec(memory_space=pltpu.SMEM),
            pl.BlockSpec((1, s, dk), lambda i, j: (i, 0, j)),
            pl.BlockSpec((1, s, dk), lambda i, j: (i, 0, k_blk0 + j)),
            pl.BlockSpec((1, s, dv), lambda i, j: (i, 0, v_blk0 + j)),
            pl.BlockSpec((1, s, dv), lambda i, j: (i, 0, g_blk0 + j)),
        ],
        out_specs=pl.BlockSpec((1, s, dv), lambda i, j: (i, 0, j)),
        out_shape=jax.ShapeDtypeStruct((b, s, h * dv), BF16),
        scratch_shapes=[pltpu.VMEM((dk, dv), F32)],
        compiler_params=_params(("parallel", "parallel"), vmem),
        name="retention_core",
    )(log_gamma, proj3d, proj3d, proj3d, proj3d)


def _post_kernel(y_ref, w_ref, g_ref, x_ref, o_ref):
    h = jnp.dot(y_ref[...], w_ref[...], preferred_element_type=F32)
    o_ref[...] = x_ref[...] + h * _rms_scale(h) * g_ref[...]


def _post(y2d, w, gain, x2d, name):
    t, kdim = y2d.shape
    d = w.shape[1]
    vmem = (2 * POST_TM * kdim * 2 + kdim * d * 2 + 4 * POST_TM * d * 4
            + 2 * POST_TM * d * 4 + 4 * MIB)
    return pl.pallas_call(
        _post_kernel,
        grid=(t // POST_TM,),
        in_specs=[
            pl.BlockSpec((POST_TM, kdim), lambda i: (i, 0)),
            _resident((kdim, d)),
            pl.BlockSpec((1, d), lambda i: (0, 0)),
            pl.BlockSpec((POST_TM, d), lambda i: (i, 0)),
        ],
        out_specs=pl.BlockSpec((POST_TM, d), lambda i: (i, 0)),
        out_shape=jax.ShapeDtypeStruct((t, d), F32),
        compiler_params=_params(("parallel",), vmem),
        name=name,
    )(y2d, w, gain, x2d)


def _ffn_kernel(x_ref, xh_ref, gpre_ref, wup_ref, cw_ref, wdn_ref, gpost_ref, o_ref,
                xn_ref, h0_ref, h1_ref, act_ref):
    tm, f_dim = act_ref.shape
    tf = h0_ref.shape[2]
    n_f = f_dim // tf
    halo = BF16_ROWS
    x = x_ref[0]
    gpre = gpre_ref[...]
    xn_ref[halo:, :] = (x * _rms_scale(x) * gpre).astype(BF16)
    xh = xh_ref[0]
    xhn = jnp.where(pl.program_id(1) > 0, xh * _rms_scale(xh) * gpre, 0.0)
    xn_ref[:halo, :] = xhn.astype(BF16)

    def up_project(f, h_ref):
        xn = xn_ref[...]
        for half in range(2):
            col0 = half * f_dim + f * tf
            h_ref[half] = jnp.dot(xn, wup_ref[:, col0:col0 + tf], preferred_element_type=F32)

    def conv(h_ref, half, f, row0):
        col0 = half * f_dim + f * tf
        cw = cw_ref[:, col0:col0 + tf]
        rows = lambda back: pl.ds(halo + row0 - back, FFN_ROWS)
        return (cw[3:4] + cw[2:3] * h_ref[half, rows(0), :]
                + cw[1:2] * h_ref[half, rows(1), :] + cw[0:1] * h_ref[half, rows(2), :])

    def gate(f, h_ref):
        for row0 in range(0, tm, FFN_ROWS):
            g = conv(h_ref, 0, f, row0)
            u = conv(h_ref, 1, f, row0)
            act_ref[row0:row0 + FFN_ROWS, f * tf:(f + 1) * tf] = (
                g / (1.0 + jnp.exp(-g)) * u).astype(BF16)

    h_refs = (h0_ref, h1_ref)
    up_project(0, h_refs[0])
    for f in range(n_f):
        if f + 1 < n_f:
            up_project(f + 1, h_refs[(f + 1) % 2])
        gate(f, h_refs[f % 2])
    h = jnp.dot(act_ref[...], wdn_ref[...], preferred_element_type=F32)
    o_ref[0] = x + h * _rms_scale(h) * gpost_ref[...]


def _conv_ffn(x3d, gpre, w_up, conv_w, conv_b, w_down, gpost, name):
    b, s, d = x3d.shape
    f_dim = w_down.shape[0]
    assert f_dim % FFN_TF == 0
    tm, halo = FFN_TM, BF16_ROWS
    wup = w_up.astype(BF16)
    wdn = w_down.astype(BF16)
    cw = jnp.concatenate([conv_w, conv_b[None, :], jnp.zeros((4, 2 * f_dim), F32)], axis=0)
    halo_blocks = tm // halo
    vmem = (2 * f_dim * d * 2 + f_dim * d * 2 + 4 * tm * d * 4 + (tm + halo) * d * 2
            + tm * f_dim * 2 + 4 * (tm + halo) * FFN_TF * 4 + 4 * tm * d * 4 + 4 * MIB)
    return pl.pallas_call(
        _ffn_kernel,
        grid=(b, s // tm),
        in_specs=[
            pl.BlockSpec((1, tm, d), lambda i, j: (i, j, 0)),
            pl.BlockSpec((1, halo, d), lambda i, j: (i, jnp.maximum(j * halo_blocks - 1, 0), 0)),
            pl.BlockSpec((1, d), lambda i, j: (0, 0)),
            _resident((d, 2 * f_dim)),
            _resident((8, 2 * f_dim)),
            _resident((f_dim, d)),
            pl.BlockSpec((1, d), lambda i, j: (0, 0)),
        ],
        out_specs=pl.BlockSpec((1, tm, d), lambda i, j: (i, j, 0)),
        out_shape=jax.ShapeDtypeStruct((b, s, d), F32),
        scratch_shapes=[
            pltpu.VMEM((tm + halo, d), BF16),
            pltpu.VMEM((2, tm + halo, FFN_TF), F32),
            pltpu.VMEM((2, tm + halo, FFN_TF), F32),
            pltpu.VMEM((tm, f_dim), BF16),
        ],
        compiler_params=_params(("parallel", "parallel"), vmem),
        name=name,
    )(x3d, x3d, gpre, wup, cw, wdn, gpost)


def _kvq_kernel(x_ref, gkv_ref, gq_ref, wkv_ref, wq_ref, kv_ref, q_ref):
    x = x_ref[...]
    xr = x * _rms_scale(x)
    kv = jnp.dot((xr * gkv_ref[...]).astype(BF16), wkv_ref[...], preferred_element_type=F32)
    kv_ref[...] = kv.astype(BF16)
    q = jnp.dot((xr * gq_ref[...]).astype(BF16), wq_ref[...], preferred_element_type=F32)
    q_ref[...] = (q * (DIFF_HEAD_DIM ** -0.5 * LOG2_E)).astype(BF16)


def _kvq(x2d, g_kv, g_q, w_kv, w_q):
    t, d = x2d.shape
    n_kv, n_q = w_kv.shape[1], w_q.shape[1]
    tm = KVQ_TM
    vmem = (2 * tm * d * 4 + d * (n_kv + n_q) * 2 + 2 * tm * (n_kv + n_q) * 2
            + 6 * tm * d * 4 + 2 * tm * (n_kv + n_q) * 4 + 4 * MIB)
    return pl.pallas_call(
        _kvq_kernel,
        grid=(t // tm,),
        in_specs=[
            pl.BlockSpec((tm, d), lambda i: (i, 0)),
            pl.BlockSpec((1, d), lambda i: (0, 0)),
            pl.BlockSpec((1, d), lambda i: (0, 0)),
            _resident((d, n_kv)),
            _resident((d, n_q)),
        ],
        out_specs=[pl.BlockSpec((tm, n_kv), lambda i: (i, 0)),
                   pl.BlockSpec((tm, n_q), lambda i: (i, 0))],
        out_shape=[jax.ShapeDtypeStruct((t, n_kv), BF16),
                   jax.ShapeDtypeStruct((t, n_q), BF16)],
        compiler_params=_params(("parallel",), vmem),
        name="kv_q_proj",
    )(x2d, g_kv, g_q, w_kv, w_q)


def _diff_attn_kernel(q_ref, k_ref, v_ref, lam_ref, gsub_ref, o_ref,
                      q2t_ref, vt_ref, s_ref, p_ref, a_ref, m_ref, acc_ref, *, lambda_init):
    tq, tk, d, dv = ATT_TQ, ATT_TK, DIFF_HEAD_DIM, DIFF_V_DIM
    strip = LANES
    n_q = q2t_ref.shape[0]
    n_kv = vt_ref.shape[0]
    n_pieces = 2 * tq // MXU_COLS
    strips_per_piece = MXU_COLS // strip

    for j in range(n_kv):
        vt_ref[j, :dv, :] = v_ref[0, j * tk:(j + 1) * tk, :].astype(F32).T.astype(BF16)
    vt_ref[:, dv:, :] = jnp.ones((n_kv, BF16_ROWS, tk), BF16)

    for i in range(n_q):
        qt = q_ref[0, i * tq:(i + 1) * tq, :].astype(F32).T
        feat = lax.broadcasted_iota(jnp.int32, qt.shape, 0)
        q2t_ref[i, :, :tq] = jnp.where(feat < d, qt, 0.0).astype(BF16)
        q2t_ref[i, :, tq:] = jnp.where(feat >= d, qt, 0.0).astype(BF16)

    lp = lam_ref[...]
    lam = (jnp.exp(jnp.sum(lp[0:1] * lp[1:2], axis=-1, keepdims=True))
           - jnp.exp(jnp.sum(lp[2:3] * lp[3:4], axis=-1, keepdims=True)) + lambda_init)

    tiles = [(i, j) for i in range(n_q) for j in range((i + 1) * tq // tk)]

    def piece(r):
        return slice(r * MXU_COLS, (r + 1) * MXU_COLS)

    def visibility(i, j, c):
        q_first = i * tq + (c * strip) % tq
        k_first = j * tk
        if k_first > q_first + strip - 1:
            return "none"
        return "some" if k_first + tk - 1 > q_first else "all"

    def piece_visible(i, j, r):
        seen = [visibility(i, j, c) != "none"
                for c in range(r * strips_per_piece, (r + 1) * strips_per_piece)]
        assert all(seen) or not any(seen)
        return seen[0]

    def scores(t, r):
        i, j = tiles[t]
        if not piece_visible(i, j, r):
            return
        res = jnp.dot(k_ref[0, j * tk:(j + 1) * tk, :], q2t_ref[i, :, piece(r)],
                      preferred_element_type=F32)
        for c in range(strips_per_piece):
            s_ref[t % 2, r * strips_per_piece + c] = res[:, c * strip:(c + 1) * strip]

    def softmax(t, r):
        i, j = tiles[t]
        if not piece_visible(i, j, r):
            return
        for c in range(r * strips_per_piece, (r + 1) * strips_per_piece):
            lanes = slice(c * strip, (c + 1) * strip)
            s = s_ref[t % 2, c]
            if visibility(i, j, c) == "some":
                rel = (lax.broadcasted_iota(jnp.int32, (tk, strip), 1)
                       - lax.broadcasted_iota(jnp.int32, (tk, strip), 0))
                s = jnp.where(rel >= j * tk - i * tq - (c * strip) % tq, s, MASK_VALUE)
            m_new = jnp.max(s, axis=0, keepdims=True)
            if j > 0:
                m_old = m_ref[i, :, lanes]
                m_new = jnp.maximum(m_old, m_new)
                a_ref[t % 2, :, lanes] = jnp.exp2(m_old - m_new)
            m_ref[i, :, lanes] = m_new
            p_ref[t % 2, c] = jnp.exp2(s - m_new).astype(BF16)

    def accumulate(t, r):
        i, j = tiles[t]
        if not piece_visible(i, j, r):
            return
        p = jnp.concatenate([p_ref[t % 2, r * strips_per_piece + c]
                             for c in range(strips_per_piece)], axis=1)
        pv = jnp.dot(vt_ref[j], p, preferred_element_type=F32)
        if j > 0:
            pv = a_ref[t % 2, :, piece(r)] * acc_ref[i, :, piece(r)] + pv
        acc_ref[i, :, piece(r)] = pv

    def finalize(i):
        inv_l = 1.0 / acc_ref[i, dv:dv + 1, :]
        o = (acc_ref[i, :dv, :tq] * inv_l[:, :tq]
             - lam * (acc_ref[i, :dv, tq:] * inv_l[:, tq:]))
        r = lax.rsqrt(jnp.mean(o * o, axis=0, keepdims=True) + EPS)
        o = o * r * gsub_ref[...] * (1.0 - lambda_init)
        o_ref[0, i * tq:(i + 1) * tq, :] = o.T.astype(BF16)

    for r in range(n_pieces):
        scores(0, r)
    for t in range(len(tiles)):
        for r in range(n_pieces):
            if t + 1 < len(tiles):
                scores(t + 1, r)
            softmax(t, r)
            if t > 0:
                accumulate(t - 1, r)
        if t > 0 and tiles[t][0] != tiles[t - 1][0]:
            finalize(tiles[t - 1][0])
    for r in range(n_pieces):
        accumulate(len(tiles) - 1, r)
    finalize(n_q - 1)


def _diff_attention(q3d, kv3d, lam_params, g_sub_col, lambda_init):
    b, s, _ = q3d.shape
    h, dv = DIFF_HEADS, DIFF_V_DIM
    tq, tk = ATT_TQ, ATT_TK
    assert tq % tk == 0 and s % tq == 0
    dv_aug = dv + BF16_ROWS
    n_q = s // tq
    strips = (2, 2 * tq // LANES, tk, LANES)
    vmem = (2 * 4 * s * dv * 2 + 2 * s * dv * 2 + s * dv_aug * 2 + 2 * tk * 2 * tq * (4 + 2)
            + n_q * dv_aug * 2 * tq * 4 + 4 * tk * 2 * tq * 4 + 8 * MIB)

    def head_block(first):
        return pl.BlockSpec((1, s, dv), lambda bi, hi: (bi, 0, first + hi))

    return pl.pallas_call(
        functools.partial(_diff_attn_kernel, lambda_init=lambda_init),
        grid=(b, h),
        in_specs=[
            head_block(0),
            head_block(0),
            head_block(h),
            pl.BlockSpec(lam_params.shape, lambda bi, hi: (0, 0)),
            pl.BlockSpec((dv, 1), lambda bi, hi: (0, 0)),
        ],
        out_specs=head_block(0),
        out_shape=jax.ShapeDtypeStruct((b, s, h * dv), BF16),
        scratch_shapes=[
            pltpu.VMEM((n_q, dv, 2 * tq), BF16),
            pltpu.VMEM((s // tk, dv_aug, tk), BF16),
            pltpu.VMEM(strips, F32),
            pltpu.VMEM(strips, BF16),
            pltpu.VMEM((2, 1, 2 * tq), F32),
            pltpu.VMEM((n_q, 1, 2 * tq), F32),
            pltpu.VMEM((n_q, dv_aug, 2 * tq), F32),
        ],
        compiler_params=_params(("parallel", "parallel"), vmem),
        name="diff_attention",
    )(q3d, kv3d, kv3d, lam_params, g_sub_col)


def kernel(x, a_norm_pre, a_norm_post, a_w_in, a_w_out, kv_norm, w_kv, b_norm_pre, b_norm_post,
           b_w_q, b_lambda, b_subln, b_w_out, ffn_norm_pre, ffn_norm_post, ffn_w_up, ffn_conv_w,
           ffn_conv_b, ffn_w_down):
    b, s, d = x.shape
    t = b * s

    half = RET_QK_DIM // 2
    inv = 1.0 / (ROPE_BASE ** jnp.linspace(0.0, 1.0, half, dtype=F32))
    ang = jnp.arange(s).astype(F32)[:, None] * inv[None, :]
    cos, sin = jnp.cos(ang), jnp.sin(ang)
    log_gamma = jnp.log1p(-jnp.power(2.0, -5.0 - jnp.arange(RET_HEADS, dtype=F32)))

    proj = _proj_in(x.reshape(t, d), a_norm_pre[0][None], a_w_in[0].astype(BF16), cos, sin, s)
    y = _retention(proj.reshape(b, s, -1), log_gamma)
    x = _post(y.reshape(t, -1), a_w_out[0].astype(BF16), a_norm_post[0][None],
              x.reshape(t, d), "ret_out_proj")
    x = _conv_ffn(x.reshape(b, s, d), ffn_norm_pre[0][None], ffn_w_up[0], ffn_conv_w[0],
                  ffn_conv_b[0], ffn_w_down[0], ffn_norm_post[0][None], "conv_ffn_0")

    layer = 1
    lambda_init = 0.8 - 0.6 * math.exp(-0.3 * layer)
    kv, q = _kvq(x.reshape(t, d), kv_norm[None], b_norm_pre[0][None],
                 w_kv.astype(BF16), b_w_q[0].astype(BF16))
    o = _diff_attention(q.reshape(b, s, -1), kv.reshape(b, s, -1), b_lambda[0],
                        b_subln[0][:, None], lambda_init)
    x = _post(o.reshape(t, -1), b_w_out[0].astype(BF16), b_norm_post[0][None],
              x.reshape(t, d), "attn_out_proj")
    x = _conv_ffn(x.reshape(b, s, d), ffn_norm_pre[1][None], ffn_w_up[1], ffn_conv_w[1],
                  ffn_conv_b[1], ffn_w_down[1], ffn_norm_post[1][None], "conv_ffn_1")
    return x
```

```python
import functools
import math

import jax
import jax.numpy as jnp
from jax import lax
from jax.experimental import pallas as pl
from jax.experimental.pallas import tpu as pltpu

F32 = jnp.float32
BF16 = jnp.bfloat16

D_MODEL = 1024
RET_HEADS = 4
RET_QK_DIM = D_MODEL // RET_HEADS
RET_V_DIM = 2 * RET_QK_DIM
RET_CHUNK = 256
DIFF_HEAD_DIM = 64
DIFF_HEADS = D_MODEL // (2 * DIFF_HEAD_DIM)
DIFF_V_DIM = 2 * DIFF_HEAD_DIM
D_FF = ((8 * D_MODEL // 3 + 127) // 128) * 128
ROPE_BASE = 10000.0
EPS = 1e-6
MASK_VALUE = -1e30
LOG2_E = math.log2(math.e)

BF16_ROWS = 16
LANES = 128
MXU_COLS = 256
MIB = 1024 * 1024

PROJ_TM = 512
PROJ_TN = 1024
POST_TM = 512
KVQ_TM = 512
FFN_TM = 512
FFN_TF = 256
FFN_ROWS = 512
ATT_TQ = 512
ATT_TK = 256


def _params(semantics, vmem_bytes):
    return pltpu.CompilerParams(dimension_semantics=semantics,
                                vmem_limit_bytes=int(vmem_bytes))


def _resident(shape):
    zeros = (0,) * len(shape)
    return pl.BlockSpec(shape, lambda *_: zeros, pipeline_mode=pl.Buffered(1))


def _rms_scale(v):
    return lax.rsqrt(jnp.mean(v * v, axis=-1, keepdims=True) + EPS)


def _proj_in_kernel(x_ref, g_ref, w_ref, cos_ref, sin_ref, o_ref, xn_ref):
    x = x_ref[...]
    xn_ref[...] = (x * _rms_scale(x) * g_ref[...]).astype(BF16)
    cos = cos_ref[...]
    sin = sin_ref[...]
    half = RET_QK_DIM // 2
    q_cols = RET_HEADS * RET_QK_DIM
    v_cols = RET_HEADS * RET_V_DIM

    for col0 in range(0, w_ref.shape[1], PROJ_TN):
        a = jnp.dot(xn_ref[...], w_ref[:, col0:col0 + PROJ_TN], preferred_element_type=F32)
        if col0 < 2 * q_cols:
            scale = 1.0 if col0 < q_cols else RET_QK_DIM ** -0.5
            for lo in range(0, PROJ_TN, RET_QK_DIM):
                x1 = a[:, lo:lo + half]
                x2 = a[:, lo + half:lo + 2 * half]
                o_ref[:, col0 + lo:col0 + lo + half] = (
                    (x1 * cos - x2 * sin) * scale).astype(BF16)
                o_ref[:, col0 + lo + half:col0 + lo + 2 * half] = (
                    (x1 * sin + x2 * cos) * scale).astype(BF16)
        elif col0 < 2 * q_cols + v_cols:
            o_ref[:, col0:col0 + PROJ_TN] = a.astype(BF16)
        else:
            o_ref[:, col0:col0 + PROJ_TN] = (a / (1.0 + jnp.exp(-a))).astype(BF16)


def _proj_in(x2d, gain, w, cos, sin, seq):
    t, d = x2d.shape
    n = w.shape[1]
    tm = PROJ_TM
    assert PROJ_TN % RET_QK_DIM == 0 and (RET_HEADS * RET_QK_DIM) % PROJ_TN == 0
    tiles_per_seq = seq // tm
    vmem = (2 * tm * d * 4 + d * n * 2 + 2 * tm * n * 2 + tm * d * 2
            + 6 * tm * PROJ_TN * 4 + 4 * MIB)
    return pl.pallas_call(
        _proj_in_kernel,
        grid=(t // tm,),
        in_specs=[
            pl.BlockSpec((tm, d), lambda i: (i, 0)),
            pl.BlockSpec((1, d), lambda i: (0, 0)),
            _resident((d, n)),
            pl.BlockSpec((tm, RET_QK_DIM // 2), lambda i: (i % tiles_per_seq, 0)),
            pl.BlockSpec((tm, RET_QK_DIM // 2), lambda i: (i % tiles_per_seq, 0)),
        ],
        out_specs=pl.BlockSpec((tm, n), lambda i: (i, 0)),
        out_shape=jax.ShapeDtypeStruct((t, n), BF16),
        scratch_shapes=[pltpu.VMEM((tm, d), BF16)],
        compiler_params=_params(("parallel",), vmem),
        name="ret_proj_in",
    )(x2d, gain, w, cos, sin)


def _retention_kernel(lg_ref, q_ref, k_ref, v_ref, sg_ref, o_ref, state_ref):
    c_len = RET_CHUNK
    lg = lg_ref[pl.program_id(1)]
    row = lax.broadcasted_iota(jnp.int32, (c_len, c_len), 0)
    col = lax.broadcasted_iota(jnp.int32, (c_len, c_len), 1)
    rel = (row - col).astype(F32)
    decay_mask = jnp.where(rel >= 0, jnp.exp(lg * jnp.maximum(rel, 0.0)), 0.0)
    idx = lax.broadcasted_iota(jnp.int32, (c_len, 1), 0).astype(F32)
    q_decay = jnp.exp(lg * (idx + 1.0))
    k_decay = jnp.exp(lg * (c_len - 1.0 - idx))
    chunk_decay = jnp.exp(lg * jnp.full((1, 1), c_len, F32))

    n_chunks = q_ref.shape[1] // c_len

    def rows(c):
        return slice(c * c_len, (c + 1) * c_len)

    def scores(c):
        return lax.dot_general(q_ref[0, rows(c), :], k_ref[0, rows(c), :],
                               (((1,), (1,)), ((), ())), preferred_element_type=F32)

    s_next = scores(0)
    for c in range(n_chunks):
        q = q_ref[0, rows(c), :]
        k = k_ref[0, rows(c), :]
        v = v_ref[0, rows(c), :]
        s_masked = (s_next * decay_mask).astype(BF16)
        out = None
        if c > 0:
            out = jnp.dot(q, state_ref[...].astype(BF16),
                          preferred_element_type=F32) * q_decay
        if c + 1 < n_chunks:
            s_next = scores(c + 1)
            kd = (k.astype(F32) * k_decay).astype(BF16)
            update = lax.dot_general(kd, v, (((0,), (0,)), ((), ())),
                                     preferred_element_type=F32)
        inner = jnp.dot(s_masked, v, preferred_element_type=F32)
        if c + 1 < n_chunks:
            state_ref[...] = update if c == 0 else state_ref[...] * chunk_decay + update
        out = inner if out is None else inner + out
        gate = sg_ref[0, rows(c), :].astype(F32)
        o_ref[0, rows(c), :] = (out * _rms_scale(out) * gate).astype(BF16)


def _retention(proj3d, log_gamma):
    b, s, _ = proj3d.shape
    dk, dv, h = RET_QK_DIM, RET_V_DIM, RET_HEADS
    k_blk0 = h * dk // dk
    v_blk0 = 2 * h * dk // dv
    g_blk0 = (2 * h * dk + h * dv) // dv
    vmem = 2 * s * (2 * dk + 3 * dv) * 2 + dk * dv * 4 + 8 * MIB
    return pl.pallas_call(
        _retention_kernel,
        grid=(b, h),
        in_specs=[
            pl.BlockSpec(memory_space=pltpu.SMEM),
            pl.BlockSpec((1, s, dk), lambda i, j: (i, 0, j)),
            pl.BlockSpec((1, s, dk), lambda i, j: (i, 0, k_blk0 + j)),
            pl.BlockSpec((1, s, dv), lambda i, j: (i, 0, v_blk0 + j)),
            pl.BlockSpec((1, s, dv), lambda i, j: (i, 0, g_blk0 + j)),
        ],
        out_specs=pl.BlockSpec((1, s, dv), lambda i, j: (i, 0, j)),
        out_shape=jax.ShapeDtypeStruct((b, s, h * dv), BF16),
        scratch_shapes=[pltpu.VMEM((dk, dv), F32)],
        compiler_params=_params(("parallel", "parallel"), vmem),
        name="retention_core",
    )(log_gamma, proj3d, proj3d, proj3d, proj3d)


def _post_kernel(y_ref, w_ref, g_ref, x_ref, o_ref):
    h = jnp.dot(y_ref[...], w_ref[...], preferred_element_type=F32)
    o_ref[...] = x_ref[...] + h * _rms_scale(h) * g_ref[...]


def _post(y2d, w, gain, x2d, name):
    t, kdim = y2d.shape
    d = w.shape[1]
    vmem = (2 * POST_TM * kdim * 2 + kdim * d * 2 + 4 * POST_TM * d * 4
            + 2 * POST_TM * d * 4 + 4 * MIB)
    return pl.pallas_call(
        _post_kernel,
        grid=(t // POST_TM,),
        in_specs=[
            pl.BlockSpec((POST_TM, kdim), lambda i: (i, 0)),
            _resident((kdim, d)),
            pl.BlockSpec((1, d), lambda i: (0, 0)),
            pl.BlockSpec((POST_TM, d), lambda i: (i, 0)),
        ],
        out_specs=pl.BlockSpec((POST_TM, d), lambda i: (i, 0)),
        out_shape=jax.ShapeDtypeStruct((t, d), F32),
        compiler_params=_params(("parallel",), vmem),
        name=name,
    )(y2d, w, gain, x2d)


def _ffn_kernel(x_ref, xh_ref, gpre_ref, wup_ref, cw_ref, wdn_ref, gpost_ref, o_ref,
                xn_ref, h0_ref, h1_ref, act_ref):
    tm, f_dim = act_ref.shape
    tf = h0_ref.shape[2]
    n_f = f_dim // tf
    halo = BF16_ROWS
    x = x_ref[0]
    gpre = gpre_ref[...]
    xn_ref[halo:, :] = (x * _rms_scale(x) * gpre).astype(BF16)
    xh = xh_ref[0]
    xhn = jnp.where(pl.program_id(1) > 0, xh * _rms_scale(xh) * gpre, 0.0)
    xn_ref[:halo, :] = xhn.astype(BF16)

    def up_project(f, h_ref):
        xn = xn_ref[...]
        for half in range(2):
            col0 = half * f_dim + f * tf
            h_ref[half] = jnp.dot(xn, wup_ref[:, col0:col0 + tf], preferred_element_type=F32)

    def conv(h_ref, half, f, row0):
        col0 = half * f_dim + f * tf
        cw = cw_ref[:, col0:col0 + tf]
        rows = lambda back: pl.ds(halo + row0 - back, FFN_ROWS)
        return (cw[3:4] + cw[2:3] * h_ref[half, rows(0), :]
                + cw[1:2] * h_ref[half, rows(1), :] + cw[0:1] * h_ref[half, rows(2), :])

    def gate(f, h_ref):
        for row0 in range(0, tm, FFN_ROWS):
            half_g = conv(h_ref, 0, f, row0)
            u = conv(h_ref, 1, f, row0)
            act_ref[row0:row0 + FFN_ROWS, f * tf:(f + 1) * tf] = (
                half_g * (1.0 + jnp.tanh(half_g)) * u).astype(BF16)

    h_refs = (h0_ref, h1_ref)
    up_project(0, h_refs[0])
    for f in range(n_f):
        if f + 1 < n_f:
            up_project(f + 1, h_refs[(f + 1) % 2])
        gate(f, h_refs[f % 2])
    h = jnp.dot(act_ref[...], wdn_ref[...], preferred_element_type=F32)
    o_ref[0] = x + h * _rms_scale(h) * gpost_ref[...]


def _conv_ffn(x3d, gpre, w_up, conv_w, conv_b, w_down, gpost, name):
    b, s, d = x3d.shape
    f_dim = w_down.shape[0]
    assert f_dim % FFN_TF == 0
    tm, halo = FFN_TM, BF16_ROWS
    wup = w_up.astype(BF16)
    wdn = w_down.astype(BF16)
    cw = jnp.concatenate([conv_w, conv_b[None, :], jnp.zeros((4, 2 * f_dim), F32)], axis=0)
    cw = cw * jnp.where(jnp.arange(2 * f_dim) < f_dim, 0.5, 1.0)[None, :]
    halo_blocks = tm // halo
    vmem = (2 * f_dim * d * 2 + f_dim * d * 2 + 4 * tm * d * 4 + (tm + halo) * d * 2
            + tm * f_dim * 2 + 4 * (tm + halo) * FFN_TF * 4 + 4 * tm * d * 4 + 4 * MIB)
    return pl.pallas_call(
        _ffn_kernel,
        grid=(b, s // tm),
        in_specs=[
            pl.BlockSpec((1, tm, d), lambda i, j: (i, j, 0)),
            pl.BlockSpec((1, halo, d), lambda i, j: (i, jnp.maximum(j * halo_blocks - 1, 0), 0)),
            pl.BlockSpec((1, d), lambda i, j: (0, 0)),
            _resident((d, 2 * f_dim)),
            _resident((8, 2 * f_dim)),
            _resident((f_dim, d)),
            pl.BlockSpec((1, d), lambda i, j: (0, 0)),
        ],
        out_specs=pl.BlockSpec((1, tm, d), lambda i, j: (i, j, 0)),
        out_shape=jax.ShapeDtypeStruct((b, s, d), F32),
        scratch_shapes=[
            pltpu.VMEM((tm + halo, d), BF16),
            pltpu.VMEM((2, tm + halo, FFN_TF), F32),
            pltpu.VMEM((2, tm + halo, FFN_TF), F32),
            pltpu.VMEM((tm, f_dim), BF16),
        ],
        compiler_params=_params(("parallel", "parallel"), vmem),
        name=name,
    )(x3d, x3d, gpre, wup, cw, wdn, gpost)


def _kvq_kernel(x_ref, gkv_ref, gq_ref, wkv_ref, wq_ref, kv_ref, q_ref):
    x = x_ref[...]
    xr = x * _rms_scale(x)
    kv = jnp.dot((xr * gkv_ref[...]).astype(BF16), wkv_ref[...], preferred_element_type=F32)
    kv_ref[...] = kv.astype(BF16)
    q = jnp.dot((xr * gq_ref[...]).astype(BF16), wq_ref[...], preferred_element_type=F32)
    q_ref[...] = (q * (DIFF_HEAD_DIM ** -0.5 * LOG2_E)).astype(BF16)


def _kvq(x2d, g_kv, g_q, w_kv, w_q):
    t, d = x2d.shape
    n_kv, n_q = w_kv.shape[1], w_q.shape[1]
    tm = KVQ_TM
    vmem = (2 * tm * d * 4 + d * (n_kv + n_q) * 2 + 2 * tm * (n_kv + n_q) * 2
            + 6 * tm * d * 4 + 2 * tm * (n_kv + n_q) * 4 + 4 * MIB)
    return pl.pallas_call(
        _kvq_kernel,
        grid=(t // tm,),
        in_specs=[
            pl.BlockSpec((tm, d), lambda i: (i, 0)),
            pl.BlockSpec((1, d), lambda i: (0, 0)),
            pl.BlockSpec((1, d), lambda i: (0, 0)),
            _resident((d, n_kv)),
            _resident((d, n_q)),
        ],
        out_specs=[pl.BlockSpec((tm, n_kv), lambda i: (i, 0)),
                   pl.BlockSpec((tm, n_q), lambda i: (i, 0))],
        out_shape=[jax.ShapeDtypeStruct((t, n_kv), BF16),
                   jax.ShapeDtypeStruct((t, n_q), BF16)],
        compiler_params=_params(("parallel",), vmem),
        name="kv_q_proj",
    )(x2d, g_kv, g_q, w_kv, w_q)


def _diff_attn_kernel(q_ref, k_ref, v_ref, lam_ref, gsub_ref, o_ref,
                      q2t_ref, vt_ref, s_ref, p_ref, a_ref, m_ref, acc_ref, *, lambda_init):
    tq, tk, d, dv = ATT_TQ, ATT_TK, DIFF_HEAD_DIM, DIFF_V_DIM
    strip = LANES
    n_q = q2t_ref.shape[0]
    n_kv = vt_ref.shape[0]
    n_pieces = 2 * tq // MXU_COLS
    strips_per_piece = MXU_COLS // strip

    for j in range(n_kv):
        vt_ref[j, :dv, :] = v_ref[0, j * tk:(j + 1) * tk, :].astype(F32).T.astype(BF16)
    vt_ref[:, dv:, :] = jnp.ones((n_kv, BF16_ROWS, tk), BF16)

    for i in range(n_q):
        qt = q_ref[0, i * tq:(i + 1) * tq, :].astype(F32).T
        feat = lax.broadcasted_iota(jnp.int32, qt.shape, 0)
        q2t_ref[i, :, :tq] = jnp.where(feat < d, qt, 0.0).astype(BF16)
        q2t_ref[i, :, tq:] = jnp.where(feat >= d, qt, 0.0).astype(BF16)

    lp = lam_ref[...]
    lam = (jnp.exp(jnp.sum(lp[0:1] * lp[1:2], axis=-1, keepdims=True))
           - jnp.exp(jnp.sum(lp[2:3] * lp[3:4], axis=-1, keepdims=True)) + lambda_init)

    tiles = [(i, j) for i in range(n_q) for j in range((i + 1) * tq // tk)]

    def piece(r):
        return slice(r * MXU_COLS, (r + 1) * MXU_COLS)

    def visibility(i, j, c):
        q_first = i * tq + (c * strip) % tq
        k_first = j * tk
        if k_first > q_first + strip - 1:
            return "none"
        return "some" if k_first + tk - 1 > q_first else "all"

    def piece_visible(i, j, r):
        seen = [visibility(i, j, c) != "none"
                for c in range(r * strips_per_piece, (r + 1) * strips_per_piece)]
        assert all(seen) or not any(seen)
        return seen[0]

    def scores(t, r):
        i, j = tiles[t]
        if not piece_visible(i, j, r):
            return
        res = jnp.dot(k_ref[0, j * tk:(j + 1) * tk, :], q2t_ref[i, :, piece(r)],
                      preferred_element_type=F32)
        for c in range(strips_per_piece):
            s_ref[t % 2, r * strips_per_piece + c] = res[:, c * strip:(c + 1) * strip]

    def softmax(t, r):
        i, j = tiles[t]
        if not piece_visible(i, j, r):
            return
        for c in range(r * strips_per_piece, (r + 1) * strips_per_piece):
            lanes = slice(c * strip, (c + 1) * strip)
            s = s_ref[t % 2, c]
            if visibility(i, j, c) == "some":
                rel = (lax.broadcasted_iota(jnp.int32, (tk, strip), 1)
                       - lax.broadcasted_iota(jnp.int32, (tk, strip), 0))
                s = jnp.where(rel >= j * tk - i * tq - (c * strip) % tq, s, MASK_VALUE)
            m_new = jnp.max(s, axis=0, keepdims=True)
            if j > 0:
                m_old = m_ref[i, :, lanes]
                m_new = jnp.maximum(m_old, m_new)
                a_ref[t % 2, :, lanes] = jnp.exp2(m_old - m_new)
            m_ref[i, :, lanes] = m_new
            p_ref[t % 2, c] = jnp.exp2(s - m_new).astype(BF16)

    def accumulate(t, r):
        i, j = tiles[t]
        if not piece_visible(i, j, r):
            return
        p = jnp.concatenate([p_ref[t % 2, r * strips_per_piece + c]
                             for c in range(strips_per_piece)], axis=1)
        pv = jnp.dot(vt_ref[j], p, preferred_element_type=F32)
        if j > 0:
            pv = a_ref[t % 2, :, piece(r)] * acc_ref[i, :, piece(r)] + pv
        acc_ref[i, :, piece(r)] = pv

    def finalize(i):
        inv_l = 1.0 / acc_ref[i, dv:dv + 1, :]
        o = (acc_ref[i, :dv, :tq] * inv_l[:, :tq]
             - lam * (acc_ref[i, :dv, tq:] * inv_l[:, tq:]))
        r = lax.rsqrt(jnp.mean(o * o, axis=0, keepdims=True) + EPS)
        o = o * r * gsub_ref[...] * (1.0 - lambda_init)
        o_ref[0, i * tq:(i + 1) * tq, :] = o.T.astype(BF16)

    for r in range(n_pieces):
        scores(0, r)
    for t in range(len(tiles)):
        for r in range(n_pieces):
            if t + 1 < len(tiles):
                scores(t + 1, r)
            softmax(t, r)
            if t > 0:
                accumulate(t - 1, r)
        if t > 0 and tiles[t][0] != tiles[t - 1][0]:
            finalize(tiles[t - 1][0])
    for r in range(n_pieces):
        accumulate(len(tiles) - 1, r)
    finalize(n_q - 1)


def _diff_attention(q3d, kv3d, lam_params, g_sub_col, lambda_init):
    b, s, _ = q3d.shape
    h, dv = DIFF_HEADS, DIFF_V_DIM
    tq, tk = ATT_TQ, ATT_TK
    assert tq % tk == 0 and s % tq == 0
    dv_aug = dv + BF16_ROWS
    n_q = s // tq
    strips = (2, 2 * tq // LANES, tk, LANES)
    vmem = (2 * 4 * s * dv * 2 + 2 * s * dv * 2 + s * dv_aug * 2 + 2 * tk * 2 * tq * (4 + 2)
            + n_q * dv_aug * 2 * tq * 4 + 4 * tk * 2 * tq * 4 + 8 * MIB)

    def head_block(first):
        return pl.BlockSpec((1, s, dv), lambda bi, hi: (bi, 0, first + hi))

    return pl.pallas_call(
        functools.partial(_diff_attn_kernel, lambda_init=lambda_init),
        grid=(b, h),
        in_specs=[
            head_block(0),
            head_block(0),
            head_block(h),
            pl.BlockSpec(lam_params.shape, lambda bi, hi: (0, 0)),
            pl.BlockSpec((dv, 1), lambda bi, hi: (0, 0)),
        ],
        out_specs=head_block(0),
        out_shape=jax.ShapeDtypeStruct((b, s, h * dv), BF16),
        scratch_shapes=[
            pltpu.VMEM((n_q, dv, 2 * tq), BF16),
            pltpu.VMEM((s // tk, dv_aug, tk), BF16),
            pltpu.VMEM(strips, F32),
            pltpu.VMEM(strips, BF16),
            pltpu.VMEM((2, 1, 2 * tq), F32),
            pltpu.VMEM((n_q, 1, 2 * tq), F32),
            pltpu.VMEM((n_q, dv_aug, 2 * tq), F32),
        ],
        compiler_params=_params(("parallel", "parallel"), vmem),
        name="diff_attention",
    )(q3d, kv3d, kv3d, lam_params, g_sub_col)


def kernel(x, a_norm_pre, a_norm_post, a_w_in, a_w_out, kv_norm, w_kv, b_norm_pre, b_norm_post,
           b_w_q, b_lambda, b_subln, b_w_out, ffn_norm_pre, ffn_norm_post, ffn_w_up, ffn_conv_w,
           ffn_conv_b, ffn_w_down):
    b, s, d = x.shape
    t = b * s

    half = RET_QK_DIM // 2
    inv = 1.0 / (ROPE_BASE ** jnp.linspace(0.0, 1.0, half, dtype=F32))
    ang = jnp.arange(s).astype(F32)[:, None] * inv[None, :]
    cos, sin = jnp.cos(ang), jnp.sin(ang)
    log_gamma = jnp.log1p(-jnp.power(2.0, -5.0 - jnp.arange(RET_HEADS, dtype=F32)))

    proj = _proj_in(x.reshape(t, d), a_norm_pre[0][None], a_w_in[0].astype(BF16), cos, sin, s)
    y = _retention(proj.reshape(b, s, -1), log_gamma)
    x = _post(y.reshape(t, -1), a_w_out[0].astype(BF16), a_norm_post[0][None],
              x.reshape(t, d), "ret_out_proj")
    x = _conv_ffn(x.reshape(b, s, d), ffn_norm_pre[0][None], ffn_w_up[0], ffn_conv_w[0],
                  ffn_conv_b[0], ffn_w_down[0], ffn_norm_post[0][None], "conv_ffn_0")

    layer = 1
    lambda_init = 0.8 - 0.6 * math.exp(-0.3 * layer)
    kv, q = _kvq(x.reshape(t, d), kv_norm[None], b_norm_pre[0][None],
                 w_kv.astype(BF16), b_w_q[0].astype(BF16))
    o = _diff_attention(q.reshape(b, s, -1), kv.reshape(b, s, -1), b_lambda[0],
                        b_subln[0][:, None], lambda_init)
    x = _post(o.reshape(t, -1), b_w_out[0].astype(BF16), b_norm_post[0][None],
              x.reshape(t, d), "attn_out_proj")
    x = _conv_ffn(x.reshape(b, s, d), ffn_norm_pre[1][None], ffn_w_up[1], ffn_conv_w[1],
                  ffn_conv_b[1], ffn_w_down[1], ffn_norm_post[1][None], "conv_ffn_1")
    return x
```

```python
import functools
import math

import jax
import jax.numpy as jnp
from jax import lax
from jax.experimental import pallas as pl
from jax.experimental.pallas import tpu as pltpu

F32 = jnp.float32
BF16 = jnp.bfloat16

D_MODEL = 1024
RET_HEADS = 4
RET_QK_DIM = D_MODEL // RET_HEADS
RET_V_DIM = 2 * RET_QK_DIM
RET_CHUNK = 256
DIFF_HEAD_DIM = 64
DIFF_HEADS = D_MODEL // (2 * DIFF_HEAD_DIM)
DIFF_V_DIM = 2 * DIFF_HEAD_DIM
D_FF = ((8 * D_MODEL // 3 + 127) // 128) * 128
ROPE_BASE = 10000.0
EPS = 1e-6
MASK_VALUE = -1e30
LOG2_E = math.log2(math.e)

BF16_ROWS = 16
LANES = 128
MXU_COLS = 256
MIB = 1024 * 1024

PROJ_TM = 512
PROJ_TN = 1024
POST_TM = 1024
KVQ_TM = 1024
FFN_TM = 512
FFN_TILES = 1
FFN_TF = 256
ATT_TQ = 512
ATT_TK = 256
ATT_HEADS = 1


def _params(semantics, vmem_bytes):
    return pltpu.CompilerParams(dimension_semantics=semantics,
                                vmem_limit_bytes=int(vmem_bytes))


def _resident(shape):
    zeros = (0,) * len(shape)
    return pl.BlockSpec(shape, lambda *_: zeros, pipeline_mode=pl.Buffered(1))


def _rms_scale(v):
    return lax.rsqrt(jnp.mean(v * v, axis=-1, keepdims=True) + EPS)


def _proj_in_kernel(x_ref, g_ref, w_ref, cos_ref, sin_ref, o_ref, xn_ref):
    x = x_ref[...]
    xn_ref[...] = (x * _rms_scale(x) * g_ref[...]).astype(BF16)
    cos = cos_ref[...]
    sin = sin_ref[...]
    half = RET_QK_DIM // 2
    q_cols = RET_HEADS * RET_QK_DIM
    v_cols = RET_HEADS * RET_V_DIM

    for col0 in range(0, w_ref.shape[1], PROJ_TN):
        a = jnp.dot(xn_ref[...], w_ref[:, col0:col0 + PROJ_TN], preferred_element_type=F32)
        if col0 < 2 * q_cols:
            scale = 1.0 if col0 < q_cols else RET_QK_DIM ** -0.5
            for lo in range(0, PROJ_TN, RET_QK_DIM):
                x1 = a[:, lo:lo + half]
                x2 = a[:, lo + half:lo + 2 * half]
                o_ref[:, col0 + lo:col0 + lo + half] = (
                    (x1 * cos - x2 * sin) * scale).astype(BF16)
                o_ref[:, col0 + lo + half:col0 + lo + 2 * half] = (
                    (x1 * sin + x2 * cos) * scale).astype(BF16)
        elif col0 < 2 * q_cols + v_cols:
            o_ref[:, col0:col0 + PROJ_TN] = a.astype(BF16)
        else:
            o_ref[:, col0:col0 + PROJ_TN] = (a / (1.0 + jnp.exp(-a))).astype(BF16)


def _proj_in(x2d, gain, w, cos, sin, seq):
    t, d = x2d.shape
    n = w.shape[1]
    tm = PROJ_TM
    assert PROJ_TN % RET_QK_DIM == 0 and (RET_HEADS * RET_QK_DIM) % PROJ_TN == 0
    tiles_per_seq = seq // tm
    vmem = (2 * tm * d * 4 + d * n * 2 + 2 * tm * n * 2 + tm * d * 2
            + 6 * tm * PROJ_TN * 4 + 4 * MIB)
    return pl.pallas_call(
        _proj_in_kernel,
        grid=(t // tm,),
        in_specs=[
            pl.BlockSpec((tm, d), lambda i: (i, 0)),
            pl.BlockSpec((1, d), lambda i: (0, 0)),
            _resident((d, n)),
            pl.BlockSpec((tm, RET_QK_DIM // 2), lambda i: (i % tiles_per_seq, 0)),
            pl.BlockSpec((tm, RET_QK_DIM // 2), lambda i: (i % tiles_per_seq, 0)),
        ],
        out_specs=pl.BlockSpec((tm, n), lambda i: (i, 0)),
        out_shape=jax.ShapeDtypeStruct((t, n), BF16),
        scratch_shapes=[pltpu.VMEM((tm, d), BF16)],
        compiler_params=_params(("parallel",), vmem),
        name="ret_proj_in",
    )(x2d, gain, w, cos, sin)


def _retention_kernel(lg_ref, q_ref, k_ref, v_ref, sg_ref, o_ref, state_ref):
    c_len = RET_CHUNK
    lg = lg_ref[pl.program_id(1)]
    row = lax.broadcasted_iota(jnp.int32, (c_len, c_len), 0)
    col = lax.broadcasted_iota(jnp.int32, (c_len, c_len), 1)
    rel = (row - col).astype(F32)
    decay_mask = jnp.where(rel >= 0, jnp.exp(lg * jnp.maximum(rel, 0.0)), 0.0)
    idx = lax.broadcasted_iota(jnp.int32, (c_len, 1), 0).astype(F32)
    q_decay = jnp.exp(lg * (idx + 1.0))
    k_decay = jnp.exp(lg * (c_len - 1.0 - idx))
    chunk_decay = jnp.exp(lg * jnp.full((1, 1), c_len, F32))

    n_chunks = q_ref.shape[1] // c_len

    def rows(c):
        return slice(c * c_len, (c + 1) * c_len)

    def scores(c):
        return lax.dot_general(q_ref[0, rows(c), :], k_ref[0, rows(c), :],
                               (((1,), (1,)), ((), ())), preferred_element_type=F32)

    s_next = scores(0)
    for c in range(n_chunks):
        q = q_ref[0, rows(c), :]
        k = k_ref[0, rows(c), :]
        v = v_ref[0, rows(c), :]
        s_masked = (s_next * decay_mask).astype(BF16)
        out = None
        if c > 0:
            out = jnp.dot(q, state_ref[...].astype(BF16),
                          preferred_element_type=F32) * q_decay
        if c + 1 < n_chunks:
            s_next = scores(c + 1)
            kd = (k.astype(F32) * k_decay).astype(BF16)
            update = lax.dot_general(kd, v, (((0,), (0,)), ((), ())),
                                     preferred_element_type=F32)
        inner = jnp.dot(s_masked, v, preferred_element_type=F32)
        if c + 1 < n_chunks:
            state_ref[...] = update if c == 0 else state_ref[...] * chunk_decay + update
        out = inner if out is None else inner + out
        gate = sg_ref[0, rows(c), :].astype(F32)
        o_ref[0, rows(c), :] = (out * _rms_scale(out) * gate).astype(BF16)


def _retention(proj3d, log_gamma):
    b, s, _ = proj3d.shape
    dk, dv, h = RET_QK_DIM, RET_V_DIM, RET_HEADS
    k_blk0 = h * dk // dk
    v_blk0 = 2 * h * dk // dv
    g_blk0 = (2 * h * dk + h * dv) // dv
    vmem = 2 * s * (2 * dk + 3 * dv) * 2 + dk * dv * 4 + 8 * MIB
    return pl.pallas_call(
        _retention_kernel,
        grid=(b, h),
        in_specs=[
            pl.BlockSpec(memory_space=pltpu.SMEM),
            pl.BlockSpec((1, s, dk), lambda i, j: (i, 0, j)),
            pl.BlockSpec((1, s, dk), lambda i, j: (i, 0, k_blk0 + j)),
            pl.BlockSpec((1, s, dv), lambda i, j: (i, 0, v_blk0 + j)),
            pl.BlockSpec((1, s, dv), lambda i, j: (i, 0, g_blk0 + j)),
        ],
        out_specs=pl.BlockSpec((1, s, dv), lambda i, j: (i, 0, j)),
        out_shape=jax.ShapeDtypeStruct((b, s, h * dv), BF16),
        scratch_shapes=[pltpu.VMEM((dk, dv), F32)],
        compiler_params=_params(("parallel", "parallel"), vmem),
        name="retention_core",
    )(log_gamma, proj3d, proj3d, proj3d, proj3d)


def _post_kernel(y_ref, w_ref, g_ref, x_ref, o_ref):
    h = jnp.dot(y_ref[...], w_ref[...], preferred_element_type=F32)
    o_ref[...] = x_ref[...] + h * _rms_scale(h) * g_ref[...]


def _post(y2d, w, gain, x2d, name):
    t, kdim = y2d.shape
    d = w.shape[1]
    vmem = (2 * POST_TM * kdim * 2 + kdim * d * 2 + 4 * POST_TM * d * 4
            + 2 * POST_TM * d * 4 + 4 * MIB)
    return pl.pallas_call(
        _post_kernel,
        grid=(t // POST_TM,),
        in_specs=[
            pl.BlockSpec((POST_TM, kdim), lambda i: (i, 0)),
            _resident((kdim, d)),
            pl.BlockSpec((1, d), lambda i: (0, 0)),
            pl.BlockSpec((POST_TM, d), lambda i: (i, 0)),
        ],
        out_specs=pl.BlockSpec((POST_TM, d), lambda i: (i, 0)),
        out_shape=jax.ShapeDtypeStruct((t, d), F32),
        compiler_params=_params(("parallel",), vmem),
        name=name,
    )(y2d, w, gain, x2d)


def _ffn_kernel(x_ref, xh_ref, gpre_ref, wup_ref, cw_ref, wdn_ref, gpost_ref, o_ref,
                xn_ref, h0_ref, h1_ref, act_ref):
    n_tiles, tm, f_dim = act_ref.shape
    tf = h0_ref.shape[2]
    n_f = f_dim // tf
    halo = BF16_ROWS
    gpre = gpre_ref[...]

    def rows(k):
        return slice(k * tm, (k + 1) * tm)

    def normalize(k):
        x = x_ref[0, rows(k), :]
        xn_ref[halo + k * tm:halo + (k + 1) * tm, :] = (x * _rms_scale(x) * gpre).astype(BF16)

    xh = xh_ref[0]
    xhn = jnp.where(pl.program_id(1) > 0, xh * _rms_scale(xh) * gpre, 0.0)
    xn_ref[:halo, :] = xhn.astype(BF16)

    def up_project(k, f, h_ref):
        xn = xn_ref[k * tm:k * tm + halo + tm, :]
        for half in range(2):
            col0 = half * f_dim + f * tf
            h_ref[half] = jnp.dot(xn, wup_ref[:, col0:col0 + tf], preferred_element_type=F32)

    def conv(h_ref, half, f):
        col0 = half * f_dim + f * tf
        cw = cw_ref[:, col0:col0 + tf]
        rows_back = lambda back: pl.ds(halo - back, tm)
        return (cw[3:4] + cw[2:3] * h_ref[half, rows_back(0), :]
                + cw[1:2] * h_ref[half, rows_back(1), :] + cw[0:1] * h_ref[half, rows_back(2), :])

    def gate(k, f, h_ref):
        half_g = conv(h_ref, 0, f)
        u = conv(h_ref, 1, f)
        act_ref[k, :, f * tf:(f + 1) * tf] = (half_g * (1.0 + jnp.tanh(half_g)) * u).astype(BF16)

    def finish(k):
        h = jnp.dot(act_ref[k], wdn_ref[...], preferred_element_type=F32)
        o_ref[0, rows(k), :] = x_ref[0, rows(k), :] + h * _rms_scale(h) * gpost_ref[...]

    h_refs = (h0_ref, h1_ref)
    order = [(k, f) for k in range(n_tiles) for f in range(n_f)]
    normalize(0)
    up_project(*order[0], h_refs[0])
    for n, (k, f) in enumerate(order):
        if n + 1 < len(order):
            up_project(*order[n + 1], h_refs[(n + 1) % 2])
        if f == 1 and k + 1 < n_tiles:
            normalize(k + 1)
        gate(k, f, h_refs[n % 2])
        if f == n_f - 1:
            finish(k)


def _conv_ffn(x3d, gpre, w_up, conv_w, conv_b, w_down, gpost, name):
    b, s, d = x3d.shape
    f_dim = w_down.shape[0]
    assert f_dim % FFN_TF == 0
    tm, halo, n_tiles = FFN_TM, BF16_ROWS, FFN_TILES
    rows = n_tiles * tm
    wup = w_up.astype(BF16)
    wdn = w_down.astype(BF16)
    cw = jnp.concatenate([conv_w, conv_b[None, :], jnp.zeros((4, 2 * f_dim), F32)], axis=0)
    cw = cw * jnp.where(jnp.arange(2 * f_dim) < f_dim, 0.5, 1.0)[None, :]
    halo_blocks = rows // halo
    vmem = (2 * f_dim * d * 2 + f_dim * d * 2 + 4 * rows * d * 4 + (rows + halo) * d * 2
            + rows * f_dim * 2 + 4 * (tm + halo) * FFN_TF * 4 + 6 * tm * d * 4 + 4 * MIB)
    return pl.pallas_call(
        _ffn_kernel,
        grid=(b, s // rows),
        in_specs=[
            pl.BlockSpec((1, rows, d), lambda i, j: (i, j, 0)),
            pl.BlockSpec((1, halo, d), lambda i, j: (i, jnp.maximum(j * halo_blocks - 1, 0), 0)),
            pl.BlockSpec((1, d), lambda i, j: (0, 0)),
            _resident((d, 2 * f_dim)),
            _resident((8, 2 * f_dim)),
            _resident((f_dim, d)),
            pl.BlockSpec((1, d), lambda i, j: (0, 0)),
        ],
        out_specs=pl.BlockSpec((1, rows, d), lambda i, j: (i, j, 0)),
        out_shape=jax.ShapeDtypeStruct((b, s, d), F32),
        scratch_shapes=[
            pltpu.VMEM((rows + halo, d), BF16),
            pltpu.VMEM((2, tm + halo, FFN_TF), F32),
            pltpu.VMEM((2, tm + halo, FFN_TF), F32),
            pltpu.VMEM((n_tiles, tm, f_dim), BF16),
        ],
        compiler_params=_params(("parallel", "parallel"), vmem),
        name=name,
    )(x3d, x3d, gpre, wup, cw, wdn, gpost)


def _kvq_kernel(x_ref, gkv_ref, gq_ref, wkv_ref, wq_ref, kv_ref, q_ref):
    x = x_ref[...]
    xr = x * _rms_scale(x)
    kv = jnp.dot((xr * gkv_ref[...]).astype(BF16), wkv_ref[...], preferred_element_type=F32)
    kv_ref[...] = kv.astype(BF16)
    q = jnp.dot((xr * gq_ref[...]).astype(BF16), wq_ref[...], preferred_element_type=F32)
    q_ref[...] = (q * (DIFF_HEAD_DIM ** -0.5 * LOG2_E)).astype(BF16)


def _kvq(x2d, g_kv, g_q, w_kv, w_q):
    t, d = x2d.shape
    n_kv, n_q = w_kv.shape[1], w_q.shape[1]
    tm = KVQ_TM
    vmem = (2 * tm * d * 4 + d * (n_kv + n_q) * 2 + 2 * tm * (n_kv + n_q) * 2
            + 6 * tm * d * 4 + 2 * tm * (n_kv + n_q) * 4 + 4 * MIB)
    return pl.pallas_call(
        _kvq_kernel,
        grid=(t // tm,),
        in_specs=[
            pl.BlockSpec((tm, d), lambda i: (i, 0)),
            pl.BlockSpec((1, d), lambda i: (0, 0)),
            pl.BlockSpec((1, d), lambda i: (0, 0)),
            _resident((d, n_kv)),
            _resident((d, n_q)),
        ],
        out_specs=[pl.BlockSpec((tm, n_kv), lambda i: (i, 0)),
                   pl.BlockSpec((tm, n_q), lambda i: (i, 0))],
        out_shape=[jax.ShapeDtypeStruct((t, n_kv), BF16),
                   jax.ShapeDtypeStruct((t, n_q), BF16)],
        compiler_params=_params(("parallel",), vmem),
        name="kv_q_proj",
    )(x2d, g_kv, g_q, w_kv, w_q)


def _diff_attn_kernel(q_ref, k_ref, v_ref, lam_ref, gsub_ref, o_ref,
                      q2t_ref, vt_ref, s_ref, p_ref, a_ref, m_ref, acc_ref, *, lambda_init):
    tq, tk, d, dv = ATT_TQ, ATT_TK, DIFF_HEAD_DIM, DIFF_V_DIM
    strip = LANES
    n_q = q2t_ref.shape[1]
    n_kv = vt_ref.shape[1]
    n_pieces = 2 * tq // MXU_COLS
    strips_per_piece = MXU_COLS // strip

    def head(h):
        return slice(h * dv, (h + 1) * dv)

    for h in range(ATT_HEADS):
        for j in range(n_kv):
            vt_ref[h, j, :dv, :] = (
                v_ref[0, j * tk:(j + 1) * tk, head(h)].astype(F32).T.astype(BF16))
    vt_ref[:, :, dv:, :] = jnp.ones((ATT_HEADS, n_kv, BF16_ROWS, tk), BF16)

    for h in range(ATT_HEADS):
        for i in range(n_q):
            qt = q_ref[0, i * tq:(i + 1) * tq, head(h)].astype(F32).T
            feat = lax.broadcasted_iota(jnp.int32, qt.shape, 0)
            q2t_ref[h, i, :, :tq] = jnp.where(feat < d, qt, 0.0).astype(BF16)
            q2t_ref[h, i, :, tq:] = jnp.where(feat >= d, qt, 0.0).astype(BF16)

    lp = lam_ref[...]
    lam = (jnp.exp(jnp.sum(lp[0:1] * lp[1:2], axis=-1, keepdims=True))
           - jnp.exp(jnp.sum(lp[2:3] * lp[3:4], axis=-1, keepdims=True)) + lambda_init)

    tiles = [(i, j) for i in range(n_q) for j in range((i + 1) * tq // tk)]

    def piece(r):
        return slice(r * MXU_COLS, (r + 1) * MXU_COLS)

    def visibility(i, j, c):
        q_first = i * tq + (c * strip) % tq
        k_first = j * tk
        if k_first > q_first + strip - 1:
            return "none"
        return "some" if k_first + tk - 1 > q_first else "all"

    def piece_visible(i, j, r):
        seen = [visibility(i, j, c) != "none"
                for c in range(r * strips_per_piece, (r + 1) * strips_per_piece)]
        assert all(seen) or not any(seen)
        return seen[0]

    def scores(h, t, r):
        i, j = tiles[t]
        if not piece_visible(i, j, r):
            return
        res = jnp.dot(k_ref[0, j * tk:(j + 1) * tk, head(h)], q2t_ref[h, i, :, piece(r)],
                      preferred_element_type=F32)
        for c in range(strips_per_piece):
            s_ref[h, t % 2, r * strips_per_piece + c] = res[:, c * strip:(c + 1) * strip]

    def softmax(h, t, r):
        i, j = tiles[t]
        if not piece_visible(i, j, r):
            return
        for c in range(r * strips_per_piece, (r + 1) * strips_per_piece):
            lanes = slice(c * strip, (c + 1) * strip)
            s = s_ref[h, t % 2, c]
            if visibility(i, j, c) == "some":
                rel = (lax.broadcasted_iota(jnp.int32, (tk, strip), 1)
                       - lax.broadcasted_iota(jnp.int32, (tk, strip), 0))
                s = jnp.where(rel >= j * tk - i * tq - (c * strip) % tq, s, MASK_VALUE)
            m_new = jnp.max(s, axis=0, keepdims=True)
            if j > 0:
                m_old = m_ref[h, i, :, lanes]
                m_new = jnp.maximum(m_old, m_new)
                a_ref[h, t % 2, :, lanes] = jnp.exp2(m_old - m_new)
            m_ref[h, i, :, lanes] = m_new
            p_ref[h, t % 2, c] = jnp.exp2(s - m_new).astype(BF16)

    def accumulate(h, t, r):
        i, j = tiles[t]
        if not piece_visible(i, j, r):
            return
        p = jnp.concatenate([p_ref[h, t % 2, r * strips_per_piece + c]
                             for c in range(strips_per_piece)], axis=1)
        pv = jnp.dot(vt_ref[h, j], p, preferred_element_type=F32)
        if j > 0:
            pv = a_ref[h, t % 2, :, piece(r)] * acc_ref[h, i, :, piece(r)] + pv
        acc_ref[h, i, :, piece(r)] = pv

    def finalize(i):
        for h in range(ATT_HEADS):
            inv_l = 1.0 / acc_ref[h, i, dv:dv + 1, :]
            o = (acc_ref[h, i, :dv, :tq] * inv_l[:, :tq]
                 - lam * (acc_ref[h, i, :dv, tq:] * inv_l[:, tq:]))
            r = lax.rsqrt(jnp.mean(o * o, axis=0, keepdims=True) + EPS)
            o = o * r * gsub_ref[...] * (1.0 - lambda_init)
            o_ref[0, i * tq:(i + 1) * tq, head(h)] = o.T.astype(BF16)

    heads = range(ATT_HEADS)
    for r in range(n_pieces):
        for h in heads:
            scores(h, 0, r)
    for t in range(len(tiles)):
        for r in range(n_pieces):
            for h in heads:
                if t + 1 < len(tiles):
                    scores(h, t + 1, r)
                softmax(h, t, r)
                if t > 0:
                    accumulate(h, t - 1, r)
        if t > 0 and tiles[t][0] != tiles[t - 1][0]:
            finalize(tiles[t - 1][0])
    for r in range(n_pieces):
        for h in heads:
            accumulate(h, len(tiles) - 1, r)
    finalize(n_q - 1)


def _diff_attention(q3d, kv3d, lam_params, g_sub_col, lambda_init):
    b, s, _ = q3d.shape
    h, dv = DIFF_HEADS, DIFF_V_DIM
    tq, tk = ATT_TQ, ATT_TK
    assert tq % tk == 0 and s % tq == 0 and h % ATT_HEADS == 0
    hps = ATT_HEADS
    dv_aug = dv + BF16_ROWS
    n_q = s // tq
    strips = (hps, 2, 2 * tq // LANES, tk, LANES)
    vmem = hps * (2 * 4 * s * dv * 2 + 2 * s * dv * 2 + s * dv_aug * 2
                  + 2 * tk * 2 * tq * (4 + 2) + n_q * dv_aug * 2 * tq * 4
                  + 4 * tk * 2 * tq * 4) + 8 * MIB

    def head_block(first):
        return pl.BlockSpec((1, s, hps * dv), lambda bi, hi: (bi, 0, first // hps + hi))

    return pl.pallas_call(
        functools.partial(_diff_attn_kernel, lambda_init=lambda_init),
        grid=(b, h // hps),
        in_specs=[
            head_block(0),
            head_block(0),
            head_block(h),
            pl.BlockSpec(lam_params.shape, lambda bi, hi: (0, 0)),
            pl.BlockSpec((dv, 1), lambda bi, hi: (0, 0)),
        ],
        out_specs=head_block(0),
        out_shape=jax.ShapeDtypeStruct((b, s, h * dv), BF16),
        scratch_shapes=[
            pltpu.VMEM((hps, n_q, dv, 2 * tq), BF16),
            pltpu.VMEM((hps, s // tk, dv_aug, tk), BF16),
            pltpu.VMEM(strips, F32),
            pltpu.VMEM(strips, BF16),
            pltpu.VMEM((hps, 2, 1, 2 * tq), F32),
            pltpu.VMEM((hps, n_q, 1, 2 * tq), F32),
            pltpu.VMEM((hps, n_q, dv_aug, 2 * tq), F32),
        ],
        compiler_params=_params(("parallel", "parallel"), vmem),
        name="diff_attention",
    )(q3d, kv3d, kv3d, lam_params, g_sub_col)


def kernel(x, a_norm_pre, a_norm_post, a_w_in, a_w_out, kv_norm, w_kv, b_norm_pre, b_norm_post,
           b_w_q, b_lambda, b_subln, b_w_out, ffn_norm_pre, ffn_norm_post, ffn_w_up, ffn_conv_w,
           ffn_conv_b, ffn_w_down):
    b, s, d = x.shape
    t = b * s

    half = RET_QK_DIM // 2
    inv = 1.0 / (ROPE_BASE ** jnp.linspace(0.0, 1.0, half, dtype=F32))
    ang = jnp.arange(s).astype(F32)[:, None] * inv[None, :]
    cos, sin = jnp.cos(ang), jnp.sin(ang)
    log_gamma = jnp.log1p(-jnp.power(2.0, -5.0 - jnp.arange(RET_HEADS, dtype=F32)))

    proj = _proj_in(x.reshape(t, d), a_norm_pre[0][None], a_w_in[0].astype(BF16), cos, sin, s)
    y = _retention(proj.reshape(b, s, -1), log_gamma)
    x = _post(y.reshape(t, -1), a_w_out[0].astype(BF16), a_norm_post[0][None],
              x.reshape(t, d), "ret_out_proj")
    x = _conv_ffn(x.reshape(b, s, d), ffn_norm_pre[0][None], ffn_w_up[0], ffn_conv_w[0],
                  ffn_conv_b[0], ffn_w_down[0], ffn_norm_post[0][None], "conv_ffn_0")

    layer = 1
    lambda_init = 0.8 - 0.6 * math.exp(-0.3 * layer)
    kv, q = _kvq(x.reshape(t, d), kv_norm[None], b_norm_pre[0][None],
                 w_kv.astype(BF16), b_w_q[0].astype(BF16))
    o = _diff_attention(q.reshape(b, s, -1), kv.reshape(b, s, -1), b_lambda[0],
                        b_subln[0][:, None], lambda_init)
    x = _post(o.reshape(t, -1), b_w_out[0].astype(BF16), b_norm_post[0][None],
              x.reshape(t, d), "attn_out_proj")
    x = _conv_ffn(x.reshape(b, s, d), ffn_norm_pre[1][None], ffn_w_up[1], ffn_conv_w[1],
                  ffn_conv_b[1], ffn_w_down[1], ffn_norm_post[1][None], "conv_ffn_1")
    return x
```

```python
import functools
import math

import jax
import jax.numpy as jnp
from jax import lax
from jax.experimental import pallas as pl
from jax.experimental.pallas import tpu as pltpu

F32 = jnp.float32
BF16 = jnp.bfloat16

D_MODEL = 1024
RET_HEADS = 4
RET_QK_DIM = D_MODEL // RET_HEADS
RET_V_DIM = 2 * RET_QK_DIM
RET_CHUNK = 256
DIFF_HEAD_DIM = 64
DIFF_HEADS = D_MODEL // (2 * DIFF_HEAD_DIM)
DIFF_V_DIM = 2 * DIFF_HEAD_DIM
D_FF = ((8 * D_MODEL // 3 + 127) // 128) * 128
ROPE_BASE = 10000.0
EPS = 1e-6
MASK_VALUE = -1e30
LOG2_E = math.log2(math.e)

BF16_ROWS = 16
LANES = 128
MXU_COLS = 256
MIB = 1024 * 1024

RET_PROJ_ROWS = 512
POST_TM = 1024
KVQ_TM = 1024
FFN_TM = 512
FFN_TILES = 1
FFN_TF = 256
ATT_TQ = 512
ATT_TK = 256
ATT_HEADS = 1


def _params(semantics, vmem_bytes):
    return pltpu.CompilerParams(dimension_semantics=semantics,
                                vmem_limit_bytes=int(vmem_bytes))


def _resident(shape):
    zeros = (0,) * len(shape)
    return pl.BlockSpec(shape, lambda *_: zeros, pipeline_mode=pl.Buffered(1))


def _rms_scale(v):
    return lax.rsqrt(jnp.mean(v * v, axis=-1, keepdims=True) + EPS)


def _retention_kernel(lg_ref, x_ref, gain_ref, wq_ref, wk_ref, wv_ref, wg_ref, cos_ref, sin_ref,
                      o_ref, xn_ref, q_ref, k_ref, v_ref, sg_ref, state_ref):
    c_len, p_len = RET_CHUNK, RET_PROJ_ROWS
    seq = x_ref.shape[1]
    n_chunks = seq // c_len
    half = RET_QK_DIM // 2

    @pl.when(pl.program_id(1) == 0)
    def _():
        for r0 in range(0, seq, p_len):
            x = x_ref[0, r0:r0 + p_len, :]
            xn_ref[r0:r0 + p_len, :] = (x * _rms_scale(x) * gain_ref[...]).astype(BF16)

    lg = lg_ref[pl.program_id(1)]
    row = lax.broadcasted_iota(jnp.int32, (c_len, c_len), 0)
    col = lax.broadcasted_iota(jnp.int32, (c_len, c_len), 1)
    rel = (row - col).astype(F32)
    decay_mask = jnp.where(rel >= 0, jnp.exp(lg * jnp.maximum(rel, 0.0)), 0.0)
    idx = lax.broadcasted_iota(jnp.int32, (c_len, 1), 0).astype(F32)
    q_decay = jnp.exp(lg * (idx + 1.0))
    k_decay = jnp.exp(lg * (c_len - 1.0 - idx))
    chunk_decay = jnp.exp(lg * jnp.full((1, 1), c_len, F32))

    def project(r0):
        rows = slice(r0, r0 + p_len)
        xn = xn_ref[rows, :]
        cos = cos_ref[rows, :]
        sin = sin_ref[rows, :]

        def rotary(a, scale, dst_ref):
            x1, x2 = a[:, :half], a[:, half:]
            dst_ref[rows, :half] = ((x1 * cos - x2 * sin) * scale).astype(BF16)
            dst_ref[rows, half:] = ((x1 * sin + x2 * cos) * scale).astype(BF16)

        rotary(jnp.dot(xn, wq_ref[...], preferred_element_type=F32), 1.0, q_ref)
        rotary(jnp.dot(xn, wk_ref[...], preferred_element_type=F32), RET_QK_DIM ** -0.5, k_ref)
        v_ref[rows, :] = jnp.dot(xn, wv_ref[...], preferred_element_type=F32).astype(BF16)
        half_g = 0.5 * jnp.dot(xn, wg_ref[...], preferred_element_type=F32)
        sg_ref[rows, :] = (half_g * (1.0 + jnp.tanh(half_g))).astype(BF16)

    def rows(c):
        return slice(c * c_len, (c + 1) * c_len)

    def scores(c):
        return lax.dot_general(q_ref[rows(c), :], k_ref[rows(c), :],
                               (((1,), (1,)), ((), ())), preferred_element_type=F32)

    chunks_per_tile = p_len // c_len
    project(0)
    s_next = scores(0)
    for c in range(n_chunks):
        next_tile = (c // chunks_per_tile + 1) * p_len
        if c % chunks_per_tile == 0 and next_tile < seq:
            project(next_tile)
        q = q_ref[rows(c), :]
        k = k_ref[rows(c), :]
        v = v_ref[rows(c), :]
        s_masked = (s_next * decay_mask).astype(BF16)
        out = None
        if c > 0:
            out = jnp.dot(q, state_ref[...].astype(BF16),
                          preferred_element_type=F32) * q_decay
        if c + 1 < n_chunks:
            s_next = scores(c + 1)
            kd = (k.astype(F32) * k_decay).astype(BF16)
            update = lax.dot_general(kd, v, (((0,), (0,)), ((), ())),
                                     preferred_element_type=F32)
        inner = jnp.dot(s_masked, v, preferred_element_type=F32)
        if c + 1 < n_chunks:
            state_ref[...] = update if c == 0 else state_ref[...] * chunk_decay + update
        out = inner if out is None else inner + out
        gate = sg_ref[rows(c), :].astype(F32)
        o_ref[0, rows(c), :] = (out * _rms_scale(out) * gate).astype(BF16)


def _retention(x3d, gain, w_in, cos, sin, log_gamma):
    b, s, d = x3d.shape
    dk, dv, h = RET_QK_DIM, RET_V_DIM, RET_HEADS
    assert RET_PROJ_ROWS % RET_CHUNK == 0 and s % RET_PROJ_ROWS == 0
    k_blk0 = h * dk // dk
    v_blk0 = 2 * h * dk // dv
    g_blk0 = (2 * h * dk + h * dv) // dv
    vmem = (2 * s * d * 4 + 2 * d * (2 * dk + 2 * dv) * 2 + 2 * s * dk * 4 + 2 * s * dv * 2
            + s * d * 2 + s * (2 * dk + 2 * dv) * 2 + dk * dv * 4
            + 8 * RET_PROJ_ROWS * dv * 4 + 4 * MIB)
    return pl.pallas_call(
        _retention_kernel,
        grid=(b, h),
        in_specs=[
            pl.BlockSpec(memory_space=pltpu.SMEM),
            pl.BlockSpec((1, s, d), lambda i, j: (i, 0, 0)),
            pl.BlockSpec((1, d), lambda i, j: (0, 0)),
            pl.BlockSpec((d, dk), lambda i, j: (0, j)),
            pl.BlockSpec((d, dk), lambda i, j: (0, k_blk0 + j)),
            pl.BlockSpec((d, dv), lambda i, j: (0, v_blk0 + j)),
            pl.BlockSpec((d, dv), lambda i, j: (0, g_blk0 + j)),
            _resident((s, dk // 2)),
            _resident((s, dk // 2)),
        ],
        out_specs=pl.BlockSpec((1, s, dv), lambda i, j: (i, 0, j)),
        out_shape=jax.ShapeDtypeStruct((b, s, h * dv), BF16),
        scratch_shapes=[
            pltpu.VMEM((s, d), BF16),
            pltpu.VMEM((s, dk), BF16),
            pltpu.VMEM((s, dk), BF16),
            pltpu.VMEM((s, dv), BF16),
            pltpu.VMEM((s, dv), BF16),
            pltpu.VMEM((dk, dv), F32),
        ],
        compiler_params=_params(("parallel", "arbitrary"), vmem),
        name="retention",
    )(log_gamma, x3d, gain, w_in, w_in, w_in, w_in, cos, sin)


def _post_kernel(y_ref, w_ref, g_ref, x_ref, o_ref):
    h = jnp.dot(y_ref[...], w_ref[...], preferred_element_type=F32)
    o_ref[...] = x_ref[...] + h * _rms_scale(h) * g_ref[...]


def _post(y2d, w, gain, x2d, name):
    t, kdim = y2d.shape
    d = w.shape[1]
    vmem = (2 * POST_TM * kdim * 2 + kdim * d * 2 + 4 * POST_TM * d * 4
            + 2 * POST_TM * d * 4 + 4 * MIB)
    return pl.pallas_call(
        _post_kernel,
        grid=(t // POST_TM,),
        in_specs=[
            pl.BlockSpec((POST_TM, kdim), lambda i: (i, 0)),
            _resident((kdim, d)),
            pl.BlockSpec((1, d), lambda i: (0, 0)),
            pl.BlockSpec((POST_TM, d), lambda i: (i, 0)),
        ],
        out_specs=pl.BlockSpec((POST_TM, d), lambda i: (i, 0)),
        out_shape=jax.ShapeDtypeStruct((t, d), F32),
        compiler_params=_params(("parallel",), vmem),
        name=name,
    )(y2d, w, gain, x2d)


def _ffn_kernel(x_ref, xh_ref, gpre_ref, wup_ref, cw_ref, wdn_ref, gpost_ref, o_ref,
                xn_ref, h0_ref, h1_ref, act_ref):
    n_tiles, tm, f_dim = act_ref.shape
    tf = h0_ref.shape[2]
    n_f = f_dim // tf
    halo = BF16_ROWS
    gpre = gpre_ref[...]

    def rows(k):
        return slice(k * tm, (k + 1) * tm)

    def normalize(k):
        x = x_ref[0, rows(k), :]
        xn_ref[halo + k * tm:halo + (k + 1) * tm, :] = (x * _rms_scale(x) * gpre).astype(BF16)

    xh = xh_ref[0]
    xhn = jnp.where(pl.program_id(1) > 0, xh * _rms_scale(xh) * gpre, 0.0)
    xn_ref[:halo, :] = xhn.astype(BF16)

    def up_project(k, f, h_ref):
        xn = xn_ref[k * tm:k * tm + halo + tm, :]
        for half in range(2):
            col0 = half * f_dim + f * tf
            h_ref[half] = jnp.dot(xn, wup_ref[:, col0:col0 + tf], preferred_element_type=F32)

    def conv(h_ref, half, f):
        col0 = half * f_dim + f * tf
        cw = cw_ref[:, col0:col0 + tf]
        rows_back = lambda back: pl.ds(halo - back, tm)
        return (cw[3:4] + cw[2:3] * h_ref[half, rows_back(0), :]
                + cw[1:2] * h_ref[half, rows_back(1), :] + cw[0:1] * h_ref[half, rows_back(2), :])

    def gate(k, f, h_ref):
        half_g = conv(h_ref, 0, f)
        u = conv(h_ref, 1, f)
        act_ref[k, :, f * tf:(f + 1) * tf] = (half_g * (1.0 + jnp.tanh(half_g)) * u).astype(BF16)

    def finish(k):
        h = jnp.dot(act_ref[k], wdn_ref[...], preferred_element_type=F32)
        o_ref[0, rows(k), :] = x_ref[0, rows(k), :] + h * _rms_scale(h) * gpost_ref[...]

    h_refs = (h0_ref, h1_ref)
    order = [(k, f) for k in range(n_tiles) for f in range(n_f)]
    normalize(0)
    up_project(*order[0], h_refs[0])
    for n, (k, f) in enumerate(order):
        if n + 1 < len(order):
            up_project(*order[n + 1], h_refs[(n + 1) % 2])
        if f == 1 and k + 1 < n_tiles:
            normalize(k + 1)
        gate(k, f, h_refs[n % 2])
        if f == n_f - 1:
            finish(k)


def _conv_ffn(x3d, gpre, w_up, conv_w, conv_b, w_down, gpost, name):
    b, s, d = x3d.shape
    f_dim = w_down.shape[0]
    assert f_dim % FFN_TF == 0
    tm, halo, n_tiles = FFN_TM, BF16_ROWS, FFN_TILES
    rows = n_tiles * tm
    wup = w_up.astype(BF16)
    wdn = w_down.astype(BF16)
    cw = jnp.concatenate([conv_w, conv_b[None, :], jnp.zeros((4, 2 * f_dim), F32)], axis=0)
    cw = cw * jnp.where(jnp.arange(2 * f_dim) < f_dim, 0.5, 1.0)[None, :]
    halo_blocks = rows // halo
    vmem = (2 * f_dim * d * 2 + f_dim * d * 2 + 4 * rows * d * 4 + (rows + halo) * d * 2
            + rows * f_dim * 2 + 4 * (tm + halo) * FFN_TF * 4 + 6 * tm * d * 4 + 4 * MIB)
    return pl.pallas_call(
        _ffn_kernel,
        grid=(b, s // rows),
        in_specs=[
            pl.BlockSpec((1, rows, d), lambda i, j: (i, j, 0)),
            pl.BlockSpec((1, halo, d), lambda i, j: (i, jnp.maximum(j * halo_blocks - 1, 0), 0)),
            pl.BlockSpec((1, d), lambda i, j: (0, 0)),
            _resident((d, 2 * f_dim)),
            _resident((8, 2 * f_dim)),
            _resident((f_dim, d)),
            pl.BlockSpec((1, d), lambda i, j: (0, 0)),
        ],
        out_specs=pl.BlockSpec((1, rows, d), lambda i, j: (i, j, 0)),
        out_shape=jax.ShapeDtypeStruct((b, s, d), F32),
        scratch_shapes=[
            pltpu.VMEM((rows + halo, d), BF16),
            pltpu.VMEM((2, tm + halo, FFN_TF), F32),
            pltpu.VMEM((2, tm + halo, FFN_TF), F32),
            pltpu.VMEM((n_tiles, tm, f_dim), BF16),
        ],
        compiler_params=_params(("parallel", "parallel"), vmem),
        name=name,
    )(x3d, x3d, gpre, wup, cw, wdn, gpost)


def _kvq_kernel(x_ref, gkv_ref, gq_ref, wkv_ref, wq_ref, kv_ref, q_ref):
    x = x_ref[...]
    xr = x * _rms_scale(x)
    kv = jnp.dot((xr * gkv_ref[...]).astype(BF16), wkv_ref[...], preferred_element_type=F32)
    kv_ref[...] = kv.astype(BF16)
    q = jnp.dot((xr * gq_ref[...]).astype(BF16), wq_ref[...], preferred_element_type=F32)
    q_ref[...] = (q * (DIFF_HEAD_DIM ** -0.5 * LOG2_E)).astype(BF16)


def _kvq(x2d, g_kv, g_q, w_kv, w_q):
    t, d = x2d.shape
    n_kv, n_q = w_kv.shape[1], w_q.shape[1]
    tm = KVQ_TM
    vmem = (2 * tm * d * 4 + d * (n_kv + n_q) * 2 + 2 * tm * (n_kv + n_q) * 2
            + 6 * tm * d * 4 + 2 * tm * (n_kv + n_q) * 4 + 4 * MIB)
    return pl.pallas_call(
        _kvq_kernel,
        grid=(t // tm,),
        in_specs=[
            pl.BlockSpec((tm, d), lambda i: (i, 0)),
            pl.BlockSpec((1, d), lambda i: (0, 0)),
            pl.BlockSpec((1, d), lambda i: (0, 0)),
            _resident((d, n_kv)),
            _resident((d, n_q)),
        ],
        out_specs=[pl.BlockSpec((tm, n_kv), lambda i: (i, 0)),
                   pl.BlockSpec((tm, n_q), lambda i: (i, 0))],
        out_shape=[jax.ShapeDtypeStruct((t, n_kv), BF16),
                   jax.ShapeDtypeStruct((t, n_q), BF16)],
        compiler_params=_params(("parallel",), vmem),
        name="kv_q_proj",
    )(x2d, g_kv, g_q, w_kv, w_q)


def _diff_attn_kernel(q_ref, k_ref, v_ref, lam_ref, gsub_ref, o_ref,
                      q2t_ref, vt_ref, s_ref, p_ref, a_ref, m_ref, acc_ref, *, lambda_init):
    tq, tk, d, dv = ATT_TQ, ATT_TK, DIFF_HEAD_DIM, DIFF_V_DIM
    strip = LANES
    n_q = q2t_ref.shape[1]
    n_kv = vt_ref.shape[1]
    n_pieces = 2 * tq // MXU_COLS
    strips_per_piece = MXU_COLS // strip

    def head(h):
        return slice(h * dv, (h + 1) * dv)

    for h in range(ATT_HEADS):
        for j in range(n_kv):
            vt_ref[h, j, :dv, :] = (
                v_ref[0, j * tk:(j + 1) * tk, head(h)].astype(F32).T.astype(BF16))
    vt_ref[:, :, dv:, :] = jnp.ones((ATT_HEADS, n_kv, BF16_ROWS, tk), BF16)

    for h in range(ATT_HEADS):
        for i in range(n_q):
            qt = q_ref[0, i * tq:(i + 1) * tq, head(h)].astype(F32).T
            feat = lax.broadcasted_iota(jnp.int32, qt.shape, 0)
            q2t_ref[h, i, :, :tq] = jnp.where(feat < d, qt, 0.0).astype(BF16)
            q2t_ref[h, i, :, tq:] = jnp.where(feat >= d, qt, 0.0).astype(BF16)

    lp = lam_ref[...]
    lam = (jnp.exp(jnp.sum(lp[0:1] * lp[1:2], axis=-1, keepdims=True))
           - jnp.exp(jnp.sum(lp[2:3] * lp[3:4], axis=-1, keepdims=True)) + lambda_init)

    tiles = [(i, j) for i in range(n_q) for j in range((i + 1) * tq // tk)]

    def piece(r):
        return slice(r * MXU_COLS, (r + 1) * MXU_COLS)

    def visibility(i, j, c):
        q_first = i * tq + (c * strip) % tq
        k_first = j * tk
        if k_first > q_first + strip - 1:
            return "none"
        return "some" if k_first + tk - 1 > q_first else "all"

    def piece_visible(i, j, r):
        seen = [visibility(i, j, c) != "none"
                for c in range(r * strips_per_piece, (r + 1) * strips_per_piece)]
        assert all(seen) or not any(seen)
        return seen[0]

    def scores(h, t, r):
        i, j = tiles[t]
        if not piece_visible(i, j, r):
            return
        res = jnp.dot(k_ref[0, j * tk:(j + 1) * tk, head(h)], q2t_ref[h, i, :, piece(r)],
                      preferred_element_type=F32)
        for c in range(strips_per_piece):
            s_ref[h, t % 2, r * strips_per_piece + c] = res[:, c * strip:(c + 1) * strip]

    def softmax(h, t, r):
        i, j = tiles[t]
        if not piece_visible(i, j, r):
            return
        for c in range(r * strips_per_piece, (r + 1) * strips_per_piece):
            lanes = slice(c * strip, (c + 1) * strip)
            s = s_ref[h, t % 2, c]
            if visibility(i, j, c) == "some":
                rel = (lax.broadcasted_iota(jnp.int32, (tk, strip), 1)
                       - lax.broadcasted_iota(jnp.int32, (tk, strip), 0))
                s = jnp.where(rel >= j * tk - i * tq - (c * strip) % tq, s, MASK_VALUE)
            m_new = jnp.max(s, axis=0, keepdims=True)
            if j > 0:
                m_old = m_ref[h, i, :, lanes]
                m_new = jnp.maximum(m_old, m_new)
                a_ref[h, t % 2, :, lanes] = jnp.exp2(m_old - m_new)
            m_ref[h, i, :, lanes] = m_new
            p_ref[h, t % 2, c] = jnp.exp2(s - m_new).astype(BF16)

    def accumulate(h, t, r):
        i, j = tiles[t]
        if not piece_visible(i, j, r):
            return
        p = jnp.concatenate([p_ref[h, t % 2, r * strips_per_piece + c]
                             for c in range(strips_per_piece)], axis=1)
        pv = jnp.dot(vt_ref[h, j], p, preferred_element_type=F32)
        if j > 0:
            pv = a_ref[h, t % 2, :, piece(r)] * acc_ref[h, i, :, piece(r)] + pv
        acc_ref[h, i, :, piece(r)] = pv

    def finalize(i):
        for h in range(ATT_HEADS):
            inv_l = 1.0 / acc_ref[h, i, dv:dv + 1, :]
            o = (acc_ref[h, i, :dv, :tq] * inv_l[:, :tq]
                 - lam * (acc_ref[h, i, :dv, tq:] * inv_l[:, tq:]))
            r = lax.rsqrt(jnp.mean(o * o, axis=0, keepdims=True) + EPS)
            o = o * r * gsub_ref[...] * (1.0 - lambda_init)
            o_ref[0, i * tq:(i + 1) * tq, head(h)] = o.T.astype(BF16)

    heads = range(ATT_HEADS)
    for r in range(n_pieces):
        for h in heads:
            scores(h, 0, r)
    for t in range(len(tiles)):
        for r in range(n_pieces):
            for h in heads:
                if t + 1 < len(tiles):
                    scores(h, t + 1, r)
                softmax(h, t, r)
                if t > 0:
                    accumulate(h, t - 1, r)
        if t > 0 and tiles[t][0] != tiles[t - 1][0]:
            finalize(tiles[t - 1][0])
    for r in range(n_pieces):
        for h in heads:
            accumulate(h, len(tiles) - 1, r)
    finalize(n_q - 1)


def _diff_attention(q3d, kv3d, lam_params, g_sub_col, lambda_init):
    b, s, _ = q3d.shape
    h, dv = DIFF_HEADS, DIFF_V_DIM
    tq, tk = ATT_TQ, ATT_TK
    assert tq % tk == 0 and s % tq == 0 and h % ATT_HEADS == 0
    hps = ATT_HEADS
    dv_aug = dv + BF16_ROWS
    n_q = s // tq
    strips = (hps, 2, 2 * tq // LANES, tk, LANES)
    vmem = hps * (2 * 4 * s * dv * 2 + 2 * s * dv * 2 + s * dv_aug * 2
                  + 2 * tk * 2 * tq * (4 + 2) + n_q * dv_aug * 2 * tq * 4
                  + 4 * tk * 2 * tq * 4) + 8 * MIB

    def head_block(first):
        return pl.BlockSpec((1, s, hps * dv), lambda bi, hi: (bi, 0, first // hps + hi))

    return pl.pallas_call(
        functools.partial(_diff_attn_kernel, lambda_init=lambda_init),
        grid=(b, h // hps),
        in_specs=[
            head_block(0),
            head_block(0),
            head_block(h),
            pl.BlockSpec(lam_params.shape, lambda bi, hi: (0, 0)),
            pl.BlockSpec((dv, 1), lambda bi, hi: (0, 0)),
        ],
        out_specs=head_block(0),
        out_shape=jax.ShapeDtypeStruct((b, s, h * dv), BF16),
        scratch_shapes=[
            pltpu.VMEM((hps, n_q, dv, 2 * tq), BF16),
            pltpu.VMEM((hps, s // tk, dv_aug, tk), BF16),
            pltpu.VMEM(strips, F32),
            pltpu.VMEM(strips, BF16),
            pltpu.VMEM((hps, 2, 1, 2 * tq), F32),
            pltpu.VMEM((hps, n_q, 1, 2 * tq), F32),
            pltpu.VMEM((hps, n_q, dv_aug, 2 * tq), F32),
        ],
        compiler_params=_params(("parallel", "parallel"), vmem),
        name="diff_attention",
    )(q3d, kv3d, kv3d, lam_params, g_sub_col)


def kernel(x, a_norm_pre, a_norm_post, a_w_in, a_w_out, kv_norm, w_kv, b_norm_pre, b_norm_post,
           b_w_q, b_lambda, b_subln, b_w_out, ffn_norm_pre, ffn_norm_post, ffn_w_up, ffn_conv_w,
           ffn_conv_b, ffn_w_down):
    b, s, d = x.shape
    t = b * s

    half = RET_QK_DIM // 2
    inv = 1.0 / (ROPE_BASE ** jnp.linspace(0.0, 1.0, half, dtype=F32))
    ang = jnp.arange(s).astype(F32)[:, None] * inv[None, :]
    cos, sin = jnp.cos(ang), jnp.sin(ang)
    log_gamma = jnp.log1p(-jnp.power(2.0, -5.0 - jnp.arange(RET_HEADS, dtype=F32)))

    y = _retention(x, a_norm_pre[0][None], a_w_in[0].astype(BF16), cos, sin, log_gamma)
    x = _post(y.reshape(t, -1), a_w_out[0].astype(BF16), a_norm_post[0][None],
              x.reshape(t, d), "ret_out_proj")
    x = _conv_ffn(x.reshape(b, s, d), ffn_norm_pre[0][None], ffn_w_up[0], ffn_conv_w[0],
                  ffn_conv_b[0], ffn_w_down[0], ffn_norm_post[0][None], "conv_ffn_0")

    layer = 1
    lambda_init = 0.8 - 0.6 * math.exp(-0.3 * layer)
    kv, q = _kvq(x.reshape(t, d), kv_norm[None], b_norm_pre[0][None],
                 w_kv.astype(BF16), b_w_q[0].astype(BF16))
    o = _diff_attention(q.reshape(b, s, -1), kv.reshape(b, s, -1), b_lambda[0],
                        b_subln[0][:, None], lambda_init)
    x = _post(o.reshape(t, -1), b_w_out[0].astype(BF16), b_norm_post[0][None],
              x.reshape(t, d), "attn_out_proj")
    x = _conv_ffn(x.reshape(b, s, d), ffn_norm_pre[1][None], ffn_w_up[1], ffn_conv_w[1],
                  ffn_conv_b[1], ffn_w_down[1], ffn_norm_post[1][None], "conv_ffn_1")
    return x
```

```python
import functools
import math

import jax
import jax.numpy as jnp
from jax import lax
from jax.experimental import pallas as pl
from jax.experimental.pallas import tpu as pltpu

F32 = jnp.float32
BF16 = jnp.bfloat16

D_MODEL = 1024
RET_HEADS = 4
RET_QK_DIM = D_MODEL // RET_HEADS
RET_V_DIM = 2 * RET_QK_DIM
RET_CHUNK = 256
DIFF_HEAD_DIM = 64
DIFF_HEADS = D_MODEL // (2 * DIFF_HEAD_DIM)
DIFF_V_DIM = 2 * DIFF_HEAD_DIM
D_FF = ((8 * D_MODEL // 3 + 127) // 128) * 128
ROPE_BASE = 10000.0
EPS = 1e-6
MASK_VALUE = -1e30
LOG2_E = math.log2(math.e)

BF16_ROWS = 16
LANES = 128
MXU_COLS = 256
MIB = 1024 * 1024

RET_PROJ_ROWS = 512
KVQ_TM = 1024
FFN_TM = 512
FFN_TF = 256
ATT_TQ = 512
ATT_TK = 256
ATT_HEADS = 1


def _params(semantics, vmem_bytes):
    return pltpu.CompilerParams(dimension_semantics=semantics,
                                vmem_limit_bytes=int(vmem_bytes))


def _resident(shape):
    zeros = (0,) * len(shape)
    return pl.BlockSpec(shape, lambda *_: zeros, pipeline_mode=pl.Buffered(1))


def _rms_scale(v):
    return lax.rsqrt(jnp.mean(v * v, axis=-1, keepdims=True) + EPS)


def _retention_kernel(lg_ref, x_ref, gain_ref, wq_ref, wk_ref, wv_ref, wg_ref, cos_ref, sin_ref,
                      o_ref, xn_ref, q_ref, k_ref, v_ref, sg_ref, state_ref):
    c_len, p_len = RET_CHUNK, RET_PROJ_ROWS
    seq = x_ref.shape[1]
    n_chunks = seq // c_len
    half = RET_QK_DIM // 2

    @pl.when(pl.program_id(1) == 0)
    def _():
        for r0 in range(0, seq, p_len):
            x = x_ref[0, r0:r0 + p_len, :]
            xn_ref[r0:r0 + p_len, :] = (x * _rms_scale(x) * gain_ref[...]).astype(BF16)

    lg = lg_ref[pl.program_id(1)]
    row = lax.broadcasted_iota(jnp.int32, (c_len, c_len), 0)
    col = lax.broadcasted_iota(jnp.int32, (c_len, c_len), 1)
    rel = (row - col).astype(F32)
    decay_mask = jnp.where(rel >= 0, jnp.exp(lg * jnp.maximum(rel, 0.0)), 0.0)
    idx = lax.broadcasted_iota(jnp.int32, (c_len, 1), 0).astype(F32)
    q_decay = jnp.exp(lg * (idx + 1.0))
    k_decay = jnp.exp(lg * (c_len - 1.0 - idx))
    chunk_decay = jnp.exp(lg * jnp.full((1, 1), c_len, F32))

    def project(r0):
        rows = slice(r0, r0 + p_len)
        xn = xn_ref[rows, :]
        cos = cos_ref[rows, :]
        sin = sin_ref[rows, :]

        def rotary(a, scale, dst_ref):
            x1, x2 = a[:, :half], a[:, half:]
            dst_ref[rows, :half] = ((x1 * cos - x2 * sin) * scale).astype(BF16)
            dst_ref[rows, half:] = ((x1 * sin + x2 * cos) * scale).astype(BF16)

        rotary(jnp.dot(xn, wq_ref[...], preferred_element_type=F32), 1.0, q_ref)
        rotary(jnp.dot(xn, wk_ref[...], preferred_element_type=F32), RET_QK_DIM ** -0.5, k_ref)
        v_ref[rows, :] = jnp.dot(xn, wv_ref[...], preferred_element_type=F32).astype(BF16)
        half_g = 0.5 * jnp.dot(xn, wg_ref[...], preferred_element_type=F32)
        sg_ref[rows, :] = (half_g * (1.0 + jnp.tanh(half_g))).astype(BF16)

    def rows(c):
        return slice(c * c_len, (c + 1) * c_len)

    def scores(c):
        return lax.dot_general(q_ref[rows(c), :], k_ref[rows(c), :],
                               (((1,), (1,)), ((), ())), preferred_element_type=F32)

    chunks_per_tile = p_len // c_len
    project(0)
    s_next = scores(0)
    for c in range(n_chunks):
        next_tile = (c // chunks_per_tile + 1) * p_len
        if c % chunks_per_tile == 0 and next_tile < seq:
            project(next_tile)
        q = q_ref[rows(c), :]
        k = k_ref[rows(c), :]
        v = v_ref[rows(c), :]
        s_masked = (s_next * decay_mask).astype(BF16)
        out = None
        if c > 0:
            out = jnp.dot(q, state_ref[...].astype(BF16),
                          preferred_element_type=F32) * q_decay
        if c + 1 < n_chunks:
            s_next = scores(c + 1)
            kd = (k.astype(F32) * k_decay).astype(BF16)
            update = lax.dot_general(kd, v, (((0,), (0,)), ((), ())),
                                     preferred_element_type=F32)
        inner = jnp.dot(s_masked, v, preferred_element_type=F32)
        if c + 1 < n_chunks:
            state_ref[...] = update if c == 0 else state_ref[...] * chunk_decay + update
        out = inner if out is None else inner + out
        gate = sg_ref[rows(c), :].astype(F32)
        o_ref[0, rows(c), :] = (out * _rms_scale(out) * gate).astype(BF16)


def _retention(x3d, gain, w_in, cos, sin, log_gamma):
    b, s, d = x3d.shape
    dk, dv, h = RET_QK_DIM, RET_V_DIM, RET_HEADS
    assert RET_PROJ_ROWS % RET_CHUNK == 0 and s % RET_PROJ_ROWS == 0
    k_blk0 = h * dk // dk
    v_blk0 = 2 * h * dk // dv
    g_blk0 = (2 * h * dk + h * dv) // dv
    vmem = (2 * s * d * 4 + 2 * d * (2 * dk + 2 * dv) * 2 + 2 * s * dk * 4 + 2 * s * dv * 2
            + s * d * 2 + s * (2 * dk + 2 * dv) * 2 + dk * dv * 4
            + 8 * RET_PROJ_ROWS * dv * 4 + 4 * MIB)
    return pl.pallas_call(
        _retention_kernel,
        grid=(b, h),
        in_specs=[
            pl.BlockSpec(memory_space=pltpu.SMEM),
            pl.BlockSpec((1, s, d), lambda i, j: (i, 0, 0)),
            pl.BlockSpec((1, d), lambda i, j: (0, 0)),
            pl.BlockSpec((d, dk), lambda i, j: (0, j)),
            pl.BlockSpec((d, dk), lambda i, j: (0, k_blk0 + j)),
            pl.BlockSpec((d, dv), lambda i, j: (0, v_blk0 + j)),
            pl.BlockSpec((d, dv), lambda i, j: (0, g_blk0 + j)),
            _resident((s, dk // 2)),
            _resident((s, dk // 2)),
        ],
        out_specs=pl.BlockSpec((1, s, dv), lambda i, j: (i, 0, j)),
        out_shape=jax.ShapeDtypeStruct((b, s, h * dv), BF16),
        scratch_shapes=[
            pltpu.VMEM((s, d), BF16),
            pltpu.VMEM((s, dk), BF16),
            pltpu.VMEM((s, dk), BF16),
            pltpu.VMEM((s, dv), BF16),
            pltpu.VMEM((s, dv), BF16),
            pltpu.VMEM((dk, dv), F32),
        ],
        compiler_params=_params(("parallel", "arbitrary"), vmem),
        name="retention",
    )(log_gamma, x3d, gain, w_in, w_in, w_in, w_in, cos, sin)


def _mix_ffn_kernel(y_ref, yh_ref, x_ref, xh_ref, wmix_ref, gmix_ref, gpre_ref, wup_ref, cw_ref,
                    wdn_ref, gpost_ref, o_ref, ycat_ref, xres_ref, xn_ref, h0_ref, h1_ref,
                    act_ref):
    tm, f_dim = act_ref.shape
    tf = h0_ref.shape[2]
    n_f = f_dim // tf
    halo = BF16_ROWS

    ycat_ref[:halo, :] = yh_ref[0]
    ycat_ref[halo:, :] = y_ref[0]
    mix = jnp.dot(ycat_ref[...], wmix_ref[...], preferred_element_type=F32)
    mix = mix * _rms_scale(mix) * gmix_ref[...]
    x_halo = xh_ref[0] + mix[:halo]
    x_tile = x_ref[0] + mix[halo:]
    xres_ref[...] = x_tile

    gpre = gpre_ref[...]
    xn_ref[halo:, :] = (x_tile * _rms_scale(x_tile) * gpre).astype(BF16)
    xhn = jnp.where(pl.program_id(1) > 0, x_halo * _rms_scale(x_halo) * gpre, 0.0)
    xn_ref[:halo, :] = xhn.astype(BF16)

    def up_project(f, h_ref):
        xn = xn_ref[...]
        for half in range(2):
            col0 = half * f_dim + f * tf
            h_ref[half] = jnp.dot(xn, wup_ref[:, col0:col0 + tf], preferred_element_type=F32)

    def conv(h_ref, half, f):
        col0 = half * f_dim + f * tf
        cw = cw_ref[:, col0:col0 + tf]
        rows_back = lambda back: pl.ds(halo - back, tm)
        return (cw[3:4] + cw[2:3] * h_ref[half, rows_back(0), :]
                + cw[1:2] * h_ref[half, rows_back(1), :] + cw[0:1] * h_ref[half, rows_back(2), :])

    def gate(f, h_ref):
        half_g = conv(h_ref, 0, f)
        u = conv(h_ref, 1, f)
        act_ref[:, f * tf:(f + 1) * tf] = (half_g * (1.0 + jnp.tanh(half_g)) * u).astype(BF16)

    h_refs = (h0_ref, h1_ref)
    up_project(0, h_refs[0])
    for f in range(n_f):
        if f + 1 < n_f:
            up_project(f + 1, h_refs[(f + 1) % 2])
        gate(f, h_refs[f % 2])
    h = jnp.dot(act_ref[...], wdn_ref[...], preferred_element_type=F32)
    o_ref[0] = xres_ref[...] + h * _rms_scale(h) * gpost_ref[...]


def _mix_ffn(y3d, x3d, w_mix, g_mix, gpre, w_up, conv_w, conv_b, w_down, gpost, name):
    b, s, d = x3d.shape
    kdim = y3d.shape[2]
    f_dim = w_down.shape[0]
    assert f_dim % FFN_TF == 0
    tm, halo = FFN_TM, BF16_ROWS
    wmix = w_mix.astype(BF16)
    wup = w_up.astype(BF16)
    wdn = w_down.astype(BF16)
    cw = jnp.concatenate([conv_w, conv_b[None, :], jnp.zeros((4, 2 * f_dim), F32)], axis=0)
    cw = cw * jnp.where(jnp.arange(2 * f_dim) < f_dim, 0.5, 1.0)[None, :]
    halo_blocks = tm // halo

    def tile(width):
        return pl.BlockSpec((1, tm, width), lambda i, j: (i, j, 0))

    def halo_rows(width):
        return pl.BlockSpec((1, halo, width),
                            lambda i, j: (i, jnp.maximum(j * halo_blocks - 1, 0), 0))

    def row(width):
        return pl.BlockSpec((1, width), lambda i, j: (0, 0))

    vmem = (kdim * d * 2 + 2 * f_dim * d * 2 + f_dim * d * 2 + 3 * tm * kdim * 2
            + 5 * tm * d * 4 + (tm + halo) * d * 2 + tm * f_dim * 2
            + 4 * (tm + halo) * FFN_TF * 4 + 6 * tm * d * 4 + 4 * MIB)
    return pl.pallas_call(
        _mix_ffn_kernel,
        grid=(b, s // tm),
        in_specs=[
            tile(kdim), halo_rows(kdim), tile(d), halo_rows(d),
            _resident((kdim, d)), row(d), row(d),
            _resident((d, 2 * f_dim)), _resident((8, 2 * f_dim)), _resident((f_dim, d)), row(d),
        ],
        out_specs=tile(d),
        out_shape=jax.ShapeDtypeStruct((b, s, d), F32),
        scratch_shapes=[
            pltpu.VMEM((tm + halo, kdim), BF16),
            pltpu.VMEM((tm, d), F32),
            pltpu.VMEM((tm + halo, d), BF16),
            pltpu.VMEM((2, tm + halo, FFN_TF), F32),
            pltpu.VMEM((2, tm + halo, FFN_TF), F32),
            pltpu.VMEM((tm, f_dim), BF16),
        ],
        compiler_params=_params(("parallel", "parallel"), vmem),
        name=name,
    )(y3d, y3d, x3d, x3d, wmix, g_mix, gpre, wup, cw, wdn, gpost)


def _kvq_kernel(x_ref, gkv_ref, gq_ref, wkv_ref, wq_ref, kv_ref, q_ref):
    x = x_ref[...]
    xr = x * _rms_scale(x)
    kv = jnp.dot((xr * gkv_ref[...]).astype(BF16), wkv_ref[...], preferred_element_type=F32)
    kv_ref[...] = kv.astype(BF16)
    q = jnp.dot((xr * gq_ref[...]).astype(BF16), wq_ref[...], preferred_element_type=F32)
    q_ref[...] = (q * (DIFF_HEAD_DIM ** -0.5 * LOG2_E)).astype(BF16)


def _kvq(x2d, g_kv, g_q, w_kv, w_q):
    t, d = x2d.shape
    n_kv, n_q = w_kv.shape[1], w_q.shape[1]
    tm = KVQ_TM
    vmem = (2 * tm * d * 4 + d * (n_kv + n_q) * 2 + 2 * tm * (n_kv + n_q) * 2
            + 6 * tm * d * 4 + 2 * tm * (n_kv + n_q) * 4 + 4 * MIB)
    return pl.pallas_call(
        _kvq_kernel,
        grid=(t // tm,),
        in_specs=[
            pl.BlockSpec((tm, d), lambda i: (i, 0)),
            pl.BlockSpec((1, d), lambda i: (0, 0)),
            pl.BlockSpec((1, d), lambda i: (0, 0)),
            _resident((d, n_kv)),
            _resident((d, n_q)),
        ],
        out_specs=[pl.BlockSpec((tm, n_kv), lambda i: (i, 0)),
                   pl.BlockSpec((tm, n_q), lambda i: (i, 0))],
        out_shape=[jax.ShapeDtypeStruct((t, n_kv), BF16),
                   jax.ShapeDtypeStruct((t, n_q), BF16)],
        compiler_params=_params(("parallel",), vmem),
        name="kv_q_proj",
    )(x2d, g_kv, g_q, w_kv, w_q)


def _diff_attn_kernel(q_ref, k_ref, v_ref, lam_ref, gsub_ref, o_ref,
                      q2t_ref, vt_ref, s_ref, p_ref, a_ref, m_ref, acc_ref, *, lambda_init):
    tq, tk, d, dv = ATT_TQ, ATT_TK, DIFF_HEAD_DIM, DIFF_V_DIM
    strip = LANES
    n_q = q2t_ref.shape[1]
    n_kv = vt_ref.shape[1]
    n_pieces = 2 * tq // MXU_COLS
    strips_per_piece = MXU_COLS // strip

    def head(h):
        return slice(h * dv, (h + 1) * dv)

    for h in range(ATT_HEADS):
        for j in range(n_kv):
            vt_ref[h, j, :dv, :] = (
                v_ref[0, j * tk:(j + 1) * tk, head(h)].astype(F32).T.astype(BF16))
    vt_ref[:, :, dv:, :] = jnp.ones((ATT_HEADS, n_kv, BF16_ROWS, tk), BF16)

    for h in range(ATT_HEADS):
        for i in range(n_q):
            qt = q_ref[0, i * tq:(i + 1) * tq, head(h)].astype(F32).T
            feat = lax.broadcasted_iota(jnp.int32, qt.shape, 0)
            q2t_ref[h, i, :, :tq] = jnp.where(feat < d, qt, 0.0).astype(BF16)
            q2t_ref[h, i, :, tq:] = jnp.where(feat >= d, qt, 0.0).astype(BF16)

    lp = lam_ref[...]
    lam = (jnp.exp(jnp.sum(lp[0:1] * lp[1:2], axis=-1, keepdims=True))
           - jnp.exp(jnp.sum(lp[2:3] * lp[3:4], axis=-1, keepdims=True)) + lambda_init)

    tiles = [(i, j) for i in range(n_q) for j in range((i + 1) * tq // tk)]

    def piece(r):
        return slice(r * MXU_COLS, (r + 1) * MXU_COLS)

    def visibility(i, j, c):
        q_first = i * tq + (c * strip) % tq
        k_first = j * tk
        if k_first > q_first + strip - 1:
            return "none"
        return "some" if k_first + tk - 1 > q_first else "all"

    def piece_visible(i, j, r):
        seen = [visibility(i, j, c) != "none"
                for c in range(r * strips_per_piece, (r + 1) * strips_per_piece)]
        assert all(seen) or not any(seen)
        return seen[0]

    def scores(h, t, r):
        i, j = tiles[t]
        if not piece_visible(i, j, r):
            return
        res = jnp.dot(k_ref[0, j * tk:(j + 1) * tk, head(h)], q2t_ref[h, i, :, piece(r)],
                      preferred_element_type=F32)
        for c in range(strips_per_piece):
            s_ref[h, t % 2, r * strips_per_piece + c] = res[:, c * strip:(c + 1) * strip]

    def softmax(h, t, r):
        i, j = tiles[t]
        if not piece_visible(i, j, r):
            return
        for c in range(r * strips_per_piece, (r + 1) * strips_per_piece):
            lanes = slice(c * strip, (c + 1) * strip)
            s = s_ref[h, t % 2, c]
            if visibility(i, j, c) == "some":
                rel = (lax.broadcasted_iota(jnp.int32, (tk, strip), 1)
                       - lax.broadcasted_iota(jnp.int32, (tk, strip), 0))
                s = jnp.where(rel >= j * tk - i * tq - (c * strip) % tq, s, MASK_VALUE)
            m_new = jnp.max(s, axis=0, keepdims=True)
            if j > 0:
                m_old = m_ref[h, i, :, lanes]
                m_new = jnp.maximum(m_old, m_new)
                a_ref[h, t % 2, :, lanes] = jnp.exp2(m_old - m_new)
            m_ref[h, i, :, lanes] = m_new
            p_ref[h, t % 2, c] = jnp.exp2(s - m_new).astype(BF16)

    def accumulate(h, t, r):
        i, j = tiles[t]
        if not piece_visible(i, j, r):
            return
        p = jnp.concatenate([p_ref[h, t % 2, r * strips_per_piece + c]
                             for c in range(strips_per_piece)], axis=1)
        pv = jnp.dot(vt_ref[h, j], p, preferred_element_type=F32)
        if j > 0:
            pv = a_ref[h, t % 2, :, piece(r)] * acc_ref[h, i, :, piece(r)] + pv
        acc_ref[h, i, :, piece(r)] = pv

    def finalize(i):
        for h in range(ATT_HEADS):
            inv_l = 1.0 / acc_ref[h, i, dv:dv + 1, :]
            o = (acc_ref[h, i, :dv, :tq] * inv_l[:, :tq]
                 - lam * (acc_ref[h, i, :dv, tq:] * inv_l[:, tq:]))
            r = lax.rsqrt(jnp.mean(o * o, axis=0, keepdims=True) + EPS)
            o = o * r * gsub_ref[...] * (1.0 - lambda_init)
            o_ref[0, i * tq:(i + 1) * tq, head(h)] = o.T.astype(BF16)

    heads = range(ATT_HEADS)
    for r in range(n_pieces):
        for h in heads:
            scores(h, 0, r)
    for t in range(len(tiles)):
        for r in range(n_pieces):
            for h in heads:
                if t + 1 < len(tiles):
                    scores(h, t + 1, r)
                softmax(h, t, r)
                if t > 0:
                    accumulate(h, t - 1, r)
        if t > 0 and tiles[t][0] != tiles[t - 1][0]:
            finalize(tiles[t - 1][0])
    for r in range(n_pieces):
        for h in heads:
            accumulate(h, len(tiles) - 1, r)
    finalize(n_q - 1)


def _diff_attention(q3d, kv3d, lam_params, g_sub_col, lambda_init):
    b, s, _ = q3d.shape
    h, dv = DIFF_HEADS, DIFF_V_DIM
    tq, tk = ATT_TQ, ATT_TK
    assert tq % tk == 0 and s % tq == 0 and h % ATT_HEADS == 0
    hps = ATT_HEADS
    dv_aug = dv + BF16_ROWS
    n_q = s // tq
    strips = (hps, 2, 2 * tq // LANES, tk, LANES)
    vmem = hps * (2 * 4 * s * dv * 2 + 2 * s * dv * 2 + s * dv_aug * 2
                  + 2 * tk * 2 * tq * (4 + 2) + n_q * dv_aug * 2 * tq * 4
                  + 4 * tk * 2 * tq * 4) + 8 * MIB

    def head_block(first):
        return pl.BlockSpec((1, s, hps * dv), lambda bi, hi: (bi, 0, first // hps + hi))

    return pl.pallas_call(
        functools.partial(_diff_attn_kernel, lambda_init=lambda_init),
        grid=(b, h // hps),
        in_specs=[
            head_block(0),
            head_block(0),
            head_block(h),
            pl.BlockSpec(lam_params.shape, lambda bi, hi: (0, 0)),
            pl.BlockSpec((dv, 1), lambda bi, hi: (0, 0)),
        ],
        out_specs=head_block(0),
        out_shape=jax.ShapeDtypeStruct((b, s, h * dv), BF16),
        scratch_shapes=[
            pltpu.VMEM((hps, n_q, dv, 2 * tq), BF16),
            pltpu.VMEM((hps, s // tk, dv_aug, tk), BF16),
            pltpu.VMEM(strips, F32),
            pltpu.VMEM(strips, BF16),
            pltpu.VMEM((hps, 2, 1, 2 * tq), F32),
            pltpu.VMEM((hps, n_q, 1, 2 * tq), F32),
            pltpu.VMEM((hps, n_q, dv_aug, 2 * tq), F32),
        ],
        compiler_params=_params(("parallel", "parallel"), vmem),
        name="diff_attention",
    )(q3d, kv3d, kv3d, lam_params, g_sub_col)


def kernel(x, a_norm_pre, a_norm_post, a_w_in, a_w_out, kv_norm, w_kv, b_norm_pre, b_norm_post,
           b_w_q, b_lambda, b_subln, b_w_out, ffn_norm_pre, ffn_norm_post, ffn_w_up, ffn_conv_w,
           ffn_conv_b, ffn_w_down):
    b, s, d = x.shape
    t = b * s

    half = RET_QK_DIM // 2
    inv = 1.0 / (ROPE_BASE ** jnp.linspace(0.0, 1.0, half, dtype=F32))
    ang = jnp.arange(s).astype(F32)[:, None] * inv[None, :]
    cos, sin = jnp.cos(ang), jnp.sin(ang)
    log_gamma = jnp.log1p(-jnp.power(2.0, -5.0 - jnp.arange(RET_HEADS, dtype=F32)))

    y = _retention(x, a_norm_pre[0][None], a_w_in[0].astype(BF16), cos, sin, log_gamma)
    x = _mix_ffn(y, x, a_w_out[0], a_norm_post[0][None], ffn_norm_pre[0][None], ffn_w_up[0],
                 ffn_conv_w[0], ffn_conv_b[0], ffn_w_down[0], ffn_norm_post[0][None],
                 "ret_out_conv_ffn")

    layer = 1
    lambda_init = 0.8 - 0.6 * math.exp(-0.3 * layer)
    kv, q = _kvq(x.reshape(t, d), kv_norm[None], b_norm_pre[0][None],
                 w_kv.astype(BF16), b_w_q[0].astype(BF16))
    o = _diff_attention(q.reshape(b, s, -1), kv.reshape(b, s, -1), b_lambda[0],
                        b_subln[0][:, None], lambda_init)
    x = _mix_ffn(o, x, b_w_out[0], b_norm_post[0][None], ffn_norm_pre[1][None], ffn_w_up[1],
                 ffn_conv_w[1], ffn_conv_b[1], ffn_w_down[1], ffn_norm_post[1][None],
                 "attn_out_conv_ffn")
    return x
```

```python
import functools
import math

import jax
import jax.numpy as jnp
from jax import lax
from jax.experimental import pallas as pl
from jax.experimental.pallas import tpu as pltpu

F32 = jnp.float32
BF16 = jnp.bfloat16

D_MODEL = 1024
RET_HEADS = 4
RET_QK_DIM = D_MODEL // RET_HEADS
RET_V_DIM = 2 * RET_QK_DIM
RET_CHUNK = 256
DIFF_HEAD_DIM = 64
DIFF_HEADS = D_MODEL // (2 * DIFF_HEAD_DIM)
DIFF_V_DIM = 2 * DIFF_HEAD_DIM
D_FF = ((8 * D_MODEL // 3 + 127) // 128) * 128
ROPE_BASE = 10000.0
EPS = 1e-6
MASK_VALUE = -1e30
LOG2_E = math.log2(math.e)

BF16_ROWS = 16
LANES = 128
MXU_COLS = 256
MIB = 1024 * 1024

RET_PROJ_ROWS = 512
POST_TM = 1024
KVQ_TM = 1024
FFN_TM = 512
FFN_TF = 256
FFN_UP_CHUNK = 512
FFN_DOWN_CHUNK = 256
ATT_TQ = 512
ATT_TK = 256
ATT_HEADS = 1


def _params(semantics, vmem_bytes):
    return pltpu.CompilerParams(dimension_semantics=semantics,
                                vmem_limit_bytes=int(vmem_bytes))


def _resident(shape):
    zeros = (0,) * len(shape)
    return pl.BlockSpec(shape, lambda *_: zeros, pipeline_mode=pl.Buffered(1))


def _rms_scale(v):
    return lax.rsqrt(jnp.mean(v * v, axis=-1, keepdims=True) + EPS)


def _retention_kernel(lg_ref, x_ref, gain_ref, wq_ref, wk_ref, wv_ref, wg_ref, cos_ref, sin_ref,
                      o_ref, xn_ref, q_ref, k_ref, v_ref, sg_ref, state_ref):
    c_len, p_len = RET_CHUNK, RET_PROJ_ROWS
    seq = x_ref.shape[1]
    n_chunks = seq // c_len
    half = RET_QK_DIM // 2

    @pl.when(pl.program_id(1) == 0)
    def _():
        for r0 in range(0, seq, p_len):
            x = x_ref[0, r0:r0 + p_len, :]
            xn_ref[r0:r0 + p_len, :] = (x * _rms_scale(x) * gain_ref[...]).astype(BF16)

    lg = lg_ref[pl.program_id(1)]
    row = lax.broadcasted_iota(jnp.int32, (c_len, c_len), 0)
    col = lax.broadcasted_iota(jnp.int32, (c_len, c_len), 1)
    rel = (row - col).astype(F32)
    decay_mask = jnp.where(rel >= 0, jnp.exp(lg * jnp.maximum(rel, 0.0)), 0.0)
    idx = lax.broadcasted_iota(jnp.int32, (c_len, 1), 0).astype(F32)
    q_decay = jnp.exp(lg * (idx + 1.0))
    k_decay = jnp.exp(lg * (c_len - 1.0 - idx))
    chunk_decay = jnp.exp(lg * jnp.full((1, 1), c_len, F32))

    def project(r0):
        rows = slice(r0, r0 + p_len)
        xn = xn_ref[rows, :]
        cos = cos_ref[rows, :]
        sin = sin_ref[rows, :]

        def rotary(a, scale, dst_ref):
            x1, x2 = a[:, :half], a[:, half:]
            dst_ref[rows, :half] = ((x1 * cos - x2 * sin) * scale).astype(BF16)
            dst_ref[rows, half:] = ((x1 * sin + x2 * cos) * scale).astype(BF16)

        rotary(jnp.dot(xn, wq_ref[...], preferred_element_type=F32), 1.0, q_ref)
        rotary(jnp.dot(xn, wk_ref[...], preferred_element_type=F32), RET_QK_DIM ** -0.5, k_ref)
        v_ref[rows, :] = jnp.dot(xn, wv_ref[...], preferred_element_type=F32).astype(BF16)
        half_g = 0.5 * jnp.dot(xn, wg_ref[...], preferred_element_type=F32)
        sg_ref[rows, :] = (half_g * (1.0 + jnp.tanh(half_g))).astype(BF16)

    def rows(c):
        return slice(c * c_len, (c + 1) * c_len)

    def scores(c):
        return lax.dot_general(q_ref[rows(c), :], k_ref[rows(c), :],
                               (((1,), (1,)), ((), ())), preferred_element_type=F32)

    chunks_per_tile = p_len // c_len
    project(0)
    s_next = scores(0)
    for c in range(n_chunks):
        next_tile = (c // chunks_per_tile + 1) * p_len
        if c % chunks_per_tile == 0 and next_tile < seq:
            project(next_tile)
        q = q_ref[rows(c), :]
        k = k_ref[rows(c), :]
        v = v_ref[rows(c), :]
        s_masked = (s_next * decay_mask).astype(BF16)
        out = None
        if c > 0:
            out = jnp.dot(q, state_ref[...].astype(BF16),
                          preferred_element_type=F32) * q_decay
        if c + 1 < n_chunks:
            s_next = scores(c + 1)
            kd = (k.astype(F32) * k_decay).astype(BF16)
            update = lax.dot_general(kd, v, (((0,), (0,)), ((), ())),
                                     preferred_element_type=F32)
        inner = jnp.dot(s_masked, v, preferred_element_type=F32)
        if c + 1 < n_chunks:
            state_ref[...] = update if c == 0 else state_ref[...] * chunk_decay + update
        out = inner if out is None else inner + out
        gate = sg_ref[rows(c), :].astype(F32)
        o_ref[0, rows(c), :] = (out * _rms_scale(out) * gate).astype(BF16)


def _retention(x3d, gain, w_in, cos, sin, log_gamma):
    b, s, d = x3d.shape
    dk, dv, h = RET_QK_DIM, RET_V_DIM, RET_HEADS
    assert RET_PROJ_ROWS % RET_CHUNK == 0 and s % RET_PROJ_ROWS == 0
    k_blk0 = h * dk // dk
    v_blk0 = 2 * h * dk // dv
    g_blk0 = (2 * h * dk + h * dv) // dv
    vmem = (2 * s * d * 4 + 2 * d * (2 * dk + 2 * dv) * 2 + 2 * s * dk * 4 + 2 * s * dv * 2
            + s * d * 2 + s * (2 * dk + 2 * dv) * 2 + dk * dv * 4
            + 8 * RET_PROJ_ROWS * dv * 4 + 4 * MIB)
    return pl.pallas_call(
        _retention_kernel,
        grid=(b, h),
        in_specs=[
            pl.BlockSpec(memory_space=pltpu.SMEM),
            pl.BlockSpec((1, s, d), lambda i, j: (i, 0, 0)),
            pl.BlockSpec((1, d), lambda i, j: (0, 0)),
            pl.BlockSpec((d, dk), lambda i, j: (0, j)),
            pl.BlockSpec((d, dk), lambda i, j: (0, k_blk0 + j)),
            pl.BlockSpec((d, dv), lambda i, j: (0, v_blk0 + j)),
            pl.BlockSpec((d, dv), lambda i, j: (0, g_blk0 + j)),
            _resident((s, dk // 2)),
            _resident((s, dk // 2)),
        ],
        out_specs=pl.BlockSpec((1, s, dv), lambda i, j: (i, 0, j)),
        out_shape=jax.ShapeDtypeStruct((b, s, h * dv), BF16),
        scratch_shapes=[
            pltpu.VMEM((s, d), BF16),
            pltpu.VMEM((s, dk), BF16),
            pltpu.VMEM((s, dk), BF16),
            pltpu.VMEM((s, dv), BF16),
            pltpu.VMEM((s, dv), BF16),
            pltpu.VMEM((dk, dv), F32),
        ],
        compiler_params=_params(("parallel", "arbitrary"), vmem),
        name="retention",
    )(log_gamma, x3d, gain, w_in, w_in, w_in, w_in, cos, sin)


def _post_kernel(y_ref, w_ref, g_ref, x_ref, o_ref):
    h = jnp.dot(y_ref[...], w_ref[...], preferred_element_type=F32)
    o_ref[...] = x_ref[...] + h * _rms_scale(h) * g_ref[...]


def _post(y2d, w, gain, x2d, name):
    t, kdim = y2d.shape
    d = w.shape[1]
    vmem = (2 * POST_TM * kdim * 2 + kdim * d * 2 + 4 * POST_TM * d * 4
            + 2 * POST_TM * d * 4 + 4 * MIB)
    return pl.pallas_call(
        _post_kernel,
        grid=(t // POST_TM,),
        in_specs=[
            pl.BlockSpec((POST_TM, kdim), lambda i: (i, 0)),
            _resident((kdim, d)),
            pl.BlockSpec((1, d), lambda i: (0, 0)),
            pl.BlockSpec((POST_TM, d), lambda i: (i, 0)),
        ],
        out_specs=pl.BlockSpec((POST_TM, d), lambda i: (i, 0)),
        out_shape=jax.ShapeDtypeStruct((t, d), F32),
        compiler_params=_params(("parallel",), vmem),
        name=name,
    )(y2d, w, gain, x2d)


def _ffn_kernel(x_ref, xh_ref, gpre_ref, wup_hbm, cw_ref, wdn_hbm, gpost_ref, o_ref,
                xn_ref, h0_ref, h1_ref, act_ref, wup_ref, wdn_ref, up_stage, dn_stage, sems,
                *, layer):
    tm, f_dim = act_ref.shape
    tf = h0_ref.shape[2]
    n_f = f_dim // tf
    halo = BF16_ROWS

    up_cols, dn_rows = up_stage.shape[2], dn_stage.shape[1]
    chunks = ([("up", c) for c in range(2 * f_dim // up_cols)]
              + [("dn", c) for c in range(f_dim // dn_rows)])

    def chunk_copy(n):
        kind, c = chunks[n]
        if kind == "up":
            src = wup_hbm.at[layer, :, pl.ds(c * up_cols, up_cols)]
            return pltpu.make_async_copy(src, up_stage.at[n % 2], sems.at[n % 2])
        src = wdn_hbm.at[layer, pl.ds(c * dn_rows, dn_rows), :]
        return pltpu.make_async_copy(src, dn_stage.at[n % 2], sems.at[n % 2])

    @pl.when((pl.program_id(0) == 0) & (pl.program_id(1) == 0))
    def _():
        chunk_copy(0).start()
        for n, (kind, c) in enumerate(chunks):
            if n + 1 < len(chunks):
                chunk_copy(n + 1).start()
            chunk_copy(n).wait()
            if kind == "up":
                wup_ref[:, c * up_cols:(c + 1) * up_cols] = up_stage[n % 2].astype(BF16)
            else:
                wdn_ref[c * dn_rows:(c + 1) * dn_rows, :] = dn_stage[n % 2].astype(BF16)

    x = x_ref[0]
    gpre = gpre_ref[...]
    xn_ref[halo:, :] = (x * _rms_scale(x) * gpre).astype(BF16)
    xh = xh_ref[0]
    xhn = jnp.where(pl.program_id(1) > 0, xh * _rms_scale(xh) * gpre, 0.0)
    xn_ref[:halo, :] = xhn.astype(BF16)

    def up_project(f, h_ref):
        xn = xn_ref[...]
        for half in range(2):
            col0 = half * f_dim + f * tf
            h_ref[half] = jnp.dot(xn, wup_ref[:, col0:col0 + tf], preferred_element_type=F32)

    def conv(h_ref, half, f):
        col0 = half * f_dim + f * tf
        cw = cw_ref[:, col0:col0 + tf]
        rows_back = lambda back: pl.ds(halo - back, tm)
        return (cw[3:4] + cw[2:3] * h_ref[half, rows_back(0), :]
                + cw[1:2] * h_ref[half, rows_back(1), :] + cw[0:1] * h_ref[half, rows_back(2), :])

    def gate(f, h_ref):
        half_g = conv(h_ref, 0, f)
        u = conv(h_ref, 1, f)
        act_ref[:, f * tf:(f + 1) * tf] = (half_g * (1.0 + jnp.tanh(half_g)) * u).astype(BF16)

    h_refs = (h0_ref, h1_ref)
    up_project(0, h_refs[0])
    for f in range(n_f):
        if f + 1 < n_f:
            up_project(f + 1, h_refs[(f + 1) % 2])
        gate(f, h_refs[f % 2])
    h = jnp.dot(act_ref[...], wdn_ref[...], preferred_element_type=F32)
    o_ref[0] = x + h * _rms_scale(h) * gpost_ref[...]


def _conv_ffn(x3d, gpre, w_up_all, conv_w, conv_b, w_down_all, gpost, layer, name):
    b, s, d = x3d.shape
    f_dim = w_down_all.shape[1]
    assert f_dim % FFN_TF == 0 and (2 * f_dim) % FFN_UP_CHUNK == 0 and f_dim % FFN_DOWN_CHUNK == 0
    tm, halo = FFN_TM, BF16_ROWS
    cw = jnp.concatenate([conv_w, conv_b[None, :], jnp.zeros((4, 2 * f_dim), F32)], axis=0)
    cw = cw * jnp.where(jnp.arange(2 * f_dim) < f_dim, 0.5, 1.0)[None, :]
    halo_blocks = tm // halo
    vmem = (3 * f_dim * d * 2 + 2 * d * FFN_UP_CHUNK * 4 + 2 * FFN_DOWN_CHUNK * d * 4
            + 4 * tm * d * 4 + (tm + halo) * d * 2 + tm * f_dim * 2
            + 4 * (tm + halo) * FFN_TF * 4 + 6 * tm * d * 4 + 4 * MIB)
    return pl.pallas_call(
        functools.partial(_ffn_kernel, layer=layer),
        grid=(b, s // tm),
        in_specs=[
            pl.BlockSpec((1, tm, d), lambda i, j: (i, j, 0)),
            pl.BlockSpec((1, halo, d), lambda i, j: (i, jnp.maximum(j * halo_blocks - 1, 0), 0)),
            pl.BlockSpec((1, d), lambda i, j: (0, 0)),
            pl.BlockSpec(memory_space=pl.ANY),
            _resident((8, 2 * f_dim)),
            pl.BlockSpec(memory_space=pl.ANY),
            pl.BlockSpec((1, d), lambda i, j: (0, 0)),
        ],
        out_specs=pl.BlockSpec((1, tm, d), lambda i, j: (i, j, 0)),
        out_shape=jax.ShapeDtypeStruct((b, s, d), F32),
        scratch_shapes=[
            pltpu.VMEM((tm + halo, d), BF16),
            pltpu.VMEM((2, tm + halo, FFN_TF), F32),
            pltpu.VMEM((2, tm + halo, FFN_TF), F32),
            pltpu.VMEM((tm, f_dim), BF16),
            pltpu.VMEM((d, 2 * f_dim), BF16),
            pltpu.VMEM((f_dim, d), BF16),
            pltpu.VMEM((2, d, FFN_UP_CHUNK), F32),
            pltpu.VMEM((2, FFN_DOWN_CHUNK, d), F32),
            pltpu.SemaphoreType.DMA((2,)),
        ],
        compiler_params=_params(("arbitrary", "arbitrary"), vmem),
        name=name,
    )(x3d, x3d, gpre, w_up_all, cw, w_down_all, gpost)


def _kvq_kernel(x_ref, gkv_ref, gq_ref, wkv_ref, wq_ref, kv_ref, q_ref):
    x = x_ref[...]
    xr = x * _rms_scale(x)
    kv = jnp.dot((xr * gkv_ref[...]).astype(BF16), wkv_ref[...], preferred_element_type=F32)
    kv_ref[...] = kv.astype(BF16)
    q = jnp.dot((xr * gq_ref[...]).astype(BF16), wq_ref[...], preferred_element_type=F32)
    q_ref[...] = (q * (DIFF_HEAD_DIM ** -0.5 * LOG2_E)).astype(BF16)


def _kvq(x2d, g_kv, g_q, w_kv, w_q):
    t, d = x2d.shape
    n_kv, n_q = w_kv.shape[1], w_q.shape[1]
    tm = KVQ_TM
    vmem = (2 * tm * d * 4 + d * (n_kv + n_q) * 2 + 2 * tm * (n_kv + n_q) * 2
            + 6 * tm * d * 4 + 2 * tm * (n_kv + n_q) * 4 + 4 * MIB)
    return pl.pallas_call(
        _kvq_kernel,
        grid=(t // tm,),
        in_specs=[
            pl.BlockSpec((tm, d), lambda i: (i, 0)),
            pl.BlockSpec((1, d), lambda i: (0, 0)),
            pl.BlockSpec((1, d), lambda i: (0, 0)),
            _resident((d, n_kv)),
            _resident((d, n_q)),
        ],
        out_specs=[pl.BlockSpec((tm, n_kv), lambda i: (i, 0)),
                   pl.BlockSpec((tm, n_q), lambda i: (i, 0))],
        out_shape=[jax.ShapeDtypeStruct((t, n_kv), BF16),
                   jax.ShapeDtypeStruct((t, n_q), BF16)],
        compiler_params=_params(("parallel",), vmem),
        name="kv_q_proj",
    )(x2d, g_kv, g_q, w_kv, w_q)


def _diff_attn_kernel(q_ref, k_ref, v_ref, lam_ref, gsub_ref, o_ref,
                      q2t_ref, vt_ref, s_ref, p_ref, a_ref, m_ref, acc_ref, *, lambda_init):
    tq, tk, d, dv = ATT_TQ, ATT_TK, DIFF_HEAD_DIM, DIFF_V_DIM
    strip = LANES
    n_q = q2t_ref.shape[1]
    n_kv = vt_ref.shape[1]
    n_pieces = 2 * tq // MXU_COLS
    strips_per_piece = MXU_COLS // strip

    def head(h):
        return slice(h * dv, (h + 1) * dv)

    for h in range(ATT_HEADS):
        for j in range(n_kv):
            vt_ref[h, j, :dv, :] = (
                v_ref[0, j * tk:(j + 1) * tk, head(h)].astype(F32).T.astype(BF16))
    vt_ref[:, :, dv:, :] = jnp.ones((ATT_HEADS, n_kv, BF16_ROWS, tk), BF16)

    for h in range(ATT_HEADS):
        for i in range(n_q):
            qt = q_ref[0, i * tq:(i + 1) * tq, head(h)].astype(F32).T
            feat = lax.broadcasted_iota(jnp.int32, qt.shape, 0)
            q2t_ref[h, i, :, :tq] = jnp.where(feat < d, qt, 0.0).astype(BF16)
            q2t_ref[h, i, :, tq:] = jnp.where(feat >= d, qt, 0.0).astype(BF16)

    lp = lam_ref[...]
    lam = (jnp.exp(jnp.sum(lp[0:1] * lp[1:2], axis=-1, keepdims=True))
           - jnp.exp(jnp.sum(lp[2:3] * lp[3:4], axis=-1, keepdims=True)) + lambda_init)

    tiles = [(i, j) for i in range(n_q) for j in range((i + 1) * tq // tk)]

    def piece(r):
        return slice(r * MXU_COLS, (r + 1) * MXU_COLS)

    def visibility(i, j, c):
        q_first = i * tq + (c * strip) % tq
        k_first = j * tk
        if k_first > q_first + strip - 1:
            return "none"
        return "some" if k_first + tk - 1 > q_first else "all"

    def piece_visible(i, j, r):
        seen = [visibility(i, j, c) != "none"
                for c in range(r * strips_per_piece, (r + 1) * strips_per_piece)]
        assert all(seen) or not any(seen)
        return seen[0]

    def scores(h, t, r):
        i, j = tiles[t]
        if not piece_visible(i, j, r):
            return
        res = jnp.dot(k_ref[0, j * tk:(j + 1) * tk, head(h)], q2t_ref[h, i, :, piece(r)],
                      preferred_element_type=F32)
        for c in range(strips_per_piece):
            s_ref[h, t % 2, r * strips_per_piece + c] = res[:, c * strip:(c + 1) * strip]

    def softmax(h, t, r):
        i, j = tiles[t]
        if not piece_visible(i, j, r):
            return
        for c in range(r * strips_per_piece, (r + 1) * strips_per_piece):
            lanes = slice(c * strip, (c + 1) * strip)
            s = s_ref[h, t % 2, c]
            if visibility(i, j, c) == "some":
                rel = (lax.broadcasted_iota(jnp.int32, (tk, strip), 1)
                       - lax.broadcasted_iota(jnp.int32, (tk, strip), 0))
                s = jnp.where(rel >= j * tk - i * tq - (c * strip) % tq, s, MASK_VALUE)
            m_new = jnp.max(s, axis=0, keepdims=True)
            if j > 0:
                m_old = m_ref[h, i, :, lanes]
                m_new = jnp.maximum(m_old, m_new)
                a_ref[h, t % 2, :, lanes] = jnp.exp2(m_old - m_new)
            m_ref[h, i, :, lanes] = m_new
            p_ref[h, t % 2, c] = jnp.exp2(s - m_new).astype(BF16)

    def accumulate(h, t, r):
        i, j = tiles[t]
        if not piece_visible(i, j, r):
            return
        p = jnp.concatenate([p_ref[h, t % 2, r * strips_per_piece + c]
                             for c in range(strips_per_piece)], axis=1)
        pv = jnp.dot(vt_ref[h, j], p, preferred_element_type=F32)
        if j > 0:
            pv = a_ref[h, t % 2, :, piece(r)] * acc_ref[h, i, :, piece(r)] + pv
        acc_ref[h, i, :, piece(r)] = pv

    def finalize(i):
        for h in range(ATT_HEADS):
            inv_l = 1.0 / acc_ref[h, i, dv:dv + 1, :]
            o = (acc_ref[h, i, :dv, :tq] * inv_l[:, :tq]
                 - lam * (acc_ref[h, i, :dv, tq:] * inv_l[:, tq:]))
            r = lax.rsqrt(jnp.mean(o * o, axis=0, keepdims=True) + EPS)
            o = o * r * gsub_ref[...] * (1.0 - lambda_init)
            o_ref[0, i * tq:(i + 1) * tq, head(h)] = o.T.astype(BF16)

    heads = range(ATT_HEADS)
    for r in range(n_pieces):
        for h in heads:
            scores(h, 0, r)
    for t in range(len(tiles)):
        for r in range(n_pieces):
            for h in heads:
                if t + 1 < len(tiles):
                    scores(h, t + 1, r)
                softmax(h, t, r)
                if t > 0:
                    accumulate(h, t - 1, r)
        if t > 0 and tiles[t][0] != tiles[t - 1][0]:
            finalize(tiles[t - 1][0])
    for r in range(n_pieces):
        for h in heads:
            accumulate(h, len(tiles) - 1, r)
    finalize(n_q - 1)


def _diff_attention(q3d, kv3d, lam_params, g_sub_col, lambda_init):
    b, s, _ = q3d.shape
    h, dv = DIFF_HEADS, DIFF_V_DIM
    tq, tk = ATT_TQ, ATT_TK
    assert tq % tk == 0 and s % tq == 0 and h % ATT_HEADS == 0
    hps = ATT_HEADS
    dv_aug = dv + BF16_ROWS
    n_q = s // tq
    strips = (hps, 2, 2 * tq // LANES, tk, LANES)
    vmem = hps * (2 * 4 * s * dv * 2 + 2 * s * dv * 2 + s * dv_aug * 2
                  + 2 * tk * 2 * tq * (4 + 2) + n_q * dv_aug * 2 * tq * 4
                  + 4 * tk * 2 * tq * 4) + 8 * MIB

    def head_block(first):
        return pl.BlockSpec((1, s, hps * dv), lambda bi, hi: (bi, 0, first // hps + hi))

    return pl.pallas_call(
        functools.partial(_diff_attn_kernel, lambda_init=lambda_init),
        grid=(b, h // hps),
        in_specs=[
            head_block(0),
            head_block(0),
            head_block(h),
            pl.BlockSpec(lam_params.shape, lambda bi, hi: (0, 0)),
            pl.BlockSpec((dv, 1), lambda bi, hi: (0, 0)),
        ],
        out_specs=head_block(0),
        out_shape=jax.ShapeDtypeStruct((b, s, h * dv), BF16),
        scratch_shapes=[
            pltpu.VMEM((hps, n_q, dv, 2 * tq), BF16),
            pltpu.VMEM((hps, s // tk, dv_aug, tk), BF16),
            pltpu.VMEM(strips, F32),
            pltpu.VMEM(strips, BF16),
            pltpu.VMEM((hps, 2, 1, 2 * tq), F32),
            pltpu.VMEM((hps, n_q, 1, 2 * tq), F32),
            pltpu.VMEM((hps, n_q, dv_aug, 2 * tq), F32),
        ],
        compiler_params=_params(("parallel", "parallel"), vmem),
        name="diff_attention",
    )(q3d, kv3d, kv3d, lam_params, g_sub_col)


def kernel(x, a_norm_pre, a_norm_post, a_w_in, a_w_out, kv_norm, w_kv, b_norm_pre, b_norm_post,
           b_w_q, b_lambda, b_subln, b_w_out, ffn_norm_pre, ffn_norm_post, ffn_w_up, ffn_conv_w,
           ffn_conv_b, ffn_w_down):
    b, s, d = x.shape
    t = b * s

    half = RET_QK_DIM // 2
    inv = 1.0 / (ROPE_BASE ** jnp.linspace(0.0, 1.0, half, dtype=F32))
    ang = jnp.arange(s).astype(F32)[:, None] * inv[None, :]
    cos, sin = jnp.cos(ang), jnp.sin(ang)
    log_gamma = jnp.log1p(-jnp.power(2.0, -5.0 - jnp.arange(RET_HEADS, dtype=F32)))

    y = _retention(x, a_norm_pre[0][None], a_w_in[0].astype(BF16), cos, sin, log_gamma)
    x = _post(y.reshape(t, -1), a_w_out[0].astype(BF16), a_norm_post[0][None],
              x.reshape(t, d), "ret_out_proj")
    x = _conv_ffn(x.reshape(b, s, d), ffn_norm_pre[0][None], ffn_w_up, ffn_conv_w[0],
                  ffn_conv_b[0], ffn_w_down, ffn_norm_post[0][None], 0, "conv_ffn_0")

    layer = 1
    lambda_init = 0.8 - 0.6 * math.exp(-0.3 * layer)
    kv, q = _kvq(x.reshape(t, d), kv_norm[None], b_norm_pre[0][None],
                 w_kv.astype(BF16), b_w_q[0].astype(BF16))
    o = _diff_attention(q.reshape(b, s, -1), kv.reshape(b, s, -1), b_lambda[0],
                        b_subln[0][:, None], lambda_init)
    x = _post(o.reshape(t, -1), b_w_out[0].astype(BF16), b_norm_post[0][None],
              x.reshape(t, d), "attn_out_proj")
    x = _conv_ffn(x.reshape(b, s, d), ffn_norm_pre[1][None], ffn_w_up, ffn_conv_w[1],
                  ffn_conv_b[1], ffn_w_down, ffn_norm_post[1][None], 1, "conv_ffn_1")
    return x
```

```python
import functools
import math

import jax
import jax.numpy as jnp
from jax import lax
from jax.experimental import pallas as pl
from jax.experimental.pallas import tpu as pltpu

F32 = jnp.float32
BF16 = jnp.bfloat16

D_MODEL = 1024
RET_HEADS = 4
RET_QK_DIM = D_MODEL // RET_HEADS
RET_V_DIM = 2 * RET_QK_DIM
RET_CHUNK = 256
DIFF_HEAD_DIM = 64
DIFF_HEADS = D_MODEL // (2 * DIFF_HEAD_DIM)
DIFF_V_DIM = 2 * DIFF_HEAD_DIM
D_FF = ((8 * D_MODEL // 3 + 127) // 128) * 128
ROPE_BASE = 10000.0
EPS = 1e-6
MASK_VALUE = -1e30
LOG2_E = math.log2(math.e)

BF16_ROWS = 16
LANES = 128
MXU_COLS = 256
MIB = 1024 * 1024

RET_PROJ_ROWS = 512
POST_TM = 1024
KVQ_TM = 1024
FFN_TM = 512
FFN_TF = 256
FFN_UP_CHUNK = 512
FFN_DOWN_CHUNK = 256
FFN_WEIGHT_SLOTS = 4
ATT_TQ = 512
ATT_TK = 256
ATT_HEADS = 1


def _params(semantics, vmem_bytes):
    return pltpu.CompilerParams(dimension_semantics=semantics,
                                vmem_limit_bytes=int(vmem_bytes))


def _resident(shape):
    zeros = (0,) * len(shape)
    return pl.BlockSpec(shape, lambda *_: zeros, pipeline_mode=pl.Buffered(1))


def _rms_scale(v):
    return lax.rsqrt(jnp.mean(v * v, axis=-1, keepdims=True) + EPS)


def _retention_kernel(lg_ref, x_ref, gain_ref, wq_ref, wk_ref, wv_ref, wg_ref, cos_ref, sin_ref,
                      o_ref, xn_ref, q_ref, k_ref, v_ref, sg_ref, state_ref):
    c_len, p_len = RET_CHUNK, RET_PROJ_ROWS
    seq = x_ref.shape[1]
    n_chunks = seq // c_len
    half = RET_QK_DIM // 2

    @pl.when(pl.program_id(1) == 0)
    def _():
        for r0 in range(0, seq, p_len):
            x = x_ref[0, r0:r0 + p_len, :]
            xn_ref[r0:r0 + p_len, :] = (x * _rms_scale(x) * gain_ref[...]).astype(BF16)

    lg = lg_ref[pl.program_id(1)]
    row = lax.broadcasted_iota(jnp.int32, (c_len, c_len), 0)
    col = lax.broadcasted_iota(jnp.int32, (c_len, c_len), 1)
    rel = (row - col).astype(F32)
    decay_mask = jnp.where(rel >= 0, jnp.exp(lg * jnp.maximum(rel, 0.0)), 0.0)
    idx = lax.broadcasted_iota(jnp.int32, (c_len, 1), 0).astype(F32)
    q_decay = jnp.exp(lg * (idx + 1.0))
    k_decay = jnp.exp(lg * (c_len - 1.0 - idx))
    chunk_decay = jnp.exp(lg * jnp.full((1, 1), c_len, F32))

    def project(r0):
        rows = slice(r0, r0 + p_len)
        xn = xn_ref[rows, :]
        cos = cos_ref[rows, :]
        sin = sin_ref[rows, :]

        def rotary(a, scale, dst_ref):
            x1, x2 = a[:, :half], a[:, half:]
            dst_ref[rows, :half] = ((x1 * cos - x2 * sin) * scale).astype(BF16)
            dst_ref[rows, half:] = ((x1 * sin + x2 * cos) * scale).astype(BF16)

        rotary(jnp.dot(xn, wq_ref[...], preferred_element_type=F32), 1.0, q_ref)
        rotary(jnp.dot(xn, wk_ref[...], preferred_element_type=F32), RET_QK_DIM ** -0.5, k_ref)
        v_ref[rows, :] = jnp.dot(xn, wv_ref[...], preferred_element_type=F32).astype(BF16)
        half_g = 0.5 * jnp.dot(xn, wg_ref[...], preferred_element_type=F32)
        sg_ref[rows, :] = (half_g * (1.0 + jnp.tanh(half_g))).astype(BF16)

    def rows(c):
        return slice(c * c_len, (c + 1) * c_len)

    def scores(c):
        return lax.dot_general(q_ref[rows(c), :], k_ref[rows(c), :],
                               (((1,), (1,)), ((), ())), preferred_element_type=F32)

    chunks_per_tile = p_len // c_len
    project(0)
    s_next = scores(0)
    for c in range(n_chunks):
        next_tile = (c // chunks_per_tile + 1) * p_len
        if c % chunks_per_tile == 0 and next_tile < seq:
            project(next_tile)
        q = q_ref[rows(c), :]
        k = k_ref[rows(c), :]
        v = v_ref[rows(c), :]
        s_masked = (s_next * decay_mask).astype(BF16)
        out = None
        if c > 0:
            out = jnp.dot(q, state_ref[...].astype(BF16),
                          preferred_element_type=F32) * q_decay
        if c + 1 < n_chunks:
            s_next = scores(c + 1)
            kd = (k.astype(F32) * k_decay).astype(BF16)
            update = lax.dot_general(kd, v, (((0,), (0,)), ((), ())),
                                     preferred_element_type=F32)
        inner = jnp.dot(s_masked, v, preferred_element_type=F32)
        if c + 1 < n_chunks:
            state_ref[...] = update if c == 0 else state_ref[...] * chunk_decay + update
        out = inner if out is None else inner + out
        gate = sg_ref[rows(c), :].astype(F32)
        o_ref[0, rows(c), :] = (out * _rms_scale(out) * gate).astype(BF16)


def _retention(x3d, gain, w_in, cos, sin, log_gamma):
    b, s, d = x3d.shape
    dk, dv, h = RET_QK_DIM, RET_V_DIM, RET_HEADS
    assert RET_PROJ_ROWS % RET_CHUNK == 0 and s % RET_PROJ_ROWS == 0
    k_blk0 = h * dk // dk
    v_blk0 = 2 * h * dk // dv
    g_blk0 = (2 * h * dk + h * dv) // dv
    vmem = (2 * s * d * 4 + 2 * d * (2 * dk + 2 * dv) * 2 + 2 * s * dk * 4 + 2 * s * dv * 2
            + s * d * 2 + s * (2 * dk + 2 * dv) * 2 + dk * dv * 4
            + 8 * RET_PROJ_ROWS * dv * 4 + 4 * MIB)
    return pl.pallas_call(
        _retention_kernel,
        grid=(b, h),
        in_specs=[
            pl.BlockSpec(memory_space=pltpu.SMEM),
            pl.BlockSpec((1, s, d), lambda i, j: (i, 0, 0)),
            pl.BlockSpec((1, d), lambda i, j: (0, 0)),
            pl.BlockSpec((d, dk), lambda i, j: (0, j)),
            pl.BlockSpec((d, dk), lambda i, j: (0, k_blk0 + j)),
            pl.BlockSpec((d, dv), lambda i, j: (0, v_blk0 + j)),
            pl.BlockSpec((d, dv), lambda i, j: (0, g_blk0 + j)),
            _resident((s, dk // 2)),
            _resident((s, dk // 2)),
        ],
        out_specs=pl.BlockSpec((1, s, dv), lambda i, j: (i, 0, j)),
        out_shape=jax.ShapeDtypeStruct((b, s, h * dv), BF16),
        scratch_shapes=[
            pltpu.VMEM((s, d), BF16),
            pltpu.VMEM((s, dk), BF16),
            pltpu.VMEM((s, dk), BF16),
            pltpu.VMEM((s, dv), BF16),
            pltpu.VMEM((s, dv), BF16),
            pltpu.VMEM((dk, dv), F32),
        ],
        compiler_params=_params(("parallel", "arbitrary"), vmem),
        name="retention",
    )(log_gamma, x3d, gain, w_in, w_in, w_in, w_in, cos, sin)


def _post_kernel(y_ref, w_ref, g_ref, x_ref, o_ref):
    h = jnp.dot(y_ref[...], w_ref[...], preferred_element_type=F32)
    o_ref[...] = x_ref[...] + h * _rms_scale(h) * g_ref[...]


def _post(y2d, w, gain, x2d, name):
    t, kdim = y2d.shape
    d = w.shape[1]
    vmem = (2 * POST_TM * kdim * 2 + kdim * d * 2 + 4 * POST_TM * d * 4
            + 2 * POST_TM * d * 4 + 4 * MIB)
    return pl.pallas_call(
        _post_kernel,
        grid=(t // POST_TM,),
        in_specs=[
            pl.BlockSpec((POST_TM, kdim), lambda i: (i, 0)),
            _resident((kdim, d)),
            pl.BlockSpec((1, d), lambda i: (0, 0)),
            pl.BlockSpec((POST_TM, d), lambda i: (i, 0)),
        ],
        out_specs=pl.BlockSpec((POST_TM, d), lambda i: (i, 0)),
        out_shape=jax.ShapeDtypeStruct((t, d), F32),
        compiler_params=_params(("parallel",), vmem),
        name=name,
    )(y2d, w, gain, x2d)


def _ffn_kernel(x_ref, xh_ref, gpre_ref, wup_hbm, cw_ref, wdn_hbm, gpost_ref, o_ref,
                xn_ref, h0_ref, h1_ref, act_ref, wup_ref, wdn_ref, up_stage, dn_stage, sems,
                *, layer):
    tm, f_dim = act_ref.shape
    tf = h0_ref.shape[2]
    n_f = f_dim // tf
    halo = BF16_ROWS

    n_slots, _, up_cols = up_stage.shape
    dn_rows = dn_stage.shape[1]
    chunks = ([("up", c) for c in range(2 * f_dim // up_cols)]
              + [("dn", c) for c in range(f_dim // dn_rows)])

    def chunk_copy(n):
        kind, c = chunks[n]
        slot = n % n_slots
        if kind == "up":
            src = wup_hbm.at[layer, :, pl.ds(c * up_cols, up_cols)]
            return pltpu.make_async_copy(src, up_stage.at[slot], sems.at[slot])
        src = wdn_hbm.at[layer, pl.ds(c * dn_rows, dn_rows), :]
        return pltpu.make_async_copy(src, dn_stage.at[slot], sems.at[slot])

    @pl.when((pl.program_id(0) == 0) & (pl.program_id(1) == 0))
    def _():
        for n in range(n_slots - 1):
            chunk_copy(n).start()
        for n, (kind, c) in enumerate(chunks):
            if n + n_slots - 1 < len(chunks):
                chunk_copy(n + n_slots - 1).start()
            chunk_copy(n).wait()
            if kind == "up":
                wup_ref[:, c * up_cols:(c + 1) * up_cols] = up_stage[n % n_slots].astype(BF16)
            else:
                wdn_ref[c * dn_rows:(c + 1) * dn_rows, :] = dn_stage[n % n_slots].astype(BF16)

    x = x_ref[0]
    gpre = gpre_ref[...]
    xn_ref[halo:, :] = (x * _rms_scale(x) * gpre).astype(BF16)
    xh = xh_ref[0]
    xhn = jnp.where(pl.program_id(1) > 0, xh * _rms_scale(xh) * gpre, 0.0)
    xn_ref[:halo, :] = xhn.astype(BF16)

    def up_project(f, h_ref):
        xn = xn_ref[...]
        for half in range(2):
            col0 = half * f_dim + f * tf
            h_ref[half] = jnp.dot(xn, wup_ref[:, col0:col0 + tf], preferred_element_type=F32)

    def conv(h_ref, half, f):
        col0 = half * f_dim + f * tf
        cw = cw_ref[:, col0:col0 + tf]
        rows_back = lambda back: pl.ds(halo - back, tm)
        return (cw[3:4] + cw[2:3] * h_ref[half, rows_back(0), :]
                + cw[1:2] * h_ref[half, rows_back(1), :] + cw[0:1] * h_ref[half, rows_back(2), :])

    def gate(f, h_ref):
        half_g = conv(h_ref, 0, f)
        u = conv(h_ref, 1, f)
        act_ref[:, f * tf:(f + 1) * tf] = (half_g * (1.0 + jnp.tanh(half_g)) * u).astype(BF16)

    h_refs = (h0_ref, h1_ref)
    up_project(0, h_refs[0])
    for f in range(n_f):
        if f + 1 < n_f:
            up_project(f + 1, h_refs[(f + 1) % 2])
        gate(f, h_refs[f % 2])
    h = jnp.dot(act_ref[...], wdn_ref[...], preferred_element_type=F32)
    o_ref[0] = x + h * _rms_scale(h) * gpost_ref[...]


def _conv_ffn(x3d, gpre, w_up_all, conv_w, conv_b, w_down_all, gpost, layer, name):
    b, s, d = x3d.shape
    f_dim = w_down_all.shape[1]
    assert f_dim % FFN_TF == 0 and (2 * f_dim) % FFN_UP_CHUNK == 0 and f_dim % FFN_DOWN_CHUNK == 0
    tm, halo = FFN_TM, BF16_ROWS
    cw = jnp.concatenate([conv_w, conv_b[None, :], jnp.zeros((4, 2 * f_dim), F32)], axis=0)
    cw = cw * jnp.where(jnp.arange(2 * f_dim) < f_dim, 0.5, 1.0)[None, :]
    halo_blocks = tm // halo
    vmem = (3 * f_dim * d * 2 + FFN_WEIGHT_SLOTS * (d * FFN_UP_CHUNK + FFN_DOWN_CHUNK * d) * 4
            + 4 * tm * d * 4 + (tm + halo) * d * 2 + tm * f_dim * 2
            + 4 * (tm + halo) * FFN_TF * 4 + 6 * tm * d * 4 + 4 * MIB)
    return pl.pallas_call(
        functools.partial(_ffn_kernel, layer=layer),
        grid=(b, s // tm),
        in_specs=[
            pl.BlockSpec((1, tm, d), lambda i, j: (i, j, 0)),
            pl.BlockSpec((1, halo, d), lambda i, j: (i, jnp.maximum(j * halo_blocks - 1, 0), 0)),
            pl.BlockSpec((1, d), lambda i, j: (0, 0)),
            pl.BlockSpec(memory_space=pl.ANY),
            _resident((8, 2 * f_dim)),
            pl.BlockSpec(memory_space=pl.ANY),
            pl.BlockSpec((1, d), lambda i, j: (0, 0)),
        ],
        out_specs=pl.BlockSpec((1, tm, d), lambda i, j: (i, j, 0)),
        out_shape=jax.ShapeDtypeStruct((b, s, d), F32),
        scratch_shapes=[
            pltpu.VMEM((tm + halo, d), BF16),
            pltpu.VMEM((2, tm + halo, FFN_TF), F32),
            pltpu.VMEM((2, tm + halo, FFN_TF), F32),
            pltpu.VMEM((tm, f_dim), BF16),
            pltpu.VMEM((d, 2 * f_dim), BF16),
            pltpu.VMEM((f_dim, d), BF16),
            pltpu.VMEM((FFN_WEIGHT_SLOTS, d, FFN_UP_CHUNK), F32),
            pltpu.VMEM((FFN_WEIGHT_SLOTS, FFN_DOWN_CHUNK, d), F32),
            pltpu.SemaphoreType.DMA((FFN_WEIGHT_SLOTS,)),
        ],
        compiler_params=_params(("arbitrary", "arbitrary"), vmem),
        name=name,
    )(x3d, x3d, gpre, w_up_all, cw, w_down_all, gpost)


def _kvq_kernel(x_ref, gkv_ref, gq_ref, wkv_ref, wq_ref, kv_ref, q_ref):
    x = x_ref[...]
    xr = x * _rms_scale(x)
    kv = jnp.dot((xr * gkv_ref[...]).astype(BF16), wkv_ref[...], preferred_element_type=F32)
    kv_ref[...] = kv.astype(BF16)
    q = jnp.dot((xr * gq_ref[...]).astype(BF16), wq_ref[...], preferred_element_type=F32)
    q_ref[...] = (q * (DIFF_HEAD_DIM ** -0.5 * LOG2_E)).astype(BF16)


def _kvq(x2d, g_kv, g_q, w_kv, w_q):
    t, d = x2d.shape
    n_kv, n_q = w_kv.shape[1], w_q.shape[1]
    tm = KVQ_TM
    vmem = (2 * tm * d * 4 + d * (n_kv + n_q) * 2 + 2 * tm * (n_kv + n_q) * 2
            + 6 * tm * d * 4 + 2 * tm * (n_kv + n_q) * 4 + 4 * MIB)
    return pl.pallas_call(
        _kvq_kernel,
        grid=(t // tm,),
        in_specs=[
            pl.BlockSpec((tm, d), lambda i: (i, 0)),
            pl.BlockSpec((1, d), lambda i: (0, 0)),
            pl.BlockSpec((1, d), lambda i: (0, 0)),
            _resident((d, n_kv)),
            _resident((d, n_q)),
        ],
        out_specs=[pl.BlockSpec((tm, n_kv), lambda i: (i, 0)),
                   pl.BlockSpec((tm, n_q), lambda i: (i, 0))],
        out_shape=[jax.ShapeDtypeStruct((t, n_kv), BF16),
                   jax.ShapeDtypeStruct((t, n_q), BF16)],
        compiler_params=_params(("parallel",), vmem),
        name="kv_q_proj",
    )(x2d, g_kv, g_q, w_kv, w_q)


def _diff_attn_kernel(q_ref, k_ref, v_ref, lam_ref, gsub_ref, o_ref,
                      q2t_ref, vt_ref, s_ref, p_ref, a_ref, m_ref, acc_ref, *, lambda_init):
    tq, tk, d, dv = ATT_TQ, ATT_TK, DIFF_HEAD_DIM, DIFF_V_DIM
    strip = LANES
    n_q = q2t_ref.shape[1]
    n_kv = vt_ref.shape[1]
    n_pieces = 2 * tq // MXU_COLS
    strips_per_piece = MXU_COLS // strip

    def head(h):
        return slice(h * dv, (h + 1) * dv)

    for h in range(ATT_HEADS):
        for j in range(n_kv):
            vt_ref[h, j, :dv, :] = (
                v_ref[0, j * tk:(j + 1) * tk, head(h)].astype(F32).T.astype(BF16))
    vt_ref[:, :, dv:, :] = jnp.ones((ATT_HEADS, n_kv, BF16_ROWS, tk), BF16)

    for h in range(ATT_HEADS):
        for i in range(n_q):
            qt = q_ref[0, i * tq:(i + 1) * tq, head(h)].astype(F32).T
            feat = lax.broadcasted_iota(jnp.int32, qt.shape, 0)
            q2t_ref[h, i, :, :tq] = jnp.where(feat < d, qt, 0.0).astype(BF16)
            q2t_ref[h, i, :, tq:] = jnp.where(feat >= d, qt, 0.0).astype(BF16)

    lp = lam_ref[...]
    lam = (jnp.exp(jnp.sum(lp[0:1] * lp[1:2], axis=-1, keepdims=True))
           - jnp.exp(jnp.sum(lp[2:3] * lp[3:4], axis=-1, keepdims=True)) + lambda_init)

    tiles = [(i, j) for i in range(n_q) for j in range((i + 1) * tq // tk)]

    def piece(r):
        return slice(r * MXU_COLS, (r + 1) * MXU_COLS)

    def visibility(i, j, c):
        q_first = i * tq + (c * strip) % tq
        k_first = j * tk
        if k_first > q_first + strip - 1:
            return "none"
        return "some" if k_first + tk - 1 > q_first else "all"

    def piece_visible(i, j, r):
        seen = [visibility(i, j, c) != "none"
                for c in range(r * strips_per_piece, (r + 1) * strips_per_piece)]
        assert all(seen) or not any(seen)
        return seen[0]

    def scores(h, t, r):
        i, j = tiles[t]
        if not piece_visible(i, j, r):
            return
        res = jnp.dot(k_ref[0, j * tk:(j + 1) * tk, head(h)], q2t_ref[h, i, :, piece(r)],
                      preferred_element_type=F32)
        for c in range(strips_per_piece):
            s_ref[h, t % 2, r * strips_per_piece + c] = res[:, c * strip:(c + 1) * strip]

    def softmax(h, t, r):
        i, j = tiles[t]
        if not piece_visible(i, j, r):
            return
        for c in range(r * strips_per_piece, (r + 1) * strips_per_piece):
            lanes = slice(c * strip, (c + 1) * strip)
            s = s_ref[h, t % 2, c]
            if visibility(i, j, c) == "some":
                rel = (lax.broadcasted_iota(jnp.int32, (tk, strip), 1)
                       - lax.broadcasted_iota(jnp.int32, (tk, strip), 0))
                s = jnp.where(rel >= j * tk - i * tq - (c * strip) % tq, s, MASK_VALUE)
            m_new = jnp.max(s, axis=0, keepdims=True)
            if j > 0:
                m_old = m_ref[h, i, :, lanes]
                m_new = jnp.maximum(m_old, m_new)
                a_ref[h, t % 2, :, lanes] = jnp.exp2(m_old - m_new)
            m_ref[h, i, :, lanes] = m_new
            p_ref[h, t % 2, c] = jnp.exp2(s - m_new).astype(BF16)

    def accumulate(h, t, r):
        i, j = tiles[t]
        if not piece_visible(i, j, r):
            return
        p = jnp.concatenate([p_ref[h, t % 2, r * strips_per_piece + c]
                             for c in range(strips_per_piece)], axis=1)
        pv = jnp.dot(vt_ref[h, j], p, preferred_element_type=F32)
        if j > 0:
            pv = a_ref[h, t % 2, :, piece(r)] * acc_ref[h, i, :, piece(r)] + pv
        acc_ref[h, i, :, piece(r)] = pv

    def finalize(i):
        for h in range(ATT_HEADS):
            inv_l = 1.0 / acc_ref[h, i, dv:dv + 1, :]
            o = (acc_ref[h, i, :dv, :tq] * inv_l[:, :tq]
                 - lam * (acc_ref[h, i, :dv, tq:] * inv_l[:, tq:]))
            r = lax.rsqrt(jnp.mean(o * o, axis=0, keepdims=True) + EPS)
            o = o * r * gsub_ref[...] * (1.0 - lambda_init)
            o_ref[0, i * tq:(i + 1) * tq, head(h)] = o.T.astype(BF16)

    heads = range(ATT_HEADS)
    for r in range(n_pieces):
        for h in heads:
            scores(h, 0, r)
    for t in range(len(tiles)):
        for r in range(n_pieces):
            for h in heads:
                if t + 1 < len(tiles):
                    scores(h, t + 1, r)
                softmax(h, t, r)
                if t > 0:
                    accumulate(h, t - 1, r)
        if t > 0 and tiles[t][0] != tiles[t - 1][0]:
            finalize(tiles[t - 1][0])
    for r in range(n_pieces):
        for h in heads:
            accumulate(h, len(tiles) - 1, r)
    finalize(n_q - 1)


def _diff_attention(q3d, kv3d, lam_params, g_sub_col, lambda_init):
    b, s, _ = q3d.shape
    h, dv = DIFF_HEADS, DIFF_V_DIM
    tq, tk = ATT_TQ, ATT_TK
    assert tq % tk == 0 and s % tq == 0 and h % ATT_HEADS == 0
    hps = ATT_HEADS
    dv_aug = dv + BF16_ROWS
    n_q = s // tq
    strips = (hps, 2, 2 * tq // LANES, tk, LANES)
    vmem = hps * (2 * 4 * s * dv * 2 + 2 * s * dv * 2 + s * dv_aug * 2
                  + 2 * tk * 2 * tq * (4 + 2) + n_q * dv_aug * 2 * tq * 4
                  + 4 * tk * 2 * tq * 4) + 8 * MIB

    def head_block(first):
        return pl.BlockSpec((1, s, hps * dv), lambda bi, hi: (bi, 0, first // hps + hi))

    return pl.pallas_call(
        functools.partial(_diff_attn_kernel, lambda_init=lambda_init),
        grid=(b, h // hps),
        in_specs=[
            head_block(0),
            head_block(0),
            head_block(h),
            pl.BlockSpec(lam_params.shape, lambda bi, hi: (0, 0)),
            pl.BlockSpec((dv, 1), lambda bi, hi: (0, 0)),
        ],
        out_specs=head_block(0),
        out_shape=jax.ShapeDtypeStruct((b, s, h * dv), BF16),
        scratch_shapes=[
            pltpu.VMEM((hps, n_q, dv, 2 * tq), BF16),
            pltpu.VMEM((hps, s // tk, dv_aug, tk), BF16),
            pltpu.VMEM(strips, F32),
            pltpu.VMEM(strips, BF16),
            pltpu.VMEM((hps, 2, 1, 2 * tq), F32),
            pltpu.VMEM((hps, n_q, 1, 2 * tq), F32),
            pltpu.VMEM((hps, n_q, dv_aug, 2 * tq), F32),
        ],
        compiler_params=_params(("parallel", "parallel"), vmem),
        name="diff_attention",
    )(q3d, kv3d, kv3d, lam_params, g_sub_col)


def kernel(x, a_norm_pre, a_norm_post, a_w_in, a_w_out, kv_norm, w_kv, b_norm_pre, b_norm_post,
           b_w_q, b_lambda, b_subln, b_w_out, ffn_norm_pre, ffn_norm_post, ffn_w_up, ffn_conv_w,
           ffn_conv_b, ffn_w_down):
    b, s, d = x.shape
    t = b * s

    half = RET_QK_DIM // 2
    inv = 1.0 / (ROPE_BASE ** jnp.linspace(0.0, 1.0, half, dtype=F32))
    ang = jnp.arange(s).astype(F32)[:, None] * inv[None, :]
    cos, sin = jnp.cos(ang), jnp.sin(ang)
    log_gamma = jnp.log1p(-jnp.power(2.0, -5.0 - jnp.arange(RET_HEADS, dtype=F32)))

    y = _retention(x, a_norm_pre[0][None], a_w_in[0].astype(BF16), cos, sin, log_gamma)
    x = _post(y.reshape(t, -1), a_w_out[0].astype(BF16), a_norm_post[0][None],
              x.reshape(t, d), "ret_out_proj")
    x = _conv_ffn(x.reshape(b, s, d), ffn_norm_pre[0][None], ffn_w_up, ffn_conv_w[0],
                  ffn_conv_b[0], ffn_w_down, ffn_norm_post[0][None], 0, "conv_ffn_0")

    layer = 1
    lambda_init = 0.8 - 0.6 * math.exp(-0.3 * layer)
    kv, q = _kvq(x.reshape(t, d), kv_norm[None], b_norm_pre[0][None],
                 w_kv.astype(BF16), b_w_q[0].astype(BF16))
    o = _diff_attention(q.reshape(b, s, -1), kv.reshape(b, s, -1), b_lambda[0],
                        b_subln[0][:, None], lambda_init)
    x = _post(o.reshape(t, -1), b_w_out[0].astype(BF16), b_norm_post[0][None],
              x.reshape(t, d), "attn_out_proj")
    x = _conv_ffn(x.reshape(b, s, d), ffn_norm_pre[1][None], ffn_w_up, ffn_conv_w[1],
                  ffn_conv_b[1], ffn_w_down, ffn_norm_post[1][None], 1, "conv_ffn_1")
    return x
```

```python
import functools
import math

import jax
import jax.numpy as jnp
from jax import lax
from jax.experimental import pallas as pl
from jax.experimental.pallas import tpu as pltpu

F32 = jnp.float32
BF16 = jnp.bfloat16

D_MODEL = 1024
RET_HEADS = 4
RET_QK_DIM = D_MODEL // RET_HEADS
RET_V_DIM = 2 * RET_QK_DIM
RET_CHUNK = 256
DIFF_HEAD_DIM = 64
DIFF_HEADS = D_MODEL // (2 * DIFF_HEAD_DIM)
DIFF_V_DIM = 2 * DIFF_HEAD_DIM
D_FF = ((8 * D_MODEL // 3 + 127) // 128) * 128
ROPE_BASE = 10000.0
EPS = 1e-6
MASK_VALUE = -1e30
LOG2_E = math.log2(math.e)

BF16_ROWS = 16
LANES = 128
MXU_COLS = 256
MIB = 1024 * 1024

RET_PROJ_ROWS = 512
POST_TM = 1024
KVQ_TM = 1024
FFN_TM = 512
FFN_TILES = 1
FFN_TF = 256
ATT_TQ = 512
ATT_TK = 256
ATT_HEADS = 2


def _params(semantics, vmem_bytes):
    return pltpu.CompilerParams(dimension_semantics=semantics,
                                vmem_limit_bytes=int(vmem_bytes))


def _resident(shape):
    zeros = (0,) * len(shape)
    return pl.BlockSpec(shape, lambda *_: zeros, pipeline_mode=pl.Buffered(1))


def _rms_scale(v):
    return lax.rsqrt(jnp.mean(v * v, axis=-1, keepdims=True) + EPS)


def _retention_kernel(lg_ref, x_ref, gain_ref, wq_ref, wk_ref, wv_ref, wg_ref, cos_ref, sin_ref,
                      o_ref, q_ref, k_ref, v_ref, sg_ref, state_ref):
    c_len, p_len = RET_CHUNK, RET_PROJ_ROWS
    seq = x_ref.shape[1]
    n_chunks = seq // c_len
    half = RET_QK_DIM // 2

    lg = lg_ref[pl.program_id(1)]
    row = lax.broadcasted_iota(jnp.int32, (c_len, c_len), 0)
    col = lax.broadcasted_iota(jnp.int32, (c_len, c_len), 1)
    rel = (row - col).astype(F32)
    decay_mask = jnp.where(rel >= 0, jnp.exp(lg * jnp.maximum(rel, 0.0)), 0.0)
    idx = lax.broadcasted_iota(jnp.int32, (c_len, 1), 0).astype(F32)
    q_decay = jnp.exp(lg * (idx + 1.0))
    k_decay = jnp.exp(lg * (c_len - 1.0 - idx))
    chunk_decay = jnp.exp(lg * jnp.full((1, 1), c_len, F32))

    def project(r0):
        rows = slice(r0, r0 + p_len)
        x = x_ref[0, rows, :]
        xn = (x * _rms_scale(x) * gain_ref[...]).astype(BF16)
        cos = cos_ref[rows, :]
        sin = sin_ref[rows, :]

        def rotary(a, scale, dst_ref):
            x1, x2 = a[:, :half], a[:, half:]
            dst_ref[rows, :half] = ((x1 * cos - x2 * sin) * scale).astype(BF16)
            dst_ref[rows, half:] = ((x1 * sin + x2 * cos) * scale).astype(BF16)

        rotary(jnp.dot(xn, wq_ref[...], preferred_element_type=F32), 1.0, q_ref)
        rotary(jnp.dot(xn, wk_ref[...], preferred_element_type=F32), RET_QK_DIM ** -0.5, k_ref)
        v_ref[rows, :] = jnp.dot(xn, wv_ref[...], preferred_element_type=F32).astype(BF16)
        half_g = 0.5 * jnp.dot(xn, wg_ref[...], preferred_element_type=F32)
        sg_ref[rows, :] = (half_g * (1.0 + jnp.tanh(half_g))).astype(BF16)

    def rows(c):
        return slice(c * c_len, (c + 1) * c_len)

    def scores(c):
        return lax.dot_general(q_ref[rows(c), :], k_ref[rows(c), :],
                               (((1,), (1,)), ((), ())), preferred_element_type=F32)

    chunks_per_tile = p_len // c_len
    project(0)
    s_next = scores(0)
    for c in range(n_chunks):
        next_tile = (c // chunks_per_tile + 1) * p_len
        if c % chunks_per_tile == 0 and next_tile < seq:
            project(next_tile)
        q = q_ref[rows(c), :]
        k = k_ref[rows(c), :]
        v = v_ref[rows(c), :]
        s_masked = (s_next * decay_mask).astype(BF16)
        out = None
        if c > 0:
            out = jnp.dot(q, state_ref[...].astype(BF16),
                          preferred_element_type=F32) * q_decay
        if c + 1 < n_chunks:
            s_next = scores(c + 1)
            kd = (k.astype(F32) * k_decay).astype(BF16)
            update = lax.dot_general(kd, v, (((0,), (0,)), ((), ())),
                                     preferred_element_type=F32)
        inner = jnp.dot(s_masked, v, preferred_element_type=F32)
        if c + 1 < n_chunks:
            state_ref[...] = update if c == 0 else state_ref[...] * chunk_decay + update
        out = inner if out is None else inner + out
        gate = sg_ref[rows(c), :].astype(F32)
        o_ref[0, rows(c), :] = (out * _rms_scale(out) * gate).astype(BF16)


def _retention(x3d, gain, w_in, cos, sin, log_gamma):
    b, s, d = x3d.shape
    dk, dv, h = RET_QK_DIM, RET_V_DIM, RET_HEADS
    assert RET_PROJ_ROWS % RET_CHUNK == 0 and s % RET_PROJ_ROWS == 0
    k_blk0 = h * dk // dk
    v_blk0 = 2 * h * dk // dv
    g_blk0 = (2 * h * dk + h * dv) // dv
    vmem = (2 * s * d * 4 + 2 * d * (2 * dk + 2 * dv) * 2 + 2 * s * dk * 4 + 2 * s * dv * 2
            + s * (2 * dk + 2 * dv) * 2 + dk * dv * 4
            + 8 * RET_PROJ_ROWS * dv * 4 + 3 * RET_PROJ_ROWS * d * 4 + 4 * MIB)
    return pl.pallas_call(
        _retention_kernel,
        grid=(b, h),
        in_specs=[
            pl.BlockSpec(memory_space=pltpu.SMEM),
            pl.BlockSpec((1, s, d), lambda i, j: (i, 0, 0)),
            pl.BlockSpec((1, d), lambda i, j: (0, 0)),
            pl.BlockSpec((d, dk), lambda i, j: (0, j)),
            pl.BlockSpec((d, dk), lambda i, j: (0, k_blk0 + j)),
            pl.BlockSpec((d, dv), lambda i, j: (0, v_blk0 + j)),
            pl.BlockSpec((d, dv), lambda i, j: (0, g_blk0 + j)),
            _resident((s, dk // 2)),
            _resident((s, dk // 2)),
        ],
        out_specs=pl.BlockSpec((1, s, dv), lambda i, j: (i, 0, j)),
        out_shape=jax.ShapeDtypeStruct((b, s, h * dv), BF16),
        scratch_shapes=[
            pltpu.VMEM((s, dk), BF16),
            pltpu.VMEM((s, dk), BF16),
            pltpu.VMEM((s, dv), BF16),
            pltpu.VMEM((s, dv), BF16),
            pltpu.VMEM((dk, dv), F32),
        ],
        compiler_params=_params(("parallel", "parallel"), vmem),
        name="retention",
    )(log_gamma, x3d, gain, w_in, w_in, w_in, w_in, cos, sin)


def _post_kernel(y_ref, w_ref, g_ref, x_ref, o_ref):
    h = jnp.dot(y_ref[...], w_ref[...], preferred_element_type=F32)
    o_ref[...] = x_ref[...] + h * _rms_scale(h) * g_ref[...]


def _post(y2d, w, gain, x2d, name):
    t, kdim = y2d.shape
    d = w.shape[1]
    vmem = (2 * POST_TM * kdim * 2 + kdim * d * 2 + 4 * POST_TM * d * 4
            + 2 * POST_TM * d * 4 + 4 * MIB)
    return pl.pallas_call(
        _post_kernel,
        grid=(t // POST_TM,),
        in_specs=[
            pl.BlockSpec((POST_TM, kdim), lambda i: (i, 0)),
            _resident((kdim, d)),
            pl.BlockSpec((1, d), lambda i: (0, 0)),
            pl.BlockSpec((POST_TM, d), lambda i: (i, 0)),
        ],
        out_specs=pl.BlockSpec((POST_TM, d), lambda i: (i, 0)),
        out_shape=jax.ShapeDtypeStruct((t, d), F32),
        compiler_params=_params(("parallel",), vmem),
        name=name,
    )(y2d, w, gain, x2d)


def _ffn_kernel(x_ref, xh_ref, gpre_ref, wup_ref, cw_ref, wdn_ref, gpost_ref, o_ref,
                xn_ref, h0_ref, h1_ref, act_ref):
    n_tiles, tm, f_dim = act_ref.shape
    tf = h0_ref.shape[2]
    n_f = f_dim // tf
    halo = BF16_ROWS
    gpre = gpre_ref[...]

    def rows(k):
        return slice(k * tm, (k + 1) * tm)

    def normalize(k):
        x = x_ref[0, rows(k), :]
        xn_ref[halo + k * tm:halo + (k + 1) * tm, :] = (x * _rms_scale(x) * gpre).astype(BF16)

    xh = xh_ref[0]
    xhn = jnp.where(pl.program_id(1) > 0, xh * _rms_scale(xh) * gpre, 0.0)
    xn_ref[:halo, :] = xhn.astype(BF16)

    def up_project(k, f, h_ref):
        xn = xn_ref[k * tm:k * tm + halo + tm, :]
        for half in range(2):
            col0 = half * f_dim + f * tf
            h_ref[half] = jnp.dot(xn, wup_ref[:, col0:col0 + tf], preferred_element_type=F32)

    def conv(h_ref, half, f):
        col0 = half * f_dim + f * tf
        cw = cw_ref[:, col0:col0 + tf]
        rows_back = lambda back: pl.ds(halo - back, tm)
        return (cw[3:4] + cw[2:3] * h_ref[half, rows_back(0), :]
                + cw[1:2] * h_ref[half, rows_back(1), :] + cw[0:1] * h_ref[half, rows_back(2), :])

    def gate(k, f, h_ref):
        half_g = conv(h_ref, 0, f)
        u = conv(h_ref, 1, f)
        act_ref[k, :, f * tf:(f + 1) * tf] = (half_g * (1.0 + jnp.tanh(half_g)) * u).astype(BF16)

    def finish(k):
        h = jnp.dot(act_ref[k], wdn_ref[...], preferred_element_type=F32)
        o_ref[0, rows(k), :] = x_ref[0, rows(k), :] + h * _rms_scale(h) * gpost_ref[...]

    h_refs = (h0_ref, h1_ref)
    order = [(k, f) for k in range(n_tiles) for f in range(n_f)]
    normalize(0)
    up_project(*order[0], h_refs[0])
    for n, (k, f) in enumerate(order):
        if n + 1 < len(order):
            up_project(*order[n + 1], h_refs[(n + 1) % 2])
        if f == 1 and k + 1 < n_tiles:
            normalize(k + 1)
        gate(k, f, h_refs[n % 2])
        if f == n_f - 1:
            finish(k)


def _conv_ffn(x3d, gpre, w_up, conv_w, conv_b, w_down, gpost, name):
    b, s, d = x3d.shape
    f_dim = w_down.shape[0]
    assert f_dim % FFN_TF == 0
    tm, halo, n_tiles = FFN_TM, BF16_ROWS, FFN_TILES
    rows = n_tiles * tm
    wup = w_up.astype(BF16)
    wdn = w_down.astype(BF16)
    cw = jnp.concatenate([conv_w, conv_b[None, :], jnp.zeros((4, 2 * f_dim), F32)], axis=0)
    cw = cw * jnp.where(jnp.arange(2 * f_dim) < f_dim, 0.5, 1.0)[None, :]
    halo_blocks = rows // halo
    vmem = (2 * f_dim * d * 2 + f_dim * d * 2 + 4 * rows * d * 4 + (rows + halo) * d * 2
            + rows * f_dim * 2 + 4 * (tm + halo) * FFN_TF * 4 + 6 * tm * d * 4 + 4 * MIB)
    return pl.pallas_call(
        _ffn_kernel,
        grid=(b, s // rows),
        in_specs=[
            pl.BlockSpec((1, rows, d), lambda i, j: (i, j, 0)),
            pl.BlockSpec((1, halo, d), lambda i, j: (i, jnp.maximum(j * halo_blocks - 1, 0), 0)),
            pl.BlockSpec((1, d), lambda i, j: (0, 0)),
            _resident((d, 2 * f_dim)),
            _resident((8, 2 * f_dim)),
            _resident((f_dim, d)),
            pl.BlockSpec((1, d), lambda i, j: (0, 0)),
        ],
        out_specs=pl.BlockSpec((1, rows, d), lambda i, j: (i, j, 0)),
        out_shape=jax.ShapeDtypeStruct((b, s, d), F32),
        scratch_shapes=[
            pltpu.VMEM((rows + halo, d), BF16),
            pltpu.VMEM((2, tm + halo, FFN_TF), F32),
            pltpu.VMEM((2, tm + halo, FFN_TF), F32),
            pltpu.VMEM((n_tiles, tm, f_dim), BF16),
        ],
        compiler_params=_params(("parallel", "parallel"), vmem),
        name=name,
    )(x3d, x3d, gpre, wup, cw, wdn, gpost)


def _kvq_kernel(x_ref, gkv_ref, gq_ref, wkv_ref, wq_ref, kv_ref, q_ref):
    x = x_ref[...]
    xr = x * _rms_scale(x)
    kv = jnp.dot((xr * gkv_ref[...]).astype(BF16), wkv_ref[...], preferred_element_type=F32)
    kv_ref[...] = kv.astype(BF16)
    q = jnp.dot((xr * gq_ref[...]).astype(BF16), wq_ref[...], preferred_element_type=F32)
    q_ref[...] = (q * (DIFF_HEAD_DIM ** -0.5 * LOG2_E)).astype(BF16)


def _kvq(x2d, g_kv, g_q, w_kv, w_q):
    t, d = x2d.shape
    n_kv, n_q = w_kv.shape[1], w_q.shape[1]
    tm = KVQ_TM
    vmem = (2 * tm * d * 4 + d * (n_kv + n_q) * 2 + 2 * tm * (n_kv + n_q) * 2
            + 6 * tm * d * 4 + 2 * tm * (n_kv + n_q) * 4 + 4 * MIB)
    return pl.pallas_call(
        _kvq_kernel,
        grid=(t // tm,),
        in_specs=[
            pl.BlockSpec((tm, d), lambda i: (i, 0)),
            pl.BlockSpec((1, d), lambda i: (0, 0)),
            pl.BlockSpec((1, d), lambda i: (0, 0)),
            _resident((d, n_kv)),
            _resident((d, n_q)),
        ],
        out_specs=[pl.BlockSpec((tm, n_kv), lambda i: (i, 0)),
                   pl.BlockSpec((tm, n_q), lambda i: (i, 0))],
        out_shape=[jax.ShapeDtypeStruct((t, n_kv), BF16),
                   jax.ShapeDtypeStruct((t, n_q), BF16)],
        compiler_params=_params(("parallel",), vmem),
        name="kv_q_proj",
    )(x2d, g_kv, g_q, w_kv, w_q)


def _diff_attn_kernel(q_ref, k_ref, v_ref, lam_ref, gsub_ref, o_ref,
                      q2t_ref, vt_ref, s_ref, p_ref, a_ref, m_ref, acc_ref, *, lambda_init):
    tq, tk, d, dv = ATT_TQ, ATT_TK, DIFF_HEAD_DIM, DIFF_V_DIM
    strip = LANES
    n_q = q2t_ref.shape[1]
    n_kv = vt_ref.shape[1]
    n_pieces = 2 * tq // MXU_COLS
    strips_per_piece = MXU_COLS // strip

    def head(h):
        return slice(h * dv, (h + 1) * dv)

    for h in range(ATT_HEADS):
        for j in range(n_kv):
            vt_ref[h, j, :dv, :] = (
                v_ref[0, j * tk:(j + 1) * tk, head(h)].astype(F32).T.astype(BF16))
    vt_ref[:, :, dv:, :] = jnp.ones((ATT_HEADS, n_kv, BF16_ROWS, tk), BF16)

    for h in range(ATT_HEADS):
        for i in range(n_q):
            qt = q_ref[0, i * tq:(i + 1) * tq, head(h)].astype(F32).T
            feat = lax.broadcasted_iota(jnp.int32, qt.shape, 0)
            q2t_ref[h, i, :, :tq] = jnp.where(feat < d, qt, 0.0).astype(BF16)
            q2t_ref[h, i, :, tq:] = jnp.where(feat >= d, qt, 0.0).astype(BF16)

    lp = lam_ref[...]
    lam = (jnp.exp(jnp.sum(lp[0:1] * lp[1:2], axis=-1, keepdims=True))
           - jnp.exp(jnp.sum(lp[2:3] * lp[3:4], axis=-1, keepdims=True)) + lambda_init)

    tiles = [(i, j) for i in range(n_q) for j in range((i + 1) * tq // tk)]

    def piece(r):
        return slice(r * MXU_COLS, (r + 1) * MXU_COLS)

    def visibility(i, j, c):
        q_first = i * tq + (c * strip) % tq
        k_first = j * tk
        if k_first > q_first + strip - 1:
            return "none"
        return "some" if k_first + tk - 1 > q_first else "all"

    def piece_visible(i, j, r):
        seen = [visibility(i, j, c) != "none"
                for c in range(r * strips_per_piece, (r + 1) * strips_per_piece)]
        assert all(seen) or not any(seen)
        return seen[0]

    def scores(h, t, r):
        i, j = tiles[t]
        if not piece_visible(i, j, r):
            return
        res = jnp.dot(k_ref[0, j * tk:(j + 1) * tk, head(h)], q2t_ref[h, i, :, piece(r)],
                      preferred_element_type=F32)
        for c in range(strips_per_piece):
            s_ref[h, t % 2, r * strips_per_piece + c] = res[:, c * strip:(c + 1) * strip]

    def softmax(h, t, r):
        i, j = tiles[t]
        if not piece_visible(i, j, r):
            return
        for c in range(r * strips_per_piece, (r + 1) * strips_per_piece):
            lanes = slice(c * strip, (c + 1) * strip)
            s = s_ref[h, t % 2, c]
            if visibility(i, j, c) == "some":
                rel = (lax.broadcasted_iota(jnp.int32, (tk, strip), 1)
                       - lax.broadcasted_iota(jnp.int32, (tk, strip), 0))
                s = jnp.where(rel >= j * tk - i * tq - (c * strip) % tq, s, MASK_VALUE)
            m_new = jnp.max(s, axis=0, keepdims=True)
            if j > 0:
                m_old = m_ref[h, i, :, lanes]
                m_new = jnp.maximum(m_old, m_new)
                a_ref[h, t % 2, :, lanes] = jnp.exp2(m_old - m_new)
            m_ref[h, i, :, lanes] = m_new
            p_ref[h, t % 2, c] = jnp.exp2(s - m_new).astype(BF16)

    def accumulate(h, t, r):
        i, j = tiles[t]
        if not piece_visible(i, j, r):
            return
        p = jnp.concatenate([p_ref[h, t % 2, r * strips_per_piece + c]
                             for c in range(strips_per_piece)], axis=1)
        pv = jnp.dot(vt_ref[h, j], p, preferred_element_type=F32)
        if j > 0:
            pv = a_ref[h, t % 2, :, piece(r)] * acc_ref[h, i, :, piece(r)] + pv
        acc_ref[h, i, :, piece(r)] = pv

    def finalize(i):
        for h in range(ATT_HEADS):
            inv_l = 1.0 / acc_ref[h, i, dv:dv + 1, :]
            o = (acc_ref[h, i, :dv, :tq] * inv_l[:, :tq]
                 - lam * (acc_ref[h, i, :dv, tq:] * inv_l[:, tq:]))
            r = lax.rsqrt(jnp.mean(o * o, axis=0, keepdims=True) + EPS)
            o = o * r * gsub_ref[...] * (1.0 - lambda_init)
            o_ref[0, i * tq:(i + 1) * tq, head(h)] = o.T.astype(BF16)

    heads = range(ATT_HEADS)
    for r in range(n_pieces):
        for h in heads:
            scores(h, 0, r)
    for t in range(len(tiles)):
        for r in range(n_pieces):
            for h in heads:
                if t + 1 < len(tiles):
                    scores(h, t + 1, r)
                softmax(h, t, r)
                if t > 0:
                    accumulate(h, t - 1, r)
        if t > 0 and tiles[t][0] != tiles[t - 1][0]:
            finalize(tiles[t - 1][0])
    for r in range(n_pieces):
        for h in heads:
            accumulate(h, len(tiles) - 1, r)
    finalize(n_q - 1)


def _diff_attention(q3d, kv3d, lam_params, g_sub_col, lambda_init):
    b, s, _ = q3d.shape
    h, dv = DIFF_HEADS, DIFF_V_DIM
    tq, tk = ATT_TQ, ATT_TK
    assert tq % tk == 0 and s % tq == 0 and h % ATT_HEADS == 0
    hps = ATT_HEADS
    dv_aug = dv + BF16_ROWS
    n_q = s // tq
    strips = (hps, 2, 2 * tq // LANES, tk, LANES)
    vmem = hps * (2 * 4 * s * dv * 2 + 2 * s * dv * 2 + s * dv_aug * 2
                  + 2 * tk * 2 * tq * (4 + 2) + n_q * dv_aug * 2 * tq * 4
                  + 4 * tk * 2 * tq * 4) + 8 * MIB

    def head_block(first):
        return pl.BlockSpec((1, s, hps * dv), lambda bi, hi: (bi, 0, first // hps + hi))

    return pl.pallas_call(
        functools.partial(_diff_attn_kernel, lambda_init=lambda_init),
        grid=(b, h // hps),
        in_specs=[
            head_block(0),
            head_block(0),
            head_block(h),
            pl.BlockSpec(lam_params.shape, lambda bi, hi: (0, 0)),
            pl.BlockSpec((dv, 1), lambda bi, hi: (0, 0)),
        ],
        out_specs=head_block(0),
        out_shape=jax.ShapeDtypeStruct((b, s, h * dv), BF16),
        scratch_shapes=[
            pltpu.VMEM((hps, n_q, dv, 2 * tq), BF16),
            pltpu.VMEM((hps, s // tk, dv_aug, tk), BF16),
            pltpu.VMEM(strips, F32),
            pltpu.VMEM(strips, BF16),
            pltpu.VMEM((hps, 2, 1, 2 * tq), F32),
            pltpu.VMEM((hps, n_q, 1, 2 * tq), F32),
            pltpu.VMEM((hps, n_q, dv_aug, 2 * tq), F32),
        ],
        compiler_params=_params(("parallel", "parallel"), vmem),
        name="diff_attention",
    )(q3d, kv3d, kv3d, lam_params, g_sub_col)


def kernel(x, a_norm_pre, a_norm_post, a_w_in, a_w_out, kv_norm, w_kv, b_norm_pre, b_norm_post,
           b_w_q, b_lambda, b_subln, b_w_out, ffn_norm_pre, ffn_norm_post, ffn_w_up, ffn_conv_w,
           ffn_conv_b, ffn_w_down):
    b, s, d = x.shape
    t = b * s

    half = RET_QK_DIM // 2
    inv = 1.0 / (ROPE_BASE ** jnp.linspace(0.0, 1.0, half, dtype=F32))
    ang = jnp.arange(s).astype(F32)[:, None] * inv[None, :]
    cos, sin = jnp.cos(ang), jnp.sin(ang)
    log_gamma = jnp.log1p(-jnp.power(2.0, -5.0 - jnp.arange(RET_HEADS, dtype=F32)))

    y = _retention(x, a_norm_pre[0][None], a_w_in[0].astype(BF16), cos, sin, log_gamma)
    x = _post(y.reshape(t, -1), a_w_out[0].astype(BF16), a_norm_post[0][None],
              x.reshape(t, d), "ret_out_proj")
    x = _conv_ffn(x.reshape(b, s, d), ffn_norm_pre[0][None], ffn_w_up[0], ffn_conv_w[0],
                  ffn_conv_b[0], ffn_w_down[0], ffn_norm_post[0][None], "conv_ffn_0")

    layer = 1
    lambda_init = 0.8 - 0.6 * math.exp(-0.3 * layer)
    kv, q = _kvq(x.reshape(t, d), kv_norm[None], b_norm_pre[0][None],
                 w_kv.astype(BF16), b_w_q[0].astype(BF16))
    o = _diff_attention(q.reshape(b, s, -1), kv.reshape(b, s, -1), b_lambda[0],
                        b_subln[0][:, None], lambda_init)
    x = _post(o.reshape(t, -1), b_w_out[0].astype(BF16), b_norm_post[0][None],
              x.reshape(t, d), "attn_out_proj")
    x = _conv_ffn(x.reshape(b, s, d), ffn_norm_pre[1][None], ffn_w_up[1], ffn_conv_w[1],
                  ffn_conv_b[1], ffn_w_down[1], ffn_norm_post[1][None], "conv_ffn_1")
    return x
```

```python
import functools
import math

import jax
import jax.numpy as jnp
from jax import lax
from jax.experimental import pallas as pl
from jax.experimental.pallas import tpu as pltpu

F32 = jnp.float32
BF16 = jnp.bfloat16

D_MODEL = 1024
RET_HEADS = 4
RET_QK_DIM = D_MODEL // RET_HEADS
RET_V_DIM = 2 * RET_QK_DIM
RET_CHUNK = 256
DIFF_HEAD_DIM = 64
DIFF_HEADS = D_MODEL // (2 * DIFF_HEAD_DIM)
DIFF_V_DIM = 2 * DIFF_HEAD_DIM
D_FF = ((8 * D_MODEL // 3 + 127) // 128) * 128
ROPE_BASE = 10000.0
EPS = 1e-6
MASK_VALUE = -1e30
LOG2_E = math.log2(math.e)

BF16_ROWS = 16
F32_ROWS = 8
LANES = 128
MXU_COLS = 256
MIB = 1024 * 1024

RET_PROJ_ROWS = 512
POST_TM = 1024
KVQ_TM = 1024
FFN_TM = 512
FFN_TILES = 1
FFN_TF = 256
ATT_TQ = 512
ATT_TK = 256
ATT_HEADS = 1


def _params(semantics, vmem_bytes):
    return pltpu.CompilerParams(dimension_semantics=semantics,
                                vmem_limit_bytes=int(vmem_bytes))


def _resident(shape):
    zeros = (0,) * len(shape)
    return pl.BlockSpec(shape, lambda *_: zeros, pipeline_mode=pl.Buffered(1))


def _rms_scale(v):
    return lax.rsqrt(jnp.mean(v * v, axis=-1, keepdims=True) + EPS)


def _retention_kernel(lg_ref, x_ref, gain_ref, wq_ref, wk_ref, wv_ref, wg_ref, cos_ref, sin_ref,
                      o_ref, xn_ref, q_ref, k_ref, v_ref, sg_ref, state_ref):
    c_len, p_len = RET_CHUNK, RET_PROJ_ROWS
    seq = x_ref.shape[1]
    n_chunks = seq // c_len
    half = RET_QK_DIM // 2

    @pl.when(pl.program_id(1) == 0)
    def _():
        for r0 in range(0, seq, p_len):
            x = x_ref[0, r0:r0 + p_len, :]
            xn_ref[r0:r0 + p_len, :] = (x * _rms_scale(x) * gain_ref[...]).astype(BF16)

    lg = lg_ref[pl.program_id(1)]
    row = lax.broadcasted_iota(jnp.int32, (c_len, c_len), 0)
    col = lax.broadcasted_iota(jnp.int32, (c_len, c_len), 1)
    rel = (row - col).astype(F32)
    decay_mask = jnp.where(rel >= 0, jnp.exp(lg * jnp.maximum(rel, 0.0)), 0.0)
    idx = lax.broadcasted_iota(jnp.int32, (c_len, 1), 0).astype(F32)
    q_decay = jnp.exp(lg * (idx + 1.0))
    k_decay = jnp.exp(lg * (c_len - 1.0 - idx))
    chunk_decay = jnp.exp(lg * jnp.full((1, 1), c_len, F32))

    def project(r0):
        rows = slice(r0, r0 + p_len)
        xn = xn_ref[rows, :]
        cos = cos_ref[rows, :]
        sin = sin_ref[rows, :]

        def rotary(a, scale, dst_ref):
            x1, x2 = a[:, :half], a[:, half:]
            dst_ref[rows, :half] = ((x1 * cos - x2 * sin) * scale).astype(BF16)
            dst_ref[rows, half:] = ((x1 * sin + x2 * cos) * scale).astype(BF16)

        rotary(jnp.dot(xn, wq_ref[...], preferred_element_type=F32), 1.0, q_ref)
        rotary(jnp.dot(xn, wk_ref[...], preferred_element_type=F32), RET_QK_DIM ** -0.5, k_ref)
        v_ref[rows, :] = jnp.dot(xn, wv_ref[...], preferred_element_type=F32).astype(BF16)
        half_g = 0.5 * jnp.dot(xn, wg_ref[...], preferred_element_type=F32)
        sg_ref[rows, :] = (half_g * (1.0 + jnp.tanh(half_g))).astype(BF16)

    def rows(c):
        return slice(c * c_len, (c + 1) * c_len)

    def scores(c):
        return lax.dot_general(q_ref[rows(c), :], k_ref[rows(c), :],
                               (((1,), (1,)), ((), ())), preferred_element_type=F32)

    chunks_per_tile = p_len // c_len
    project(0)
    s_next = scores(0)
    for c in range(n_chunks):
        next_tile = (c // chunks_per_tile + 1) * p_len
        if c % chunks_per_tile == 0 and next_tile < seq:
            project(next_tile)
        q = q_ref[rows(c), :]
        k = k_ref[rows(c), :]
        v = v_ref[rows(c), :]
        s_masked = (s_next * decay_mask).astype(BF16)
        out = None
        if c > 0:
            out = jnp.dot(q, state_ref[...].astype(BF16),
                          preferred_element_type=F32) * q_decay
        if c + 1 < n_chunks:
            s_next = scores(c + 1)
            kd = (k.astype(F32) * k_decay).astype(BF16)
            update = lax.dot_general(kd, v, (((0,), (0,)), ((), ())),
                                     preferred_element_type=F32)
        inner = jnp.dot(s_masked, v, preferred_element_type=F32)
        if c + 1 < n_chunks:
            state_ref[...] = update if c == 0 else state_ref[...] * chunk_decay + update
        out = inner if out is None else inner + out
        gate = sg_ref[rows(c), :].astype(F32)
        o_ref[0, rows(c), :] = (out * _rms_scale(out) * gate).astype(BF16)


def _retention(x3d, gain, w_in, cos, sin, log_gamma):
    b, s, d = x3d.shape
    dk, dv, h = RET_QK_DIM, RET_V_DIM, RET_HEADS
    assert RET_PROJ_ROWS % RET_CHUNK == 0 and s % RET_PROJ_ROWS == 0
    k_blk0 = h * dk // dk
    v_blk0 = 2 * h * dk // dv
    g_blk0 = (2 * h * dk + h * dv) // dv
    vmem = (2 * s * d * 4 + 2 * d * (2 * dk + 2 * dv) * 2 + 2 * s * dk * 4 + 2 * s * dv * 2
            + s * d * 2 + s * (2 * dk + 2 * dv) * 2 + dk * dv * 4
            + 8 * RET_PROJ_ROWS * dv * 4 + 4 * MIB)
    return pl.pallas_call(
        _retention_kernel,
        grid=(b, h),
        in_specs=[
            pl.BlockSpec(memory_space=pltpu.SMEM),
            pl.BlockSpec((1, s, d), lambda i, j: (i, 0, 0)),
            pl.BlockSpec((1, d), lambda i, j: (0, 0)),
            pl.BlockSpec((d, dk), lambda i, j: (0, j)),
            pl.BlockSpec((d, dk), lambda i, j: (0, k_blk0 + j)),
            pl.BlockSpec((d, dv), lambda i, j: (0, v_blk0 + j)),
            pl.BlockSpec((d, dv), lambda i, j: (0, g_blk0 + j)),
            _resident((s, dk // 2)),
            _resident((s, dk // 2)),
        ],
        out_specs=pl.BlockSpec((1, s, dv), lambda i, j: (i, 0, j)),
        out_shape=jax.ShapeDtypeStruct((b, s, h * dv), BF16),
        scratch_shapes=[
            pltpu.VMEM((s, d), BF16),
            pltpu.VMEM((s, dk), BF16),
            pltpu.VMEM((s, dk), BF16),
            pltpu.VMEM((s, dv), BF16),
            pltpu.VMEM((s, dv), BF16),
            pltpu.VMEM((dk, dv), F32),
        ],
        compiler_params=_params(("parallel", "arbitrary"), vmem),
        name="retention",
    )(log_gamma, x3d, gain, w_in, w_in, w_in, w_in, cos, sin)


def _post_kernel(y_ref, w_ref, g_ref, x_ref, o_ref):
    h = jnp.dot(y_ref[...], w_ref[...], preferred_element_type=F32)
    o_ref[...] = x_ref[...] + h * _rms_scale(h) * g_ref[...]


def _post(y2d, w, gain, x2d, name):
    t, kdim = y2d.shape
    d = w.shape[1]
    vmem = (2 * POST_TM * kdim * 2 + kdim * d * 2 + 4 * POST_TM * d * 4
            + 2 * POST_TM * d * 4 + 4 * MIB)
    return pl.pallas_call(
        _post_kernel,
        grid=(t // POST_TM,),
        in_specs=[
            pl.BlockSpec((POST_TM, kdim), lambda i: (i, 0)),
            _resident((kdim, d)),
            pl.BlockSpec((1, d), lambda i: (0, 0)),
            pl.BlockSpec((POST_TM, d), lambda i: (i, 0)),
        ],
        out_specs=pl.BlockSpec((POST_TM, d), lambda i: (i, 0)),
        out_shape=jax.ShapeDtypeStruct((t, d), F32),
        compiler_params=_params(("parallel",), vmem),
        name=name,
    )(y2d, w, gain, x2d)


def _ffn_kernel(x_ref, gpre_ref, wup_ref, cw_ref, wdn_ref, gpost_ref, o_ref,
                xn_ref, h0_ref, h1_ref, act_ref, tail_ref):
    n_tiles, tm, f_dim = act_ref.shape
    tf = h0_ref.shape[2]
    n_f = f_dim // tf
    halo = F32_ROWS
    gpre = gpre_ref[...]

    def rows(k):
        return slice(k * tm, (k + 1) * tm)

    def normalize(k):
        x = x_ref[0, rows(k), :]
        xn_ref[rows(k), :] = (x * _rms_scale(x) * gpre).astype(BF16)

    @pl.when(pl.program_id(1) == 0)
    def _():
        tail_ref[...] = jnp.zeros_like(tail_ref)

    def up_project(k, f, h_ref):
        xn = xn_ref[rows(k), :]
        for half in range(2):
            col0 = half * f_dim + f * tf
            res = jnp.dot(xn, wup_ref[:, col0:col0 + tf], preferred_element_type=F32)
            h_ref[half, :halo, :] = tail_ref[2 * f + half]
            h_ref[half, halo:, :] = res
            tail_ref[2 * f + half] = res[tm - halo:, :]

    def conv(h_ref, half, f):
        col0 = half * f_dim + f * tf
        cw = cw_ref[:, col0:col0 + tf]
        rows_back = lambda back: pl.ds(halo - back, tm)
        return (cw[3:4] + cw[2:3] * h_ref[half, rows_back(0), :]
                + cw[1:2] * h_ref[half, rows_back(1), :] + cw[0:1] * h_ref[half, rows_back(2), :])

    def gate(k, f, h_ref):
        half_g = conv(h_ref, 0, f)
        u = conv(h_ref, 1, f)
        act_ref[k, :, f * tf:(f + 1) * tf] = (half_g * (1.0 + jnp.tanh(half_g)) * u).astype(BF16)

    def finish(k):
        h = jnp.dot(act_ref[k], wdn_ref[...], preferred_element_type=F32)
        o_ref[0, rows(k), :] = x_ref[0, rows(k), :] + h * _rms_scale(h) * gpost_ref[...]

    h_refs = (h0_ref, h1_ref)
    order = [(k, f) for k in range(n_tiles) for f in range(n_f)]
    normalize(0)
    up_project(*order[0], h_refs[0])
    for n, (k, f) in enumerate(order):
        if n + 1 < len(order):
            up_project(*order[n + 1], h_refs[(n + 1) % 2])
        if f == 1 and k + 1 < n_tiles:
            normalize(k + 1)
        gate(k, f, h_refs[n % 2])
        if f == n_f - 1:
            finish(k)


def _conv_ffn(x3d, gpre, w_up, conv_w, conv_b, w_down, gpost, name):
    b, s, d = x3d.shape
    f_dim = w_down.shape[0]
    assert f_dim % FFN_TF == 0
    tm, halo, n_tiles = FFN_TM, F32_ROWS, FFN_TILES
    assert n_tiles == 1
    rows = n_tiles * tm
    wup = w_up.astype(BF16)
    wdn = w_down.astype(BF16)
    cw = jnp.concatenate([conv_w, conv_b[None, :], jnp.zeros((4, 2 * f_dim), F32)], axis=0)
    cw = cw * jnp.where(jnp.arange(2 * f_dim) < f_dim, 0.5, 1.0)[None, :]
    vmem = (2 * f_dim * d * 2 + f_dim * d * 2 + 4 * rows * d * 4 + (rows + halo) * d * 2
            + rows * f_dim * 2 + 4 * (tm + halo) * FFN_TF * 4 + 6 * tm * d * 4 + 4 * MIB)
    return pl.pallas_call(
        _ffn_kernel,
        grid=(b, s // rows),
        in_specs=[
            pl.BlockSpec((1, rows, d), lambda i, j: (i, j, 0)),
            pl.BlockSpec((1, d), lambda i, j: (0, 0)),
            _resident((d, 2 * f_dim)),
            _resident((8, 2 * f_dim)),
            _resident((f_dim, d)),
            pl.BlockSpec((1, d), lambda i, j: (0, 0)),
        ],
        out_specs=pl.BlockSpec((1, rows, d), lambda i, j: (i, j, 0)),
        out_shape=jax.ShapeDtypeStruct((b, s, d), F32),
        scratch_shapes=[
            pltpu.VMEM((rows, d), BF16),
            pltpu.VMEM((2, tm + halo, FFN_TF), F32),
            pltpu.VMEM((2, tm + halo, FFN_TF), F32),
            pltpu.VMEM((n_tiles, tm, f_dim), BF16),
            pltpu.VMEM((2 * f_dim // FFN_TF, halo, FFN_TF), F32),
        ],
        compiler_params=_params(("parallel", "arbitrary"), vmem),
        name=name,
    )(x3d, gpre, wup, cw, wdn, gpost)


def _kvq_kernel(x_ref, gkv_ref, gq_ref, wkv_ref, wq_ref, kv_ref, q_ref):
    x = x_ref[...]
    xr = x * _rms_scale(x)
    kv = jnp.dot((xr * gkv_ref[...]).astype(BF16), wkv_ref[...], preferred_element_type=F32)
    kv_ref[...] = kv.astype(BF16)
    q = jnp.dot((xr * gq_ref[...]).astype(BF16), wq_ref[...], preferred_element_type=F32)
    q_ref[...] = (q * (DIFF_HEAD_DIM ** -0.5 * LOG2_E)).astype(BF16)


def _kvq(x2d, g_kv, g_q, w_kv, w_q):
    t, d = x2d.shape
    n_kv, n_q = w_kv.shape[1], w_q.shape[1]
    tm = KVQ_TM
    vmem = (2 * tm * d * 4 + d * (n_kv + n_q) * 2 + 2 * tm * (n_kv + n_q) * 2
            + 6 * tm * d * 4 + 2 * tm * (n_kv + n_q) * 4 + 4 * MIB)
    return pl.pallas_call(
        _kvq_kernel,
        grid=(t // tm,),
        in_specs=[
            pl.BlockSpec((tm, d), lambda i: (i, 0)),
            pl.BlockSpec((1, d), lambda i: (0, 0)),
            pl.BlockSpec((1, d), lambda i: (0, 0)),
            _resident((d, n_kv)),
            _resident((d, n_q)),
        ],
        out_specs=[pl.BlockSpec((tm, n_kv), lambda i: (i, 0)),
                   pl.BlockSpec((tm, n_q), lambda i: (i, 0))],
        out_shape=[jax.ShapeDtypeStruct((t, n_kv), BF16),
                   jax.ShapeDtypeStruct((t, n_q), BF16)],
        compiler_params=_params(("parallel",), vmem),
        name="kv_q_proj",
    )(x2d, g_kv, g_q, w_kv, w_q)


def _diff_attn_kernel(q_ref, k_ref, v_ref, lam_ref, gsub_ref, o_ref,
                      q2t_ref, vt_ref, s_ref, p_ref, a_ref, m_ref, acc_ref, *, lambda_init):
    tq, tk, d, dv = ATT_TQ, ATT_TK, DIFF_HEAD_DIM, DIFF_V_DIM
    strip = LANES
    n_q = q2t_ref.shape[1]
    n_kv = vt_ref.shape[1]
    n_pieces = 2 * tq // MXU_COLS
    strips_per_piece = MXU_COLS // strip

    def head(h):
        return slice(h * dv, (h + 1) * dv)

    for h in range(ATT_HEADS):
        for j in range(n_kv):
            vt_ref[h, j, :dv, :] = (
                v_ref[0, j * tk:(j + 1) * tk, head(h)].astype(F32).T.astype(BF16))
    vt_ref[:, :, dv:, :] = jnp.ones((ATT_HEADS, n_kv, BF16_ROWS, tk), BF16)

    for h in range(ATT_HEADS):
        for i in range(n_q):
            qt = q_ref[0, i * tq:(i + 1) * tq, head(h)].astype(F32).T
            feat = lax.broadcasted_iota(jnp.int32, qt.shape, 0)
            q2t_ref[h, i, :, :tq] = jnp.where(feat < d, qt, 0.0).astype(BF16)
            q2t_ref[h, i, :, tq:] = jnp.where(feat >= d, qt, 0.0).astype(BF16)

    lp = lam_ref[...]
    lam = (jnp.exp(jnp.sum(lp[0:1] * lp[1:2], axis=-1, keepdims=True))
           - jnp.exp(jnp.sum(lp[2:3] * lp[3:4], axis=-1, keepdims=True)) + lambda_init)

    tiles = [(i, j) for i in range(n_q) for j in range((i + 1) * tq // tk)]

    def piece(r):
        return slice(r * MXU_COLS, (r + 1) * MXU_COLS)

    def visibility(i, j, c):
        q_first = i * tq + (c * strip) % tq
        k_first = j * tk
        if k_first > q_first + strip - 1:
            return "none"
        return "some" if k_first + tk - 1 > q_first else "all"

    def piece_visible(i, j, r):
        seen = [visibility(i, j, c) != "none"
                for c in range(r * strips_per_piece, (r + 1) * strips_per_piece)]
        assert all(seen) or not any(seen)
        return seen[0]

    def scores(h, t, r):
        i, j = tiles[t]
        if not piece_visible(i, j, r):
            return
        res = jnp.dot(k_ref[0, j * tk:(j + 1) * tk, head(h)], q2t_ref[h, i, :, piece(r)],
                      preferred_element_type=F32)
        for c in range(strips_per_piece):
            s_ref[h, t % 2, r * strips_per_piece + c] = res[:, c * strip:(c + 1) * strip]

    def softmax(h, t, r):
        i, j = tiles[t]
        if not piece_visible(i, j, r):
            return
        for c in range(r * strips_per_piece, (r + 1) * strips_per_piece):
            lanes = slice(c * strip, (c + 1) * strip)
            s = s_ref[h, t % 2, c]
            if visibility(i, j, c) == "some":
                rel = (lax.broadcasted_iota(jnp.int32, (tk, strip), 1)
                       - lax.broadcasted_iota(jnp.int32, (tk, strip), 0))
                s = jnp.where(rel >= j * tk - i * tq - (c * strip) % tq, s, MASK_VALUE)
            m_new = jnp.max(s, axis=0, keepdims=True)
            if j > 0:
                m_old = m_ref[h, i, :, lanes]
                m_new = jnp.maximum(m_old, m_new)
                a_ref[h, t % 2, :, lanes] = jnp.exp2(m_old - m_new)
            m_ref[h, i, :, lanes] = m_new
            p_ref[h, t % 2, c] = jnp.exp2(s - m_new).astype(BF16)

    def accumulate(h, t, r):
        i, j = tiles[t]
        if not piece_visible(i, j, r):
            return
        p = jnp.concatenate([p_ref[h, t % 2, r * strips_per_piece + c]
                             for c in range(strips_per_piece)], axis=1)
        pv = jnp.dot(vt_ref[h, j], p, preferred_element_type=F32)
        if j > 0:
            pv = a_ref[h, t % 2, :, piece(r)] * acc_ref[h, i, :, piece(r)] + pv
        acc_ref[h, i, :, piece(r)] = pv

    def finalize(i):
        for h in range(ATT_HEADS):
            inv_l = 1.0 / acc_ref[h, i, dv:dv + 1, :]
            o = (acc_ref[h, i, :dv, :tq] * inv_l[:, :tq]
                 - lam * (acc_ref[h, i, :dv, tq:] * inv_l[:, tq:]))
            r = lax.rsqrt(jnp.mean(o * o, axis=0, keepdims=True) + EPS)
            o = o * r * gsub_ref[...] * (1.0 - lambda_init)
            o_ref[0, i * tq:(i + 1) * tq, head(h)] = o.T.astype(BF16)

    heads = range(ATT_HEADS)
    for r in range(n_pieces):
        for h in heads:
            scores(h, 0, r)
    for t in range(len(tiles)):
        for r in range(n_pieces):
            for h in heads:
                if t + 1 < len(tiles):
                    scores(h, t + 1, r)
                softmax(h, t, r)
                if t > 0:
                    accumulate(h, t - 1, r)
        if t > 0 and tiles[t][0] != tiles[t - 1][0]:
            finalize(tiles[t - 1][0])
    for r in range(n_pieces):
        for h in heads:
            accumulate(h, len(tiles) - 1, r)
    finalize(n_q - 1)


def _diff_attention(q3d, kv3d, lam_params, g_sub_col, lambda_init):
    b, s, _ = q3d.shape
    h, dv = DIFF_HEADS, DIFF_V_DIM
    tq, tk = ATT_TQ, ATT_TK
    assert tq % tk == 0 and s % tq == 0 and h % ATT_HEADS == 0
    hps = ATT_HEADS
    dv_aug = dv + BF16_ROWS
    n_q = s // tq
    strips = (hps, 2, 2 * tq // LANES, tk, LANES)
    vmem = hps * (2 * 4 * s * dv * 2 + 2 * s * dv * 2 + s * dv_aug * 2
                  + 2 * tk * 2 * tq * (4 + 2) + n_q * dv_aug * 2 * tq * 4
                  + 4 * tk * 2 * tq * 4) + 8 * MIB

    def head_block(first):
        return pl.BlockSpec((1, s, hps * dv), lambda bi, hi: (bi, 0, first // hps + hi))

    return pl.pallas_call(
        functools.partial(_diff_attn_kernel, lambda_init=lambda_init),
        grid=(b, h // hps),
        in_specs=[
            head_block(0),
            head_block(0),
            head_block(h),
            pl.BlockSpec(lam_params.shape, lambda bi, hi: (0, 0)),
            pl.BlockSpec((dv, 1), lambda bi, hi: (0, 0)),
        ],
        out_specs=head_block(0),
        out_shape=jax.ShapeDtypeStruct((b, s, h * dv), BF16),
        scratch_shapes=[
            pltpu.VMEM((hps, n_q, dv, 2 * tq), BF16),
            pltpu.VMEM((hps, s // tk, dv_aug, tk), BF16),
            pltpu.VMEM(strips, F32),
            pltpu.VMEM(strips, BF16),
            pltpu.VMEM((hps, 2, 1, 2 * tq), F32),
            pltpu.VMEM((hps, n_q, 1, 2 * tq), F32),
            pltpu.VMEM((hps, n_q, dv_aug, 2 * tq), F32),
        ],
        compiler_params=_params(("parallel", "parallel"), vmem),
        name="diff_attention",
    )(q3d, kv3d, kv3d, lam_params, g_sub_col)


def kernel(x, a_norm_pre, a_norm_post, a_w_in, a_w_out, kv_norm, w_kv, b_norm_pre, b_norm_post,
           b_w_q, b_lambda, b_subln, b_w_out, ffn_norm_pre, ffn_norm_post, ffn_w_up, ffn_conv_w,
           ffn_conv_b, ffn_w_down):
    b, s, d = x.shape
    t = b * s

    half = RET_QK_DIM // 2
    inv = 1.0 / (ROPE_BASE ** jnp.linspace(0.0, 1.0, half, dtype=F32))
    ang = jnp.arange(s).astype(F32)[:, None] * inv[None, :]
    cos, sin = jnp.cos(ang), jnp.sin(ang)
    log_gamma = jnp.log1p(-jnp.power(2.0, -5.0 - jnp.arange(RET_HEADS, dtype=F32)))

    y = _retention(x, a_norm_pre[0][None], a_w_in[0].astype(BF16), cos, sin, log_gamma)
    x = _post(y.reshape(t, -1), a_w_out[0].astype(BF16), a_norm_post[0][None],
              x.reshape(t, d), "ret_out_proj")
    x = _conv_ffn(x.reshape(b, s, d), ffn_norm_pre[0][None], ffn_w_up[0], ffn_conv_w[0],
                  ffn_conv_b[0], ffn_w_down[0], ffn_norm_post[0][None], "conv_ffn_0")

    layer = 1
    lambda_init = 0.8 - 0.6 * math.exp(-0.3 * layer)
    kv, q = _kvq(x.reshape(t, d), kv_norm[None], b_norm_pre[0][None],
                 w_kv.astype(BF16), b_w_q[0].astype(BF16))
    o = _diff_attention(q.reshape(b, s, -1), kv.reshape(b, s, -1), b_lambda[0],
                        b_subln[0][:, None], lambda_init)
    x = _post(o.reshape(t, -1), b_w_out[0].astype(BF16), b_norm_post[0][None],
              x.reshape(t, d), "attn_out_proj")
    x = _conv_ffn(x.reshape(b, s, d), ffn_norm_pre[1][None], ffn_w_up[1], ffn_conv_w[1],
                  ffn_conv_b[1], ffn_w_down[1], ffn_norm_post[1][None], "conv_ffn_1")
    return x
```

```python
import functools
import math

import jax
import jax.numpy as jnp
from jax import lax
from jax.experimental import pallas as pl
from jax.experimental.pallas import tpu as pltpu

F32 = jnp.float32
BF16 = jnp.bfloat16

D_MODEL = 1024
RET_HEADS = 4
RET_QK_DIM = D_MODEL // RET_HEADS
RET_V_DIM = 2 * RET_QK_DIM
RET_CHUNK = 256
DIFF_HEAD_DIM = 64
DIFF_HEADS = D_MODEL // (2 * DIFF_HEAD_DIM)
DIFF_V_DIM = 2 * DIFF_HEAD_DIM
D_FF = ((8 * D_MODEL // 3 + 127) // 128) * 128
ROPE_BASE = 10000.0
EPS = 1e-6
MASK_VALUE = -1e30
LOG2_E = math.log2(math.e)

BF16_ROWS = 16
F32_ROWS = 8
LANES = 128
MXU_COLS = 256
MIB = 1024 * 1024

RET_PROJ_ROWS = 256
POST_TM = 1024
KVQ_TM = 1024
FFN_TM = 512
FFN_TILES = 1
FFN_TF = 256
ATT_TQ = 512
ATT_TK = 256
ATT_HEADS = 1


def _params(semantics, vmem_bytes):
    return pltpu.CompilerParams(dimension_semantics=semantics,
                                vmem_limit_bytes=int(vmem_bytes))


def _resident(shape):
    zeros = (0,) * len(shape)
    return pl.BlockSpec(shape, lambda *_: zeros, pipeline_mode=pl.Buffered(1))


def _rms_scale(v):
    return lax.rsqrt(jnp.mean(v * v, axis=-1, keepdims=True) + EPS)


def _retention_kernel(lg_ref, x_ref, gain_ref, wq_ref, wk_ref, wv_ref, wg_ref, cos_ref, sin_ref,
                      o_ref, q_ref, k_ref, v_ref, sg_ref, state_ref):
    c_len, p_len = RET_CHUNK, RET_PROJ_ROWS
    seq = x_ref.shape[1]
    n_chunks = seq // c_len
    half = RET_QK_DIM // 2

    lg = lg_ref[pl.program_id(1)]
    row = lax.broadcasted_iota(jnp.int32, (c_len, c_len), 0)
    col = lax.broadcasted_iota(jnp.int32, (c_len, c_len), 1)
    rel = (row - col).astype(F32)
    decay_mask = jnp.where(rel >= 0, jnp.exp(lg * jnp.maximum(rel, 0.0)), 0.0)
    idx = lax.broadcasted_iota(jnp.int32, (c_len, 1), 0).astype(F32)
    q_decay = jnp.exp(lg * (idx + 1.0))
    k_decay = jnp.exp(lg * (c_len - 1.0 - idx))
    chunk_decay = jnp.exp(lg * jnp.full((1, 1), c_len, F32))

    def project(r0):
        rows = slice(r0, r0 + p_len)
        x = x_ref[0, rows, :]
        xn = (x * _rms_scale(x) * gain_ref[...]).astype(BF16)
        cos = cos_ref[rows, :]
        sin = sin_ref[rows, :]

        def rotary(a, scale, dst_ref):
            x1, x2 = a[:, :half], a[:, half:]
            dst_ref[rows, :half] = ((x1 * cos - x2 * sin) * scale).astype(BF16)
            dst_ref[rows, half:] = ((x1 * sin + x2 * cos) * scale).astype(BF16)

        rotary(jnp.dot(xn, wq_ref[...], preferred_element_type=F32), 1.0, q_ref)
        rotary(jnp.dot(xn, wk_ref[...], preferred_element_type=F32), RET_QK_DIM ** -0.5, k_ref)
        v_ref[rows, :] = jnp.dot(xn, wv_ref[...], preferred_element_type=F32).astype(BF16)
        half_g = 0.5 * jnp.dot(xn, wg_ref[...], preferred_element_type=F32)
        sg_ref[rows, :] = (half_g * (1.0 + jnp.tanh(half_g))).astype(BF16)

    def rows(c):
        return slice(c * c_len, (c + 1) * c_len)

    def scores(c):
        return lax.dot_general(q_ref[rows(c), :], k_ref[rows(c), :],
                               (((1,), (1,)), ((), ())), preferred_element_type=F32)

    chunks_per_tile = p_len // c_len
    project(0)
    s_next = scores(0)
    for c in range(n_chunks):
        next_tile = (c // chunks_per_tile + 1) * p_len
        if c % chunks_per_tile == 0 and next_tile < seq:
            project(next_tile)
        q = q_ref[rows(c), :]
        k = k_ref[rows(c), :]
        v = v_ref[rows(c), :]
        s_masked = (s_next * decay_mask).astype(BF16)
        out = None
        if c > 0:
            out = jnp.dot(q, state_ref[...].astype(BF16),
                          preferred_element_type=F32) * q_decay
        if c + 1 < n_chunks:
            s_next = scores(c + 1)
            kd = (k.astype(F32) * k_decay).astype(BF16)
            update = lax.dot_general(kd, v, (((0,), (0,)), ((), ())),
                                     preferred_element_type=F32)
        inner = jnp.dot(s_masked, v, preferred_element_type=F32)
        if c + 1 < n_chunks:
            state_ref[...] = update if c == 0 else state_ref[...] * chunk_decay + update
        out = inner if out is None else inner + out
        gate = sg_ref[rows(c), :].astype(F32)
        o_ref[0, rows(c), :] = (out * _rms_scale(out) * gate).astype(BF16)


def _retention(x3d, gain, w_in, cos, sin, log_gamma):
    b, s, d = x3d.shape
    dk, dv, h = RET_QK_DIM, RET_V_DIM, RET_HEADS
    assert RET_PROJ_ROWS % RET_CHUNK == 0 and s % RET_PROJ_ROWS == 0
    k_blk0 = h * dk // dk
    v_blk0 = 2 * h * dk // dv
    g_blk0 = (2 * h * dk + h * dv) // dv
    vmem = (2 * s * d * 4 + 2 * d * (2 * dk + 2 * dv) * 2 + 2 * s * dk * 4 + 2 * s * dv * 2
            + s * (2 * dk + 2 * dv) * 2 + dk * dv * 4
            + 8 * RET_PROJ_ROWS * dv * 4 + 3 * RET_PROJ_ROWS * d * 4 + 4 * MIB)
    return pl.pallas_call(
        _retention_kernel,
        grid=(b, h),
        in_specs=[
            pl.BlockSpec(memory_space=pltpu.SMEM),
            pl.BlockSpec((1, s, d), lambda i, j: (i, 0, 0)),
            pl.BlockSpec((1, d), lambda i, j: (0, 0)),
            pl.BlockSpec((d, dk), lambda i, j: (0, j)),
            pl.BlockSpec((d, dk), lambda i, j: (0, k_blk0 + j)),
            pl.BlockSpec((d, dv), lambda i, j: (0, v_blk0 + j)),
            pl.BlockSpec((d, dv), lambda i, j: (0, g_blk0 + j)),
            _resident((s, dk // 2)),
            _resident((s, dk // 2)),
        ],
        out_specs=pl.BlockSpec((1, s, dv), lambda i, j: (i, 0, j)),
        out_shape=jax.ShapeDtypeStruct((b, s, h * dv), BF16),
        scratch_shapes=[
            pltpu.VMEM((s, dk), BF16),
            pltpu.VMEM((s, dk), BF16),
            pltpu.VMEM((s, dv), BF16),
            pltpu.VMEM((s, dv), BF16),
            pltpu.VMEM((dk, dv), F32),
        ],
        compiler_params=_params(("parallel", "parallel"), vmem),
        name="retention",
    )(log_gamma, x3d, gain, w_in, w_in, w_in, w_in, cos, sin)


def _post_kernel(y_ref, w_ref, g_ref, x_ref, o_ref):
    h = jnp.dot(y_ref[...], w_ref[...], preferred_element_type=F32)
    o_ref[...] = x_ref[...] + h * _rms_scale(h) * g_ref[...]


def _post(y2d, w, gain, x2d, name):
    t, kdim = y2d.shape
    d = w.shape[1]
    vmem = (2 * POST_TM * kdim * 2 + kdim * d * 2 + 4 * POST_TM * d * 4
            + 2 * POST_TM * d * 4 + 4 * MIB)
    return pl.pallas_call(
        _post_kernel,
        grid=(t // POST_TM,),
        in_specs=[
            pl.BlockSpec((POST_TM, kdim), lambda i: (i, 0)),
            _resident((kdim, d)),
            pl.BlockSpec((1, d), lambda i: (0, 0)),
            pl.BlockSpec((POST_TM, d), lambda i: (i, 0)),
        ],
        out_specs=pl.BlockSpec((POST_TM, d), lambda i: (i, 0)),
        out_shape=jax.ShapeDtypeStruct((t, d), F32),
        compiler_params=_params(("parallel",), vmem),
        name=name,
    )(y2d, w, gain, x2d)


def _ffn_kernel(x_ref, gpre_ref, wup_ref, cw_ref, wdn_ref, gpost_ref, o_ref,
                xn_ref, h0_ref, h1_ref, act_ref, tail_ref):
    n_tiles, tm, f_dim = act_ref.shape
    tf = h0_ref.shape[2]
    n_f = f_dim // tf
    halo = F32_ROWS
    gpre = gpre_ref[...]

    def rows(k):
        return slice(k * tm, (k + 1) * tm)

    def normalize(k):
        x = x_ref[0, rows(k), :]
        xn_ref[rows(k), :] = (x * _rms_scale(x) * gpre).astype(BF16)

    @pl.when(pl.program_id(1) == 0)
    def _():
        tail_ref[...] = jnp.zeros_like(tail_ref)

    def up_project(k, f, h_ref):
        xn = xn_ref[rows(k), :]
        for half in range(2):
            col0 = half * f_dim + f * tf
            res = jnp.dot(xn, wup_ref[:, col0:col0 + tf], preferred_element_type=F32)
            h_ref[half, :halo, :] = tail_ref[2 * f + half]
            h_ref[half, halo:, :] = res
            tail_ref[2 * f + half] = res[tm - halo:, :]

    def conv(h_ref, half, f):
        col0 = half * f_dim + f * tf
        cw = cw_ref[:, col0:col0 + tf]
        rows_back = lambda back: pl.ds(halo - back, tm)
        return (cw[3:4] + cw[2:3] * h_ref[half, rows_back(0), :]
                + cw[1:2] * h_ref[half, rows_back(1), :] + cw[0:1] * h_ref[half, rows_back(2), :])

    def gate(k, f, h_ref):
        half_g = conv(h_ref, 0, f)
        u = conv(h_ref, 1, f)
        act_ref[k, :, f * tf:(f + 1) * tf] = (half_g * (1.0 + jnp.tanh(half_g)) * u).astype(BF16)

    def finish(k):
        h = jnp.dot(act_ref[k], wdn_ref[...], preferred_element_type=F32)
        o_ref[0, rows(k), :] = x_ref[0, rows(k), :] + h * _rms_scale(h) * gpost_ref[...]

    h_refs = (h0_ref, h1_ref)
    order = [(k, f) for k in range(n_tiles) for f in range(n_f)]
    normalize(0)
    up_project(*order[0], h_refs[0])
    for n, (k, f) in enumerate(order):
        if n + 1 < len(order):
            up_project(*order[n + 1], h_refs[(n + 1) % 2])
        if f == 1 and k + 1 < n_tiles:
            normalize(k + 1)
        gate(k, f, h_refs[n % 2])
        if f == n_f - 1:
            finish(k)


def _conv_ffn(x3d, gpre, w_up, conv_w, conv_b, w_down, gpost, name):
    b, s, d = x3d.shape
    f_dim = w_down.shape[0]
    assert f_dim % FFN_TF == 0
    tm, halo, n_tiles = FFN_TM, F32_ROWS, FFN_TILES
    assert n_tiles == 1
    rows = n_tiles * tm
    wup = w_up.astype(BF16)
    wdn = w_down.astype(BF16)
    cw = jnp.concatenate([conv_w, conv_b[None, :], jnp.zeros((4, 2 * f_dim), F32)], axis=0)
    cw = cw * jnp.where(jnp.arange(2 * f_dim) < f_dim, 0.5, 1.0)[None, :]
    vmem = (2 * f_dim * d * 2 + f_dim * d * 2 + 4 * rows * d * 4 + (rows + halo) * d * 2
            + rows * f_dim * 2 + 4 * (tm + halo) * FFN_TF * 4 + 6 * tm * d * 4 + 4 * MIB)
    return pl.pallas_call(
        _ffn_kernel,
        grid=(b, s // rows),
        in_specs=[
            pl.BlockSpec((1, rows, d), lambda i, j: (i, j, 0)),
            pl.BlockSpec((1, d), lambda i, j: (0, 0)),
            _resident((d, 2 * f_dim)),
            _resident((8, 2 * f_dim)),
            _resident((f_dim, d)),
            pl.BlockSpec((1, d), lambda i, j: (0, 0)),
        ],
        out_specs=pl.BlockSpec((1, rows, d), lambda i, j: (i, j, 0)),
        out_shape=jax.ShapeDtypeStruct((b, s, d), F32),
        scratch_shapes=[
            pltpu.VMEM((rows, d), BF16),
            pltpu.VMEM((2, tm + halo, FFN_TF), F32),
            pltpu.VMEM((2, tm + halo, FFN_TF), F32),
            pltpu.VMEM((n_tiles, tm, f_dim), BF16),
            pltpu.VMEM((2 * f_dim // FFN_TF, halo, FFN_TF), F32),
        ],
        compiler_params=_params(("parallel", "arbitrary"), vmem),
        name=name,
    )(x3d, gpre, wup, cw, wdn, gpost)


def _kvq_kernel(x_ref, gkv_ref, gq_ref, wkv_ref, wq_ref, kv_ref, q_ref):
    x = x_ref[...]
    xr = x * _rms_scale(x)
    kv = jnp.dot((xr * gkv_ref[...]).astype(BF16), wkv_ref[...], preferred_element_type=F32)
    kv_ref[...] = kv.astype(BF16)
    q = jnp.dot((xr * gq_ref[...]).astype(BF16), wq_ref[...], preferred_element_type=F32)
    q_ref[...] = (q * (DIFF_HEAD_DIM ** -0.5 * LOG2_E)).astype(BF16)


def _kvq(x2d, g_kv, g_q, w_kv, w_q):
    t, d = x2d.shape
    n_kv, n_q = w_kv.shape[1], w_q.shape[1]
    tm = KVQ_TM
    vmem = (2 * tm * d * 4 + d * (n_kv + n_q) * 2 + 2 * tm * (n_kv + n_q) * 2
            + 6 * tm * d * 4 + 2 * tm * (n_kv + n_q) * 4 + 4 * MIB)
    return pl.pallas_call(
        _kvq_kernel,
        grid=(t // tm,),
        in_specs=[
            pl.BlockSpec((tm, d), lambda i: (i, 0)),
            pl.BlockSpec((1, d), lambda i: (0, 0)),
            pl.BlockSpec((1, d), lambda i: (0, 0)),
            _resident((d, n_kv)),
            _resident((d, n_q)),
        ],
        out_specs=[pl.BlockSpec((tm, n_kv), lambda i: (i, 0)),
                   pl.BlockSpec((tm, n_q), lambda i: (i, 0))],
        out_shape=[jax.ShapeDtypeStruct((t, n_kv), BF16),
                   jax.ShapeDtypeStruct((t, n_q), BF16)],
        compiler_params=_params(("parallel",), vmem),
        name="kv_q_proj",
    )(x2d, g_kv, g_q, w_kv, w_q)


def _diff_attn_kernel(q_ref, k_ref, v_ref, lam_ref, gsub_ref, o_ref,
                      q2t_ref, vt_ref, s_ref, p_ref, a_ref, m_ref, acc_ref, *, lambda_init):
    tq, tk, d, dv = ATT_TQ, ATT_TK, DIFF_HEAD_DIM, DIFF_V_DIM
    strip = LANES
    n_q = q2t_ref.shape[1]
    n_kv = vt_ref.shape[1]
    n_pieces = 2 * tq // MXU_COLS
    strips_per_piece = MXU_COLS // strip

    def head(h):
        return slice(h * dv, (h + 1) * dv)

    for h in range(ATT_HEADS):
        for j in range(n_kv):
            vt_ref[h, j, :dv, :] = (
                v_ref[0, j * tk:(j + 1) * tk, head(h)].astype(F32).T.astype(BF16))
    vt_ref[:, :, dv:, :] = jnp.ones((ATT_HEADS, n_kv, BF16_ROWS, tk), BF16)

    for h in range(ATT_HEADS):
        for i in range(n_q):
            qt = q_ref[0, i * tq:(i + 1) * tq, head(h)].astype(F32).T
            feat = lax.broadcasted_iota(jnp.int32, qt.shape, 0)
            q2t_ref[h, i, :, :tq] = jnp.where(feat < d, qt, 0.0).astype(BF16)
            q2t_ref[h, i, :, tq:] = jnp.where(feat >= d, qt, 0.0).astype(BF16)

    lp = lam_ref[...]
    lam = (jnp.exp(jnp.sum(lp[0:1] * lp[1:2], axis=-1, keepdims=True))
           - jnp.exp(jnp.sum(lp[2:3] * lp[3:4], axis=-1, keepdims=True)) + lambda_init)

    tiles = [(i, j) for i in range(n_q) for j in range((i + 1) * tq // tk)]

    def piece(r):
        return slice(r * MXU_COLS, (r + 1) * MXU_COLS)

    def visibility(i, j, c):
        q_first = i * tq + (c * strip) % tq
        k_first = j * tk
        if k_first > q_first + strip - 1:
            return "none"
        return "some" if k_first + tk - 1 > q_first else "all"

    def piece_visible(i, j, r):
        seen = [visibility(i, j, c) != "none"
                for c in range(r * strips_per_piece, (r + 1) * strips_per_piece)]
        assert all(seen) or not any(seen)
        return seen[0]

    def scores(h, t, r):
        i, j = tiles[t]
        if not piece_visible(i, j, r):
            return
        res = jnp.dot(k_ref[0, j * tk:(j + 1) * tk, head(h)], q2t_ref[h, i, :, piece(r)],
                      preferred_element_type=F32)
        for c in range(strips_per_piece):
            s_ref[h, t % 2, r * strips_per_piece + c] = res[:, c * strip:(c + 1) * strip]

    def softmax(h, t, r):
        i, j = tiles[t]
        if not piece_visible(i, j, r):
            return
        for c in range(r * strips_per_piece, (r + 1) * strips_per_piece):
            lanes = slice(c * strip, (c + 1) * strip)
            s = s_ref[h, t % 2, c]
            if visibility(i, j, c) == "some":
                rel = (lax.broadcasted_iota(jnp.int32, (tk, strip), 1)
                       - lax.broadcasted_iota(jnp.int32, (tk, strip), 0))
                s = jnp.where(rel >= j * tk - i * tq - (c * strip) % tq, s, MASK_VALUE)
            m_new = jnp.max(s, axis=0, keepdims=True)
            if j > 0:
                m_old = m_ref[h, i, :, lanes]
                m_new = jnp.maximum(m_old, m_new)
                a_ref[h, t % 2, :, lanes] = jnp.exp2(m_old - m_new)
            m_ref[h, i, :, lanes] = m_new
            p_ref[h, t % 2, c] = jnp.exp2(s - m_new).astype(BF16)

    def accumulate(h, t, r):
        i, j = tiles[t]
        if not piece_visible(i, j, r):
            return
        p = jnp.concatenate([p_ref[h, t % 2, r * strips_per_piece + c]
                             for c in range(strips_per_piece)], axis=1)
        pv = jnp.dot(vt_ref[h, j], p, preferred_element_type=F32)
        if j > 0:
            pv = a_ref[h, t % 2, :, piece(r)] * acc_ref[h, i, :, piece(r)] + pv
        acc_ref[h, i, :, piece(r)] = pv

    def finalize(i):
        for h in range(ATT_HEADS):
            inv_l = 1.0 / acc_ref[h, i, dv:dv + 1, :]
            o = (acc_ref[h, i, :dv, :tq] * inv_l[:, :tq]
                 - lam * (acc_ref[h, i, :dv, tq:] * inv_l[:, tq:]))
            r = lax.rsqrt(jnp.mean(o * o, axis=0, keepdims=True) + EPS)
            o = o * r * gsub_ref[...] * (1.0 - lambda_init)
            o_ref[0, i * tq:(i + 1) * tq, head(h)] = o.T.astype(BF16)

    heads = range(ATT_HEADS)
    for r in range(n_pieces):
        for h in heads:
            scores(h, 0, r)
    for t in range(len(tiles)):
        for r in range(n_pieces):
            for h in heads:
                if t + 1 < len(tiles):
                    scores(h, t + 1, r)
                softmax(h, t, r)
                if t > 0:
                    accumulate(h, t - 1, r)
        if t > 0 and tiles[t][0] != tiles[t - 1][0]:
            finalize(tiles[t - 1][0])
    for r in range(n_pieces):
        for h in heads:
            accumulate(h, len(tiles) - 1, r)
    finalize(n_q - 1)


def _diff_attention(q3d, kv3d, lam_params, g_sub_col, lambda_init):
    b, s, _ = q3d.shape
    h, dv = DIFF_HEADS, DIFF_V_DIM
    tq, tk = ATT_TQ, ATT_TK
    assert tq % tk == 0 and s % tq == 0 and h % ATT_HEADS == 0
    hps = ATT_HEADS
    dv_aug = dv + BF16_ROWS
    n_q = s // tq
    strips = (hps, 2, 2 * tq // LANES, tk, LANES)
    vmem = hps * (2 * 4 * s * dv * 2 + 2 * s * dv * 2 + s * dv_aug * 2
                  + 2 * tk * 2 * tq * (4 + 2) + n_q * dv_aug * 2 * tq * 4
                  + 4 * tk * 2 * tq * 4) + 8 * MIB

    def head_block(first):
        return pl.BlockSpec((1, s, hps * dv), lambda bi, hi: (bi, 0, first // hps + hi))

    return pl.pallas_call(
        functools.partial(_diff_attn_kernel, lambda_init=lambda_init),
        grid=(b, h // hps),
        in_specs=[
            head_block(0),
            head_block(0),
            head_block(h),
            pl.BlockSpec(lam_params.shape, lambda bi, hi: (0, 0)),
            pl.BlockSpec((dv, 1), lambda bi, hi: (0, 0)),
        ],
        out_specs=head_block(0),
        out_shape=jax.ShapeDtypeStruct((b, s, h * dv), BF16),
        scratch_shapes=[
            pltpu.VMEM((hps, n_q, dv, 2 * tq), BF16),
            pltpu.VMEM((hps, s // tk, dv_aug, tk), BF16),
            pltpu.VMEM(strips, F32),
            pltpu.VMEM(strips, BF16),
            pltpu.VMEM((hps, 2, 1, 2 * tq), F32),
            pltpu.VMEM((hps, n_q, 1, 2 * tq), F32),
            pltpu.VMEM((hps, n_q, dv_aug, 2 * tq), F32),
        ],
        compiler_params=_params(("parallel", "parallel"), vmem),
        name="diff_attention",
    )(q3d, kv3d, kv3d, lam_params, g_sub_col)


def kernel(x, a_norm_pre, a_norm_post, a_w_in, a_w_out, kv_norm, w_kv, b_norm_pre, b_norm_post,
           b_w_q, b_lambda, b_subln, b_w_out, ffn_norm_pre, ffn_norm_post, ffn_w_up, ffn_conv_w,
           ffn_conv_b, ffn_w_down):
    b, s, d = x.shape
    t = b * s

    half = RET_QK_DIM // 2
    inv = 1.0 / (ROPE_BASE ** jnp.linspace(0.0, 1.0, half, dtype=F32))
    ang = jnp.arange(s).astype(F32)[:, None] * inv[None, :]
    cos, sin = jnp.cos(ang), jnp.sin(ang)
    log_gamma = jnp.log1p(-jnp.power(2.0, -5.0 - jnp.arange(RET_HEADS, dtype=F32)))

    y = _retention(x, a_norm_pre[0][None], a_w_in[0].astype(BF16), cos, sin, log_gamma)
    x = _post(y.reshape(t, -1), a_w_out[0].astype(BF16), a_norm_post[0][None],
              x.reshape(t, d), "ret_out_proj")
    x = _conv_ffn(x.reshape(b, s, d), ffn_norm_pre[0][None], ffn_w_up[0], ffn_conv_w[0],
                  ffn_conv_b[0], ffn_w_down[0], ffn_norm_post[0][None], "conv_ffn_0")

    layer = 1
    lambda_init = 0.8 - 0.6 * math.exp(-0.3 * layer)
    kv, q = _kvq(x.reshape(t, d), kv_norm[None], b_norm_pre[0][None],
                 w_kv.astype(BF16), b_w_q[0].astype(BF16))
    o = _diff_attention(q.reshape(b, s, -1), kv.reshape(b, s, -1), b_lambda[0],
                        b_subln[0][:, None], lambda_init)
    x = _post(o.reshape(t, -1), b_w_out[0].astype(BF16), b_norm_post[0][None],
              x.reshape(t, d), "attn_out_proj")
    x = _conv_ffn(x.reshape(b, s, d), ffn_norm_pre[1][None], ffn_w_up[1], ffn_conv_w[1],
                  ffn_conv_b[1], ffn_w_down[1], ffn_norm_post[1][None], "conv_ffn_1")
    return x
```
